```python
import math
import jax, jax.numpy as jnp
from jax import lax
import numpy as np

D_MODEL = 1024
BATCH = 16
SEQ = 2048
DEPTH = 2

GRID_W = 64
CTX_LEN = 256

D_HY = D_MODEL // 4
D_HG = D_MODEL // 2
D_FN = D_MODEL // 4
D_MIX = D_HY + D_HG + D_FN

HY_ORDER = 2
HY_SHORT_CONV = 3
HY_BANDS = 16
HY_EMB_DIM = 1 + 2 * HY_BANDS
HY_FILTER_HIDDEN = 64
HY_DECAY_TARGET = 1e-2
HY_FAST_DECAY_PCT = 0.3
HY_SLOW_DECAY_PCT = 1.5
D_HY_PROJ = (HY_ORDER + 1) * D_HY

HG_HEAD_DIM = 128
HG_HEADS = D_HG // HG_HEAD_DIM
HG_CHUNK = 64
HG_PROJ = 5 * D_HG

FN_GROUPS = 4
FN_GROUP_DIM = D_FN // FN_GROUPS

D_PROJ = D_HY_PROJ + HG_PROJ + D_FN

N_EXPERTS = 16
N_GROUPS = 4
EXPERTS_PER_GROUP = N_EXPERTS // N_GROUPS
TOP_K = 2
D_EXPERT = 512

N_MOD = 6
RMS_EPS = 1e-6
POS_BASE = 10000.0

kernel_name = "hybrid_hyena_hgrn2_fnet_moe_dit"


def rmsnorm(x, g):
    xf = x.astype(jnp.float32)
    y = xf * lax.rsqrt(jnp.mean(xf * xf, axis=-1, keepdims=True) + RMS_EPS)
    return (y * g.astype(jnp.float32)).astype(x.dtype)


def modulate(h, shift, scale):
    return h * (1 + scale) + shift


def ada_modulation(cond, w, b):
    return jnp.split(jax.nn.silu(cond) @ w + b, N_MOD, axis=-1)


def grid_sincos(n_tok):
    rows = n_tok // GRID_W
    row = jnp.repeat(jnp.arange(rows, dtype=jnp.float32), GRID_W)
    col = jnp.tile(jnp.arange(GRID_W, dtype=jnp.float32), rows)
    quarter = D_MODEL // 4
    omega = 1.0 / (POS_BASE ** (jnp.arange(quarter, dtype=jnp.float32) / quarter))
    ar = row[:, None] * omega
    ac = col[:, None] * omega
    return jnp.concatenate([jnp.sin(ar), jnp.cos(ar), jnp.sin(ac), jnp.cos(ac)], axis=-1)


def short_conv3(u, w, b):
    up = jnp.pad(u, ((0, 0), (1, 1), (0, 0)))
    return up[:, :-2] * w[0] + up[:, 1:-1] * w[1] + up[:, 2:] * w[2] + b


def hyena_deltas():
    max_decay = math.log(HY_DECAY_TARGET) / HY_FAST_DECAY_PCT
    min_decay = math.log(HY_DECAY_TARGET) / HY_SLOW_DECAY_PCT
    return jnp.linspace(min_decay, max_decay, D_HY, dtype=jnp.float32)


def hyena_filters(L, w1, b1, w2, b2, w3, freq):
    f32 = jnp.float32
    t = jnp.linspace(0.0, 1.0, L, dtype=f32)[:, None]
    w = 2.0 * math.pi * jnp.arange(L, dtype=f32)[:, None] / L
    bands = jnp.linspace(1e-4, HY_BANDS - 1, HY_BANDS, dtype=f32)[None, :]
    z = jnp.concatenate([t, jnp.cos(bands * w), -jnp.sin(bands * w)], axis=-1)
    fr = freq.astype(f32)
    h = jnp.sin(fr * (z @ w1.astype(f32) + b1.astype(f32)))
    h = jnp.sin(fr * (h @ w2.astype(f32) + b2.astype(f32)))
    h = (h @ w3.astype(f32)).reshape(L, 2 * HY_ORDER, D_HY)
    decay = jnp.exp(-t[:, :, None] * jnp.abs(hyena_deltas()))
    return h * decay


def bidir_fftconv(u, h_fwd, h_bwd, skip):
    L, C = h_fwd.shape
    k = jnp.concatenate([h_fwd, jnp.zeros((1, C), jnp.float32), h_bwd[1:][::-1]], axis=0)
    uf = jnp.fft.rfft(u.astype(jnp.float32), n=2 * L, axis=1)
    kf = jnp.fft.rfft(k, n=2 * L, axis=0)
    y = jnp.fft.irfft(uf * kf[None], n=2 * L, axis=1)[:, :L]
    return (y + u.astype(jnp.float32) * skip.astype(jnp.float32)).astype(u.dtype)


def hyena_mixer(z, conv_w, conv_b, w1, b1, w2, b2, w3, freq, bias, norm_g):
    L = z.shape[1]
    u = short_conv3(z, conv_w, conv_b)
    v, x1, x2 = jnp.split(u, HY_ORDER + 1, axis=-1)
    h = hyena_filters(L, w1, b1, w2, b2, w3, freq)
    y = x1 * bidir_fftconv(v, h[:, 0], h[:, 1], bias[0])
    y = x2 * bidir_fftconv(y, h[:, 2], h[:, 3], bias[1])
    return rmsnorm(y, norm_g)


def fourier_mixer(z, w, b, norm_g):
    B, L, _ = z.shape
    zg = z.astype(jnp.float32).reshape(B, L, FN_GROUPS, FN_GROUP_DIM)
    y = jnp.fft.fftn(zg, axes=(1, 3), norm="ortho").real
    y = jnp.einsum("blgc,gcd->blgd", y, w.astype(jnp.float32)).reshape(B, L, D_FN) + b.astype(jnp.float32)
    return rmsnorm(y.astype(z.dtype), norm_g)


def hgrn_lower_bounds(raw):
    cs = jnp.cumsum(jax.nn.softmax(raw.astype(jnp.float32), axis=0), axis=0)
    return cs - cs[0:1]


def hgrn2_inputs(z, lb):
    B, L, _ = z.shape
    q, f_fwd, f_bwd, i, g = jnp.split(z.astype(jnp.float32), 5, axis=-1)
    heads = lambda t: t.reshape(B, L, HG_HEADS, HG_HEAD_DIM)
    q = heads(jax.nn.silu(q)) * (HG_HEAD_DIM ** -0.5)
    dirs = []
    for d, f_raw in enumerate((f_fwd, f_bwd)):
        lbd = lb[d]
        forget = lbd + (1.0 - lbd) * jax.nn.sigmoid(f_raw)
        dirs.append((heads(1.0 - forget), heads(jnp.log(forget))))
    return q, heads(i), g, dirs


def to_chunks(t):
    B, L, H, d = t.shape
    return t.reshape(B, L // HG_CHUNK, HG_CHUNK, H, d).transpose(1, 0, 3, 2, 4)


def from_chunks(t):
    n, B, H, C, d = t.shape
    return t.transpose(1, 0, 3, 2, 4).reshape(B, n * C, H, d)


def gla_chunk_scan(q, k, v, logf, s0):
    causal = jnp.tril(jnp.ones((HG_CHUNK, HG_CHUNK), dtype=bool))

    def step(s, inp):
        qc, kc, vc, gc = inp
        b = jnp.cumsum(gc, axis=2)
        o_inter = jnp.einsum("bhtd,bhdv->bhtv", qc * jnp.exp(b), s)
        diff = b[:, :, :, None, :] - b[:, :, None, :, :]
        decay = jnp.exp(jnp.where(causal[:, :, None], diff, -jnp.inf))
        scores = jnp.einsum("bhtd,bhsd,bhtsd->bhts", qc, kc, decay)
        o_intra = jnp.einsum("bhts,bhsv->bhtv", scores, vc)
        b_end = b[:, :, -1, :]
        s_new = jnp.exp(b_end)[..., None] * s + jnp.einsum(
            "bhsd,bhsv->bhdv", kc * jnp.exp(b_end[:, :, None, :] - b), vc)
        return s_new, o_inter + o_intra

    s_fin, o = lax.scan(step, s0, (to_chunks(q), to_chunks(k), to_chunks(v), to_chunks(logf)))
    return s_fin, from_chunks(o)


def hgrn2_bidir(q, v, dirs, s0_fwd, s0_bwd):
    (k_f, g_f), (k_b, g_b) = dirs
    s_f, o_f = gla_chunk_scan(q, k_f, v, g_f, s0_fwd)
    flip = lambda t: t[:, ::-1]
    s_b, o_b = gla_chunk_scan(flip(q), flip(k_b), flip(v), flip(g_b), s0_bwd)
    return s_f, s_b, o_f + flip(o_b)


def hgrn2_readout(o, g, norm_g):
    B, L, H, d = o.shape
    on = o * lax.rsqrt(jnp.mean(o * o, axis=-1, keepdims=True) + RMS_EPS)
    return on.reshape(B, L, D_HG) * norm_g.astype(jnp.float32) * jax.nn.silu(g)


def mixer_output(z, o_hg, g_hg, w_out, conv_w, conv_b, w1, b1, w2, b2, w3, freq,
                 hy_bias, hy_norm_g, hg_norm_g, fn_w, fn_b, fn_norm_g):
    y_hy = hyena_mixer(z[..., :D_HY_PROJ], conv_w, conv_b, w1, b1, w2, b2, w3, freq, hy_bias, hy_norm_g)
    y_hg = hgrn2_readout(o_hg, g_hg, hg_norm_g).astype(z.dtype)
    y_fn = fourier_mixer(z[..., D_HY_PROJ + HG_PROJ:], fn_w, fn_b, fn_norm_g)
    return jnp.concatenate([y_hy, y_hg, y_fn], axis=-1) @ w_out


def moe_ffn(h, router_w, router_b, w_gate, w_up, w_down):
    N = h.shape[0]
    probs = jax.nn.softmax(h.astype(jnp.float32) @ router_w.astype(jnp.float32), axis=-1)
    sel = probs + router_b.astype(jnp.float32)
    group_score = lax.top_k(sel.reshape(N, N_GROUPS, EXPERTS_PER_GROUP), TOP_K)[0].sum(-1)
    best_group = jnp.argmax(group_score, axis=-1)
    in_group = (jnp.arange(N_EXPERTS) // EXPERTS_PER_GROUP)[None, :] == best_group[:, None]
    _, idx = lax.top_k(jnp.where(in_group, sel, -jnp.inf), TOP_K)
    gp = jnp.take_along_axis(probs, idx, axis=-1)
    gp = gp / jnp.sum(gp, axis=-1, keepdims=True)
    combine = jnp.sum(jax.nn.one_hot(idx, N_EXPERTS, dtype=jnp.float32) * gp[..., None], axis=1)
    out = jnp.zeros(h.shape, jnp.float32)
    for e in range(N_EXPERTS):
        he = (jax.nn.silu(h @ w_gate[e]) * (h @ w_up[e])) @ w_down[e]
        out = out + combine[:, e:e + 1] * he.astype(jnp.float32)
    return out.astype(h.dtype)


def setup_inputs(seed: int = 0) -> dict:
    key = jax.random.key(seed)
    ks = jax.random.split(key, 32)
    f32 = jnp.float32
    nrm = lambda k, shape, s: s * jax.random.normal(k, shape, f32)
    D = D_MODEL
    return {
        "x": nrm(ks[0], (BATCH, SEQ, D), 1.0),
        "c": nrm(ks[1], (BATCH, D), 1.0),
        "ctx": nrm(ks[2], (BATCH, CTX_LEN, D), 1.0),
        "c_ctx": nrm(ks[3], (D,), 1.0),
        "ada_w": nrm(ks[4], (DEPTH, D, N_MOD * D), 0.5 * D ** -0.5),
        "ada_b": nrm(ks[5], (DEPTH, N_MOD * D), 0.02),
        "norm1_g": 1.0 + nrm(ks[6], (DEPTH, D), 0.1),
        "norm2_g": 1.0 + nrm(ks[7], (DEPTH, D), 0.1),
        "w_in": nrm(ks[8], (DEPTH, D, D_PROJ), D ** -0.5),
        "w_out": nrm(ks[9], (DEPTH, D_MIX, D), D_MIX ** -0.5),
        "hy_conv_w": nrm(ks[10], (DEPTH, HY_SHORT_CONV, D_HY_PROJ), 0.5),
        "hy_conv_b": nrm(ks[11], (DEPTH, D_HY_PROJ), 0.02),
        "hy_filt_w1": nrm(ks[12], (DEPTH, HY_EMB_DIM, HY_FILTER_HIDDEN), HY_EMB_DIM ** -0.5),
        "hy_filt_b1": nrm(ks[13], (DEPTH, HY_FILTER_HIDDEN), 0.02),
        "hy_filt_w2": nrm(ks[14], (DEPTH, HY_FILTER_HIDDEN, HY_FILTER_HIDDEN), HY_FILTER_HIDDEN ** -0.5),
        "hy_filt_b2": nrm(ks[15], (DEPTH, HY_FILTER_HIDDEN), 0.02),
        "hy_filt_w3": nrm(ks[16], (DEPTH, HY_FILTER_HIDDEN, 2 * HY_ORDER * D_HY), HY_FILTER_HIDDEN ** -0.5),
        "hy_filt_freq": 1.0 + nrm(ks[17], (DEPTH, HY_FILTER_HIDDEN), 0.1),
        "hy_bias": nrm(ks[18], (DEPTH, HY_ORDER, D_HY), 0.5),
        "hy_norm_g": 1.0 + nrm(ks[19], (DEPTH, D_HY), 0.1),
        "hg_lower_bounds": nrm(ks[20], (DEPTH, 2, D_HG), 1.0),
        "hg_norm_g": 1.0 + nrm(ks[21], (DEPTH, D_HG), 0.1),
        "fn_w": nrm(ks[22], (DEPTH, FN_GROUPS, FN_GROUP_DIM, FN_GROUP_DIM), FN_GROUP_DIM ** -0.5),
        "fn_b": nrm(ks[23], (DEPTH, D_FN), 0.02),
        "fn_norm_g": 1.0 + nrm(ks[24], (DEPTH, D_FN), 0.1),
        "router_w": nrm(ks[25], (D, N_EXPERTS), D ** -0.5),
        "router_b": nrm(ks[26], (N_EXPERTS,), 0.01),
        "moe_w_gate": nrm(ks[27], (DEPTH, N_EXPERTS, D, D_EXPERT), D ** -0.5),
        "moe_w_up": nrm(ks[28], (DEPTH, N_EXPERTS, D, D_EXPERT), D ** -0.5),
        "moe_w_down": nrm(ks[29], (DEPTH, N_EXPERTS, D_EXPERT, D), D_EXPERT ** -0.5),
        "final_norm_g": 1.0 + nrm(ks[30], (D,), 0.1),
    }


def reference(x, c, ctx, c_ctx, ada_w, ada_b, norm1_g, norm2_g, w_in, w_out, hy_conv_w, hy_conv_b,
              hy_filt_w1, hy_filt_b1, hy_filt_w2, hy_filt_b2, hy_filt_w3, hy_filt_freq, hy_bias, hy_norm_g,
              hg_lower_bounds, hg_norm_g, fn_w, fn_b, fn_norm_g, router_w, router_b,
              moe_w_gate, moe_w_up, moe_w_down, final_norm_g):
    B, L, D = x.shape
    n_ctx = ctx.shape[1]
    lower_bounds = hgrn_lower_bounds(hg_lower_bounds)
    x = x + grid_sincos(L).astype(x.dtype)[None]
    cond_x = c[:, None, :]
    cond_c = c_ctx[None, None, :]
    s_zero = jnp.zeros((B, HG_HEADS, HG_HEAD_DIM, HG_HEAD_DIM), jnp.float32)
    hg_lo, hg_hi = D_HY_PROJ, D_HY_PROJ + HG_PROJ
    for layer in range(DEPTH):
        last = layer == DEPTH - 1
        mod_x = ada_modulation(cond_x, ada_w[layer], ada_b[layer])
        mod_c = ada_modulation(cond_c, ada_w[layer], ada_b[layer])
        mix_params = (w_out[layer], hy_conv_w[layer], hy_conv_b[layer], hy_filt_w1[layer], hy_filt_b1[layer],
                      hy_filt_w2[layer], hy_filt_b2[layer], hy_filt_w3[layer], hy_filt_freq[layer],
                      hy_bias[layer], hy_norm_g[layer], hg_norm_g[layer], fn_w[layer], fn_b[layer], fn_norm_g[layer])
        hx = modulate(rmsnorm(x, norm1_g[layer]), mod_x[0], mod_x[1])
        hc = modulate(rmsnorm(ctx, norm1_g[layer]), mod_c[0], mod_c[1])
        zx = hx @ w_in[layer]
        zc = hc @ w_in[layer]
        q_c, v_c, g_c, dirs_c = hgrn2_inputs(zc[..., hg_lo:hg_hi], lower_bounds[layer])
        s_f, s_b, o_c = hgrn2_bidir(q_c, v_c, dirs_c, s_zero, s_zero)
        q_x, v_x, g_x, dirs_x = hgrn2_inputs(zx[..., hg_lo:hg_hi], lower_bounds[layer])
        _, _, o_x = hgrn2_bidir(q_x, v_x, dirs_x, s_f, s_b)
        x = x + mod_x[2] * mixer_output(zx, o_x, g_x, *mix_params)
        hx2 = modulate(rmsnorm(x, norm2_g[layer]), mod_x[3], mod_x[4])
        if last:
            ff_x = moe_ffn(hx2.reshape(B * L, D), router_w, router_b,
                           moe_w_gate[layer], moe_w_up[layer], moe_w_down[layer])
        else:
            ctx = ctx + mod_c[2] * mixer_output(zc, o_c, g_c, *mix_params)
            hc2 = modulate(rmsnorm(ctx, norm2_g[layer]), mod_c[3], mod_c[4])
            ff = moe_ffn(jnp.concatenate([hc2.reshape(B * n_ctx, D), hx2.reshape(B * L, D)], axis=0),
                         router_w, router_b, moe_w_gate[layer], moe_w_up[layer], moe_w_down[layer])
            ctx = ctx + mod_c[5] * ff[:B * n_ctx].reshape(B, n_ctx, D)
            ff_x = ff[B * n_ctx:]
        x = x + mod_x[5] * ff_x.reshape(B, L, D)
    return rmsnorm(x, final_norm_g)
```

```python
import functools
import math

import numpy as np
import jax
import jax.numpy as jnp
from jax import lax
from jax.experimental import pallas as pl
from jax.experimental.pallas import tpu as pltpu

F32 = jnp.float32
BF16 = jnp.bfloat16
HIGHEST = lax.Precision.HIGHEST

GRID_W = 64
HY_ORDER = 2
HY_BANDS = 16
HY_DECAY_TARGET = 1e-2
HY_FAST_DECAY_PCT = 0.3
HY_SLOW_DECAY_PCT = 1.5
HG_HEAD_DIM = 128
FN_GROUPS = 4
N_EXPERTS = 16
N_GROUPS = 4
EXPERTS_PER_GROUP = N_EXPERTS // N_GROUPS
N_MOD = 6
RMS_EPS = 1e-6
POS_BASE = 10000.0

V7X_VMEM_LIMIT_BYTES = 56 * 1024 * 1024
SCAN_CHUNK = 64
SCAN_SUB = 16
NEG_BIG = -1e30


def _cparams(sem):
    return pltpu.CompilerParams(dimension_semantics=sem, vmem_limit_bytes=V7X_VMEM_LIMIT_BYTES)


def _const_spec(shape):
    nd = len(shape)
    return pl.BlockSpec(shape, lambda *_: (0,) * nd, pipeline_mode=pl.Buffered(1))


def _silu(x):
    return x * jax.nn.sigmoid(x)


def _dot(a, b):
    return jnp.dot(a, b, preferred_element_type=F32)


def _dot_nt(a, b, precision=None):
    return lax.dot_general(a, b, (((1,), (1,)), ((), ())), precision=precision,
                           preferred_element_type=F32)


def _dot_tn(a, b):
    return lax.dot_general(a, b, (((0,), (0,)), ((), ())), preferred_element_type=F32)


def _ada_kernel(c_ref, w_ref, b_ref, o_ref):
    s = _silu(c_ref[...])
    o_ref[0] = jnp.dot(s, w_ref[0], precision=HIGHEST, preferred_element_type=F32) + b_ref[0]


def _ada_modulation(cond, ada_w, ada_b):
    depth, d, nd = ada_w.shape
    rows = cond.shape[0]
    tn = 1536
    return pl.pallas_call(
        _ada_kernel,
        out_shape=jax.ShapeDtypeStruct((depth, rows, nd), F32),
        grid=(depth, nd // tn),
        in_specs=[pl.BlockSpec((rows, d), lambda l, j: (0, 0)),
                  pl.BlockSpec((1, d, tn), lambda l, j: (l, 0, j)),
                  pl.BlockSpec((1, 1, tn), lambda l, j: (l, 0, j))],
        out_specs=pl.BlockSpec((1, rows, tn), lambda l, j: (l, 0, j)),
        compiler_params=_cparams(("parallel", "parallel")),
        name="ada_modulation",
    )(cond, ada_w, ada_b.reshape(depth, 1, nd))


def _inproj_kernel(*refs, add_pos, segs):
    if add_pos:
        x_ref, pos_ref, g_ref, sh_ref, sc_ref, w_ref = refs[:6]
        outs = refs[6:]
        x = x_ref[0] + pos_ref[...]
    else:
        x_ref, g_ref, sh_ref, sc_ref, w_ref = refs[:5]
        outs = refs[5:]
        x = x_ref[0]
    ms = jnp.mean(x * x, axis=-1, keepdims=True)
    h = x * lax.rsqrt(ms + RMS_EPS) * g_ref[...]
    h = (h * (1.0 + sc_ref[0]) + sh_ref[0]).astype(BF16)
    for o_ref, (a, b, _) in zip(outs, segs):
        o_ref[0] = _dot(h, w_ref[:, a:b]).astype(o_ref.dtype)


def _in_proj(x, pos, norm_g, mod, mod_row, w, segs, tm):
    bsz, seq, d = x.shape
    add_pos = pos is not None
    in_specs = [pl.BlockSpec((1, tm, d), lambda b, i: (b, i, 0))]
    args = [x]
    if add_pos:
        in_specs.append(pl.BlockSpec((tm, d), lambda b, i: (i, 0)))
        args.append(pos)
    in_specs += [pl.BlockSpec((1, d), lambda b, i: (0, 0)),
                 pl.BlockSpec((1, 1, d), lambda b, i: (mod_row(b), 0, 0)),
                 pl.BlockSpec((1, 1, d), lambda b, i: (mod_row(b), 0, 1)),
                 _const_spec(w.shape)]
    args += [norm_g.reshape(1, d), mod, mod, w]
    out_shape = [jax.ShapeDtypeStruct((bsz, seq, b - a), dt) for a, b, dt in segs]
    out_specs = [pl.BlockSpec((1, tm, b - a), lambda bb, i: (bb, i, 0)) for a, b, _ in segs]
    return pl.pallas_call(
        functools.partial(_inproj_kernel, add_pos=add_pos, segs=tuple(segs)),
        out_shape=out_shape, grid=(bsz, seq // tm), in_specs=in_specs, out_specs=out_specs,
        compiler_params=_cparams(("parallel", "parallel")),
        name="in_proj",
    )(*args)


def _cumsum_rows(x, reverse):
    c = x.shape[0]
    row = lax.broadcasted_iota(jnp.int32, x.shape, 0)
    sh = 1
    while sh < c:
        if reverse:
            x = x + jnp.where(row < c - sh, pltpu.roll(x, c - sh, 0), 0.0)
        else:
            x = x + jnp.where(row >= sh, pltpu.roll(x, sh, 0), 0.0)
        sh *= 2
    return x


def _scan_chunk(q, fraw, v, lb, st_ref, h, b_scr, q_scr, reverse):
    c = SCAN_CHUNK
    s = SCAN_SUB
    nsub = c // s
    forget = lb + (1.0 - lb) * jax.nn.sigmoid(fraw)
    k = 1.0 - forget
    b = _cumsum_rows(jnp.log(forget), reverse)
    qs = _silu(q) * (HG_HEAD_DIM ** -0.5)
    st = st_ref[h]
    o_inter = _dot_nt((qs * jnp.exp(b)).astype(BF16), st.astype(BF16))

    b_scr[...] = b
    q_scr[...] = qs
    v_bf = v.astype(BF16)
    ones = jnp.ones((HG_HEAD_DIM, HG_HEAD_DIM), BF16)
    sub_iota = lax.broadcasted_iota(jnp.int32, (s, HG_HEAD_DIM), 0)
    row_iota = lax.broadcasted_iota(jnp.int32, (c, HG_HEAD_DIM), 0)
    grp = (lax.broadcasted_iota(jnp.int32, (s, s * s), 1) // s ==
           lax.broadcasted_iota(jnp.int32, (s, s * s), 0)).astype(BF16)
    o_parts = []
    for i in range(nsub):
        r0 = i * s
        bi = b[r0:r0 + s]
        ki = k[r0:r0 + s]
        vi = v[r0:r0 + s]
        prods = []
        for t in range(s):
            bt = b_scr[pl.ds(r0 + t, 1), :]
            qt = q_scr[pl.ds(r0 + t, 1), :]
            keep = (sub_iota >= t) if reverse else (sub_iota <= t)
            e = jnp.exp(jnp.where(keep, bt - bi, NEG_BIG))
            prods.append(((qt * ki) * e).astype(BF16))
        p = jnp.concatenate(prods, axis=0)
        rsum = _dot(p, ones)
        zt = (rsum * jnp.concatenate([vi] * s, axis=0)).astype(BF16)
        o_i = _dot(grp, zt)
        if reverse and i < nsub - 1:
            ref_row = b[r0 + s:r0 + s + 1]
            key_rows = row_iota >= r0 + s
        elif (not reverse) and i > 0:
            ref_row = b[r0 - 1:r0]
            key_rows = row_iota < r0
        else:
            ref_row = None
        if ref_row is not None:
            qi = qs[r0:r0 + s] * jnp.exp(bi - ref_row)
            ks = k * jnp.exp(jnp.where(key_rows, ref_row - b, NEG_BIG))
            sc = _dot_nt(qi.astype(BF16), ks.astype(BF16))
            o_i = o_i + _dot(sc.astype(BF16), v_bf)
        o_parts.append(o_i)
    o = o_inter + jnp.concatenate(o_parts, axis=0)

    b_end = b[0:1] if reverse else b[c - 1:c]
    kd = (k * jnp.exp(b_end - b)).astype(BF16)
    st_ref[h] = st * jnp.exp(b_end) + _dot_tn(v_bf, kd)
    return o


def _scan_kernel(qf_ref, qb_ref, ff_ref, fb_ref, vf_ref, vb_ref, lb_ref, s0f_ref, s0b_ref,
                 of_ref, ob_ref, sf_ref, sb_ref, stf, stb, bf_scr, qf_scr, bb_scr, qb_scr, *, n_heads, n_chunks):
    n = pl.program_id(1)

    @pl.when(n == 0)
    def _():
        stf[...] = s0f_ref[0]
        stb[...] = s0b_ref[0]

    def head_body(h, carry):
        col = pl.multiple_of(h * HG_HEAD_DIM, HG_HEAD_DIM)
        cols = pl.ds(col, HG_HEAD_DIM)
        lbf = lb_ref[0:1, cols]
        lbb = lb_ref[1:2, cols]

        def chunk_body(ci, carry2):
            rf = pl.ds(pl.multiple_of(ci * SCAN_CHUNK, SCAN_CHUNK), SCAN_CHUNK)
            of_ref[0, rf, cols] = _scan_chunk(qf_ref[0, rf, cols], ff_ref[0, rf, cols], vf_ref[0, rf, cols],
                                              lbf, stf, h, bf_scr, qf_scr, False)
            rb = pl.ds(pl.multiple_of((n_chunks - 1 - ci) * SCAN_CHUNK, SCAN_CHUNK), SCAN_CHUNK)
            ob_ref[0, rb, cols] = _scan_chunk(qb_ref[0, rb, cols], fb_ref[0, rb, cols], vb_ref[0, rb, cols],
                                              lbb, stb, h, bb_scr, qb_scr, True)
            return carry2

        return lax.fori_loop(0, n_chunks, chunk_body, carry)

    lax.fori_loop(0, n_heads, head_body, 0)

    @pl.when(n == pl.num_programs(1) - 1)
    def _():
        sf_ref[0] = stf[...]
        sb_ref[0] = stb[...]


def _hgrn2_scan(z_hg, lb, s0f, s0b, tb):
    bsz, seq, w5 = z_hg.shape
    w = w5 // 5
    nh = w // HG_HEAD_DIM
    nb = seq // tb
    blk = (1, tb, w)
    fwd = lambda j: pl.BlockSpec(blk, lambda b, n: (b, n, j))
    bwd = lambda j: pl.BlockSpec(blk, lambda b, n: (b, nb - 1 - n, j))
    st_spec = pl.BlockSpec((1, nh, HG_HEAD_DIM, HG_HEAD_DIM), lambda b, n: (b, 0, 0, 0))
    st_shape = jax.ShapeDtypeStruct((bsz, nh, HG_HEAD_DIM, HG_HEAD_DIM), F32)
    o_shape = jax.ShapeDtypeStruct((bsz, seq, w), F32)
    return pl.pallas_call(
        functools.partial(_scan_kernel, n_heads=nh, n_chunks=tb // SCAN_CHUNK),
        out_shape=[o_shape, o_shape, st_shape, st_shape],
        grid=(bsz, nb),
        in_specs=[fwd(0), bwd(0), fwd(1), bwd(2), fwd(3), bwd(3),
                  pl.BlockSpec((2, w), lambda b, n: (0, 0)), st_spec, st_spec],
        out_specs=[fwd(0), bwd(0), st_spec, st_spec],
        scratch_shapes=[pltpu.VMEM((nh, HG_HEAD_DIM, HG_HEAD_DIM), F32),
                        pltpu.VMEM((nh, HG_HEAD_DIM, HG_HEAD_DIM), F32),
                        pltpu.VMEM((SCAN_CHUNK, HG_HEAD_DIM), F32),
                        pltpu.VMEM((SCAN_CHUNK, HG_HEAD_DIM), F32),
                        pltpu.VMEM((SCAN_CHUNK, HG_HEAD_DIM), F32),
                        pltpu.VMEM((SCAN_CHUNK, HG_HEAD_DIM), F32)],
        compiler_params=_cparams(("parallel", "arbitrary")),
        name="hgrn2_scan",
    )(z_hg, z_hg, z_hg, z_hg, z_hg, z_hg, lb, s0f, s0b)


def _dft_mats(n_rows, period):
    f = jnp.arange(n_rows, dtype=jnp.int32)
    m = (f[:, None] * f[None, :]) % period
    ang = m.astype(F32) * (2.0 * math.pi / period)
    return jnp.cos(ang).astype(BF16), jnp.sin(ang).astype(BF16)


def _split_bf16(x):
    hi = x.astype(BF16)
    lo = (x - hi.astype(F32)).astype(BF16)
    return hi, lo


def _filter_spectrum_kernel(hs_ref, hd_ref, c_ref, s_ref, kr_ref, ki_ref, kn_ref, *, scale):
    hs = hs_ref[...]
    hd = hd_ref[...]
    hs_hi, hs_lo = _split_bf16(hs)
    hd_hi, hd_lo = _split_bf16(hd)
    c = c_ref[...]
    s = s_ref[...]
    kr_ref[...] = ((_dot(c, hs_hi) + _dot(c, hs_lo)) * scale).astype(kr_ref.dtype)
    ki_ref[...] = ((_dot(s, hd_hi) + _dot(s, hd_lo)) * (-scale)).astype(ki_ref.dtype)
    row = lax.broadcasted_iota(jnp.int32, hs.shape, 0)
    sign = jnp.where(row % 2 == 0, 1.0, -1.0)
    kn = jnp.sum(hs * sign, axis=0, keepdims=True) * scale
    kn_ref[...] = jnp.broadcast_to(kn, kn_ref.shape)


def _filter_spectrum(hs, hd, cmat, smat):
    seq, ch = hs.shape
    scale = 2.0 / (2 * seq)
    return pl.pallas_call(
        functools.partial(_filter_spectrum_kernel, scale=scale),
        out_shape=[jax.ShapeDtypeStruct((seq, ch), BF16), jax.ShapeDtypeStruct((seq, ch), BF16),
                   jax.ShapeDtypeStruct((8, ch), F32)],
        compiler_params=pltpu.CompilerParams(vmem_limit_bytes=V7X_VMEM_LIMIT_BYTES),
        name="hyena_filter_spectrum",
    )(hs, hd, cmat, smat)


def _hyena_kernel(z_ref, cw_ref, cb_ref, c_ref, s_ref, kr_ref, ki_ref, kn_ref, g_ref, o_ref, *, seq, ch):
    row = lax.broadcasted_iota(jnp.int32, (seq, 1), 0)
    sign = jnp.where(row % 2 == 0, 1.0, -1.0)

    def short_conv(part):
        cols = slice(part * ch, (part + 1) * ch)
        z = z_ref[0, :, cols].astype(F32)
        z_prev = jnp.where(row >= 1, pltpu.roll(z, 1, 0), 0.0)
        z_next = jnp.where(row <= seq - 2, pltpu.roll(z, seq - 1, 0), 0.0)
        return z_prev * cw_ref[0:1, cols] + z * cw_ref[1:2, cols] + z_next * cw_ref[2:3, cols] + cb_ref[:, cols]

    def long_conv(x, order):
        cols = slice(order * ch, (order + 1) * ch)
        xb = x.astype(BF16)
        a = _dot(c_ref[...], xb)
        bm = _dot(s_ref[...], xb)
        kr = kr_ref[:, cols]
        ki = ki_ref[:, cols]
        yr = a * kr + bm * ki
        yi = (a * ki - bm * kr).astype(BF16)
        dc = 0.5 * yr[0:1, :]
        x_nyq = jnp.sum(x * sign, axis=0, keepdims=True)
        y = _dot(c_ref[...], yr.astype(BF16)) - _dot(s_ref[...], yi)
        return y - dc + (0.5 * x_nyq * kn_ref[0:1, cols]) * sign

    y = short_conv(1) * long_conv(short_conv(0), 0)
    y = short_conv(2) * long_conv(y, 1)
    ms = jnp.mean(y * y, axis=-1, keepdims=True)
    o_ref[0] = y * lax.rsqrt(ms + RMS_EPS) * g_ref[...]


def _hyena(z_hy, conv_w, conv_b, cmat, smat, kr, ki, kn, norm_g):
    bsz, seq, c3 = z_hy.shape
    ch = c3 // 3
    return pl.pallas_call(
        functools.partial(_hyena_kernel, seq=seq, ch=ch),
        out_shape=jax.ShapeDtypeStruct((bsz, seq, ch), F32),
        grid=(bsz,),
        in_specs=[pl.BlockSpec((1, seq, c3), lambda b: (b, 0, 0)),
                  _const_spec(conv_w.shape), _const_spec((1, c3)),
                  _const_spec(cmat.shape), _const_spec(smat.shape),
                  _const_spec(kr.shape), _const_spec(ki.shape), _const_spec(kn.shape),
                  _const_spec((1, ch))],
        out_specs=pl.BlockSpec((1, seq, ch), lambda b: (b, 0, 0)),
        compiler_params=_cparams(("parallel",)),
        name="hyena_mixer",
    )(z_hy, conv_w, conv_b.reshape(1, c3), cmat, smat, kr, ki, kn, norm_g.reshape(1, ch))


def _fnet_kernel(z_ref, c_ref, s_ref, bdc_ref, bds_ref, bdw_ref, b_ref, g_ref, o_ref, *, scale):
    zb = z_ref[0].astype(BF16)
    y1 = _dot(c_ref[...], zb)
    y2 = _dot(s_ref[...], zb)
    r = (_dot(y1.astype(BF16), bdc_ref[...]) - _dot(y2.astype(BF16), bds_ref[...])) * scale
    y = _dot(r.astype(BF16), bdw_ref[...]) + b_ref[...]
    ms = jnp.mean(y * y, axis=-1, keepdims=True)
    o_ref[0] = y * lax.rsqrt(ms + RMS_EPS) * g_ref[...]


def _block_diag(blocks):
    g, a, b = blocks.shape
    out = jnp.zeros((g * a, g * b), blocks.dtype)
    for i in range(g):
        out = out.at[i * a:(i + 1) * a, i * b:(i + 1) * b].set(blocks[i])
    return out


def _fnet(z_fn, cmat, smat, fn_w, fn_b, norm_g):
    bsz, seq, ch = z_fn.shape
    gd = ch // FN_GROUPS
    k = np.arange(gd)
    ang = 2.0 * np.pi * ((k[:, None] * k[None, :]) % gd) / gd
    eye = np.eye(FN_GROUPS)
    bdc = jnp.asarray(np.kron(eye, np.cos(ang)), BF16)
    bds = jnp.asarray(np.kron(eye, np.sin(ang)), BF16)
    bdw = _block_diag(fn_w).astype(BF16)
    scale = 1.0 / math.sqrt(seq * gd)
    return pl.pallas_call(
        functools.partial(_fnet_kernel, scale=scale),
        out_shape=jax.ShapeDtypeStruct((bsz, seq, ch), F32),
        grid=(bsz,),
        in_specs=[pl.BlockSpec((1, seq, ch), lambda b: (b, 0, 0)),
                  _const_spec(cmat.shape), _const_spec(smat.shape),
                  _const_spec((ch, ch)), _const_spec((ch, ch)), _const_spec((ch, ch)),
                  _const_spec((1, ch)), _const_spec((1, ch))],
        out_specs=pl.BlockSpec((1, seq, ch), lambda b: (b, 0, 0)),
        compiler_params=_cparams(("parallel",)),
        name="fnet_mixer",
    )(z_fn, cmat, smat, bdc, bds, bdw, fn_b.reshape(1, ch), norm_g.reshape(1, ch))


def _route(logits_t, rb_ref):
    rows = [logits_t[e:e + 1, :] for e in range(N_EXPERTS)]
    mx = functools.reduce(jnp.maximum, rows)
    ex = [jnp.exp(r - mx) for r in rows]
    inv = 1.0 / functools.reduce(lambda a, b: a + b, ex)
    probs = [e * inv for e in ex]
    sel = [probs[e] + rb_ref[e:e + 1, 0:1] for e in range(N_EXPERTS)]
    epg = EXPERTS_PER_GROUP
    gscore = []
    for g in range(N_GROUPS):
        s = sel[g * epg:(g + 1) * epg]
        pairs = [s[i] + s[j] for i in range(epg) for j in range(i + 1, epg)]
        gscore.append(functools.reduce(jnp.maximum, pairs))
    best = gscore[0]
    best_g = jnp.zeros_like(best, dtype=jnp.int32)
    for g in range(1, N_GROUPS):
        better = gscore[g] > best
        best_g = jnp.where(better, g, best_g)
        best = jnp.where(better, gscore[g], best)
    chosen = []
    for e in range(N_EXPERTS):
        g = e // epg
        beaten = jnp.zeros_like(best_g)
        for e2 in range(g * epg, (g + 1) * epg):
            if e2 == e:
                continue
            wins = (sel[e2] > sel[e]) | ((sel[e2] == sel[e]) & (e2 < e))
            beaten = beaten + wins.astype(jnp.int32)
        chosen.append(jnp.where((best_g == g) & (beaten < 2), probs[e], 0.0))
    tot = functools.reduce(lambda a, b: a + b, chosen)
    inv_tot = 1.0 / tot
    return jnp.concatenate([c * inv_tot for c in chosen], axis=0)


def _outproj_kernel(*refs, add_pos, n_heads):
    if add_pos:
        x_ref, pos_ref = refs[:2]
        rest = refs[2:]
        x = x_ref[0] + pos_ref[...]
    else:
        x_ref = refs[0]
        rest = refs[1:]
        x = x_ref[0]
    (yhy_ref, of_ref, ob_ref, g_ref, yfn_ref, wout_ref, hgg_ref, gate_ref, n2g_ref, sh_ref, sc_ref,
     rwt_ref, rb_ref, xo_ref, h_ref, comb_ref) = rest
    o = of_ref[0] + ob_ref[0]
    parts = []
    for h in range(n_heads):
        oh = o[:, h * HG_HEAD_DIM:(h + 1) * HG_HEAD_DIM]
        parts.append(oh * lax.rsqrt(jnp.mean(oh * oh, axis=-1, keepdims=True) + RMS_EPS))
    y_hg = jnp.concatenate(parts, axis=-1) * hgg_ref[...] * _silu(g_ref[0])
    c_hy = yhy_ref.shape[-1]
    c_hg = y_hg.shape[-1]
    mix = (_dot(yhy_ref[0].astype(BF16), wout_ref[0:c_hy, :]) +
           _dot(y_hg.astype(BF16), wout_ref[c_hy:c_hy + c_hg, :]) +
           _dot(yfn_ref[0].astype(BF16), wout_ref[c_hy + c_hg:, :]))
    xn = x + gate_ref[0] * mix
    xo_ref[0] = xn
    ms = jnp.mean(xn * xn, axis=-1, keepdims=True)
    h2 = xn * lax.rsqrt(ms + RMS_EPS) * n2g_ref[...]
    h2 = h2 * (1.0 + sc_ref[0]) + sh_ref[0]
    h_ref[0] = h2.astype(BF16)
    logits_t = _dot_nt(rwt_ref[...], h2, precision=HIGHEST)
    comb_ref[0] = _route(logits_t, rb_ref)


def _out_proj(x, pos, y_hy, o_f, o_b, z_hg, y_fn, w_out, hg_norm_g, mod, mod_row, norm2_g, rwt, rb, tm):
    bsz, seq, d = x.shape
    add_pos = pos is not None
    c_hy, c_hg, c_fn = y_hy.shape[-1], o_f.shape[-1], y_fn.shape[-1]
    tok = lambda c, j=0: pl.BlockSpec((1, tm, c), lambda b, i: (b, i, j))
    modk = lambda k: pl.BlockSpec((1, 1, d), lambda b, i: (mod_row(b), 0, k))
    in_specs = [tok(d)]
    args = [x]
    if add_pos:
        in_specs.append(pl.BlockSpec((tm, d), lambda b, i: (i, 0)))
        args.append(pos)
    in_specs += [tok(c_hy), tok(c_hg), tok(c_hg), tok(c_hg, 4), tok(c_fn),
                 _const_spec(w_out.shape), _const_spec((1, c_hg)), modk(2), _const_spec((1, d)),
                 modk(3), modk(4), _const_spec(rwt.shape), _const_spec(rb.shape)]
    args += [y_hy, o_f, o_b, z_hg, y_fn, w_out, hg_norm_g.reshape(1, c_hg), mod, norm2_g.reshape(1, d),
             mod, mod, rwt, rb]
    nt = seq // tm
    out_shape = [jax.ShapeDtypeStruct((bsz, seq, d), F32), jax.ShapeDtypeStruct((bsz, seq, d), BF16),
                 jax.ShapeDtypeStruct((bsz * nt, N_EXPERTS, tm), F32)]
    out_specs = [tok(d), tok(d), pl.BlockSpec((1, N_EXPERTS, tm), lambda b, i: (b * nt + i, 0, 0))]
    return pl.pallas_call(
        functools.partial(_outproj_kernel, add_pos=add_pos, n_heads=c_hg // HG_HEAD_DIM),
        out_shape=out_shape, grid=(bsz, nt), in_specs=in_specs, out_specs=out_specs,
        compiler_params=_cparams(("parallel", "parallel")),
        name="out_proj_router",
    )(*args)


def _moe_kernel(*refs, final):
    if final:
        h_ref, comb_ref, wg_ref, wu_ref, wd_ref, x_ref, gate_ref, fg_ref, o_ref, acc_ref = refs
    else:
        h_ref, comb_ref, wg_ref, wu_ref, wd_ref, x_ref, gate_ref, o_ref, acc_ref = refs
    e = pl.program_id(1)

    @pl.when(e == 0)
    def _():
        acc_ref[...] = jnp.zeros_like(acc_ref)

    h = h_ref[...]
    a = _silu(_dot(h, wg_ref[0])) * _dot(h, wu_ref[0])
    he = _dot(a.astype(BF16), wd_ref[0])
    comb = comb_ref[...]
    lane = lax.broadcasted_iota(jnp.int32, comb.shape, 1)
    ce = jnp.sum(jnp.where(lane == e, comb, 0.0), axis=-1, keepdims=True)
    acc_ref[...] += ce * he

    @pl.when(e == pl.num_programs(1) - 1)
    def _():
        y = x_ref[...] + gate_ref[0] * acc_ref[...]
        if final:
            ms = jnp.mean(y * y, axis=-1, keepdims=True)
            y = y * lax.rsqrt(ms + RMS_EPS) * fg_ref[...]
        o_ref[...] = y


def _moe(h, comb, wg, wu, wd, x, mod, mod_row_of_tile, final_g, tm):
    n, d = h.shape
    ne, _, de = wg.shape
    final = final_g is not None
    in_specs = [pl.BlockSpec((tm, d), lambda i, e: (i, 0)),
                pl.BlockSpec((tm, ne), lambda i, e: (i, 0)),
                pl.BlockSpec((1, d, de), lambda i, e: (e, 0, 0)),
                pl.BlockSpec((1, d, de), lambda i, e: (e, 0, 0)),
                pl.BlockSpec((1, de, d), lambda i, e: (e, 0, 0)),
                pl.BlockSpec((tm, d), lambda i, e: (i, 0)),
                pl.BlockSpec((1, 1, d), lambda i, e: (mod_row_of_tile(i), 0, 5))]
    args = [h, comb, wg, wu, wd, x, mod]
    if final:
        in_specs.append(pl.BlockSpec((1, d), lambda i, e: (0, 0)))
        args.append(final_g.reshape(1, d))
    return pl.pallas_call(
        functools.partial(_moe_kernel, final=final),
        out_shape=jax.ShapeDtypeStruct((n, d), F32),
        grid=(n // tm, ne), in_specs=in_specs,
        out_specs=pl.BlockSpec((tm, d), lambda i, e: (i, 0)),
        scratch_shapes=[pltpu.VMEM((tm, d), F32)],
        compiler_params=_cparams(("parallel", "arbitrary")),
        name="moe_experts",
    )(*args)


def _grid_sincos(n_tok, d):
    rows = n_tok // GRID_W
    row = jnp.repeat(jnp.arange(rows, dtype=F32), GRID_W)
    col = jnp.tile(jnp.arange(GRID_W, dtype=F32), rows)
    quarter = d // 4
    omega = 1.0 / (POS_BASE ** (jnp.arange(quarter, dtype=F32) / quarter))
    ar = row[:, None] * omega
    ac = col[:, None] * omega
    return jnp.concatenate([jnp.sin(ar), jnp.cos(ar), jnp.sin(ac), jnp.cos(ac)], axis=-1)


def _hyena_filters(seq, ch, w1, b1, w2, b2, w3, freq):
    t = jnp.linspace(0.0, 1.0, seq, dtype=F32)[:, None]
    w = 2.0 * math.pi * jnp.arange(seq, dtype=F32)[:, None] / seq
    bands = jnp.linspace(1e-4, HY_BANDS - 1, HY_BANDS, dtype=F32)[None, :]
    z = jnp.concatenate([t, jnp.cos(bands * w), -jnp.sin(bands * w)], axis=-1)
    hp = functools.partial(jnp.dot, precision=HIGHEST)
    h = jnp.sin(freq * (hp(z, w1) + b1))
    h = jnp.sin(freq * (hp(h, w2) + b2))
    h = hp(h, w3).reshape(seq, 2 * HY_ORDER, ch)
    max_decay = math.log(HY_DECAY_TARGET) / HY_FAST_DECAY_PCT
    min_decay = math.log(HY_DECAY_TARGET) / HY_SLOW_DECAY_PCT
    deltas = jnp.linspace(min_decay, max_decay, ch, dtype=F32)
    return h * jnp.exp(-t[:, :, None] * jnp.abs(deltas))


def _filter_halves(h, bias):
    seq = h.shape[0]
    not0 = (jnp.arange(seq) > 0)[:, None].astype(F32)
    hs, hd = [], []
    for o in range(HY_ORDER):
        hf = h[:, 2 * o].at[0].add(bias[o])
        hb = h[:, 2 * o + 1] * not0
        hs.append(hf + hb)
        hd.append(hf - hb)
    return jnp.concatenate(hs, axis=-1), jnp.concatenate(hd, axis=-1)


def kernel(x, c, ctx, c_ctx, ada_w, ada_b, norm1_g, norm2_g, w_in, w_out, hy_conv_w, hy_conv_b, hy_filt_w1, hy_filt_b1, hy_filt_w2, hy_filt_b2, hy_filt_w3, hy_filt_freq, hy_bias, hy_norm_g, hg_lower_bounds, hg_norm_g, fn_w, fn_b, fn_norm_g, router_w, router_b, moe_w_gate, moe_w_up, moe_w_down, final_norm_g):
    bsz, seq, d = x.shape
    n_ctx = ctx.shape[1]
    depth = ada_w.shape[0]
    c_hy = hy_norm_g.shape[-1]
    c_hyp = hy_conv_w.shape[-1]
    w_hg = hg_norm_g.shape[-1]
    c_fn = fn_norm_g.shape[-1]
    hg_lo, hg_hi = c_hyp, c_hyp + 5 * w_hg
    nh = w_hg // HG_HEAD_DIM

    cs = jnp.cumsum(jax.nn.softmax(hg_lower_bounds.astype(F32), axis=0), axis=0)
    lower_bounds = cs - cs[0:1]
    pos = _grid_sincos(seq, d)

    n_rows = -(-(bsz + 1) // 8) * 8
    cond = jnp.zeros((n_rows, d), F32).at[:bsz].set(c).at[bsz].set(c_ctx)
    mods = _ada_modulation(cond, ada_w, ada_b)
    x_row = lambda b: b
    c_row = lambda b: bsz

    dft = {}
    for n in (seq, n_ctx):
        dft[("hy", n)] = _dft_mats(n, 2 * n)
        dft[("fn", n)] = _dft_mats(n, n)

    rwt = jnp.transpose(router_w).astype(F32)
    rb = jnp.broadcast_to(router_b.astype(F32)[:, None], (N_EXPERTS, 128))
    s_zero = jnp.zeros((bsz, nh, HG_HEAD_DIM, HG_HEAD_DIM), F32)

    tm_x = min(512, seq)
    tm_c = min(256, n_ctx)
    tb_x = min(256, seq)
    tb_c = min(256, n_ctx)
    segs = [(hg_lo, hg_hi, F32), (0, c_hyp, BF16), (hg_hi, hg_hi + c_fn, BF16)]

    def mixer(z_hy, z_fn, n, layer):
        cmat, smat = dft[("hy", n)]
        h = _hyena_filters(n, c_hy, hy_filt_w1[layer], hy_filt_b1[layer], hy_filt_w2[layer],
                           hy_filt_b2[layer], hy_filt_w3[layer], hy_filt_freq[layer])
        hs, hd = _filter_halves(h, hy_bias[layer])
        kr, ki, kn = _filter_spectrum(hs, hd, cmat, smat)
        y_hy = _hyena(z_hy, hy_conv_w[layer], hy_conv_b[layer], cmat, smat, kr, ki, kn, hy_norm_g[layer])
        fc, fs = dft[("fn", n)]
        y_fn = _fnet(z_fn, fc, fs, fn_w[layer], fn_b[layer], fn_norm_g[layer])
        return y_hy, y_fn

    for layer in range(depth):
        last = layer == depth - 1
        mod = mods[layer].reshape(n_rows, 1, N_MOD * d)
        w_in_b = w_in[layer].astype(BF16)
        w_out_b = w_out[layer].astype(BF16)
        wg = moe_w_gate[layer].astype(BF16)
        wu = moe_w_up[layer].astype(BF16)
        wd = moe_w_down[layer].astype(BF16)
        lb = lower_bounds[layer]
        x_pos = pos if layer == 0 else None

        if last:
            (zc_hg,) = _in_proj(ctx, None, norm1_g[layer], mod, c_row, w_in_b[:, hg_lo:hg_hi],
                                [(0, 5 * w_hg, F32)], tm_c)
        else:
            zc_hg, zc_hy, zc_fn = _in_proj(ctx, None, norm1_g[layer], mod, c_row, w_in_b, segs, tm_c)
        oc_f, oc_b, s_f, s_b = _hgrn2_scan(zc_hg, lb, s_zero, s_zero, tb_c)

        zx_hg, zx_hy, zx_fn = _in_proj(x, x_pos, norm1_g[layer], mod, x_row, w_in_b, segs, tm_x)
        ox_f, ox_b, _, _ = _hgrn2_scan(zx_hg, lb, s_f, s_b, tb_x)
        yx_hy, yx_fn = mixer(zx_hy, zx_fn, seq, layer)
        x_mid, hx2, comb_x = _out_proj(x, x_pos, yx_hy, ox_f, ox_b, zx_hg, yx_fn, w_out_b, hg_norm_g[layer],
                                       mod, x_row, norm2_g[layer], rwt, rb, tm_x)
        comb_x = jnp.transpose(comb_x, (0, 2, 1)).reshape(bsz * seq, N_EXPERTS)
        tm_moe = min(1024, seq)
        tiles_per_seq = seq // tm_moe
        x = _moe(hx2.reshape(bsz * seq, d), comb_x, wg, wu, wd, x_mid.reshape(bsz * seq, d), mod,
                 lambda i: i // tiles_per_seq, final_norm_g if last else None, tm_moe).reshape(bsz, seq, d)

        if not last:
            yc_hy, yc_fn = mixer(zc_hy, zc_fn, n_ctx, layer)
            c_mid, hc2, comb_c = _out_proj(ctx, None, yc_hy, oc_f, oc_b, zc_hg, yc_fn, w_out_b,
                                           hg_norm_g[layer], mod, c_row, norm2_g[layer], rwt, rb, tm_c)
            comb_c = jnp.transpose(comb_c, (0, 2, 1)).reshape(bsz * n_ctx, N_EXPERTS)
            tm_mc = min(1024, bsz * n_ctx)
            ctx = _moe(hc2.reshape(bsz * n_ctx, d), comb_c, wg, wu, wd, c_mid.reshape(bsz * n_ctx, d), mod,
                       lambda i: bsz, None, tm_mc).reshape(bsz, n_ctx, d)
    return x
```

```python
import functools
import math

import numpy as np
import jax
import jax.numpy as jnp
from jax import lax
from jax.experimental import pallas as pl
from jax.experimental.pallas import tpu as pltpu

F32 = jnp.float32
BF16 = jnp.bfloat16
HIGHEST = lax.Precision.HIGHEST

GRID_W = 64
HY_ORDER = 2
HY_BANDS = 16
HY_DECAY_TARGET = 1e-2
HY_FAST_DECAY_PCT = 0.3
HY_SLOW_DECAY_PCT = 1.5
HG_HEAD_DIM = 128
FN_GROUPS = 4
N_EXPERTS = 16
N_GROUPS = 4
EXPERTS_PER_GROUP = N_EXPERTS // N_GROUPS
N_MOD = 6
RMS_EPS = 1e-6
POS_BASE = 10000.0

V7X_VMEM_LIMIT_BYTES = 56 * 1024 * 1024
SCAN_CHUNK = 64
SCAN_SUB = 16
SCAN_SAFE_LOG2_RANGE = 80.0
LOG2_E = 1.0 / math.log(2.0)
NEG_BIG = -1e30


def _cparams(sem):
    return pltpu.CompilerParams(dimension_semantics=sem, vmem_limit_bytes=V7X_VMEM_LIMIT_BYTES)


def _const_spec(shape):
    nd = len(shape)
    return pl.BlockSpec(shape, lambda *_: (0,) * nd, pipeline_mode=pl.Buffered(1))


def _silu(x):
    return x * jax.nn.sigmoid(x)


def _dot(a, b):
    return jnp.dot(a, b, preferred_element_type=F32)


def _dot_nt(a, b, precision=None):
    return lax.dot_general(a, b, (((1,), (1,)), ((), ())), precision=precision,
                           preferred_element_type=F32)


def _dot_tn(a, b):
    return lax.dot_general(a, b, (((0,), (0,)), ((), ())), preferred_element_type=F32)


def _ada_kernel(c_ref, w_ref, b_ref, o_ref):
    s = _silu(c_ref[...])
    o_ref[0] = jnp.dot(s, w_ref[0], precision=HIGHEST, preferred_element_type=F32) + b_ref[0]


def _ada_modulation(cond, ada_w, ada_b):
    depth, d, nd = ada_w.shape
    rows = cond.shape[0]
    tn = 1536
    return pl.pallas_call(
        _ada_kernel,
        out_shape=jax.ShapeDtypeStruct((depth, rows, nd), F32),
        grid=(depth, nd // tn),
        in_specs=[pl.BlockSpec((rows, d), lambda l, j: (0, 0)),
                  pl.BlockSpec((1, d, tn), lambda l, j: (l, 0, j)),
                  pl.BlockSpec((1, 1, tn), lambda l, j: (l, 0, j))],
        out_specs=pl.BlockSpec((1, rows, tn), lambda l, j: (l, 0, j)),
        compiler_params=_cparams(("parallel", "parallel")),
        name="ada_modulation",
    )(cond, ada_w, ada_b.reshape(depth, 1, nd))


def _inproj_kernel(*refs, add_pos, segs):
    if add_pos:
        x_ref, pos_ref, g_ref, sh_ref, sc_ref, w_ref = refs[:6]
        outs = refs[6:]
        x = x_ref[0] + pos_ref[...]
    else:
        x_ref, g_ref, sh_ref, sc_ref, w_ref = refs[:5]
        outs = refs[5:]
        x = x_ref[0]
    ms = jnp.mean(x * x, axis=-1, keepdims=True)
    h = x * lax.rsqrt(ms + RMS_EPS) * g_ref[...]
    h = (h * (1.0 + sc_ref[0]) + sh_ref[0]).astype(BF16)
    for o_ref, (a, b, _) in zip(outs, segs):
        o_ref[0] = _dot(h, w_ref[:, a:b]).astype(o_ref.dtype)


def _in_proj(x, pos, norm_g, mod, mod_row, w, segs, tm):
    bsz, seq, d = x.shape
    add_pos = pos is not None
    in_specs = [pl.BlockSpec((1, tm, d), lambda b, i: (b, i, 0))]
    args = [x]
    if add_pos:
        in_specs.append(pl.BlockSpec((tm, d), lambda b, i: (i, 0)))
        args.append(pos)
    in_specs += [pl.BlockSpec((1, d), lambda b, i: (0, 0)),
                 pl.BlockSpec((1, 1, d), lambda b, i: (mod_row(b), 0, 0)),
                 pl.BlockSpec((1, 1, d), lambda b, i: (mod_row(b), 0, 1)),
                 _const_spec(w.shape)]
    args += [norm_g.reshape(1, d), mod, mod, w]
    out_shape = [jax.ShapeDtypeStruct((bsz, seq, b - a), dt) for a, b, dt in segs]
    out_specs = [pl.BlockSpec((1, tm, b - a), lambda bb, i: (bb, i, 0)) for a, b, _ in segs]
    return pl.pallas_call(
        functools.partial(_inproj_kernel, add_pos=add_pos, segs=tuple(segs)),
        out_shape=out_shape, grid=(bsz, seq // tm), in_specs=in_specs, out_specs=out_specs,
        compiler_params=_cparams(("parallel", "parallel")),
        name="in_proj",
    )(*args)


def _cumsum_rows(x, reverse):
    c = x.shape[0]
    row = lax.broadcasted_iota(jnp.int32, x.shape, 0)
    sh = 1
    while sh < c:
        if reverse:
            x = x + jnp.where(row < c - sh, pltpu.roll(x, c - sh, 0), 0.0)
        else:
            x = x + jnp.where(row >= sh, pltpu.roll(x, sh, 0), 0.0)
        sh *= 2
    return x


def _scan_chunk_fast(q, lf2, k, v, st_ref, h, reverse):
    c = SCAN_CHUNK
    s = SCAN_SUB
    nsub = c // s
    b = _cumsum_rows(lf2, reverse)
    qs = _silu(q) * (HG_HEAD_DIM ** -0.5)
    st = st_ref[h]
    o_inter = _dot_nt((qs * jnp.exp2(b)).astype(BF16), st.astype(BF16))
    v_bf = v.astype(BF16)
    zero_row = jnp.zeros((1, HG_HEAD_DIM), F32)
    zero_blk = jnp.zeros((s, HG_HEAD_DIM), BF16)
    kt = {}
    prev_ref = None
    sc = [None] * nsub
    for i in (range(nsub - 1, -1, -1) if reverse else range(nsub)):
        r0 = i * s
        bi = b[r0:r0 + s]
        if reverse:
            ref = b[r0 + s:r0 + s + 1] if i < nsub - 1 else zero_row
        else:
            ref = b[r0 - 1:r0] if i > 0 else zero_row
        if prev_ref is not None:
            step = jnp.exp2(ref - prev_ref)
            kt = {j: blk * step for j, blk in kt.items()}
        kt[i] = k[r0:r0 + s] * jnp.exp2(ref - bi)
        prev_ref = ref
        qi = (qs[r0:r0 + s] * jnp.exp2(bi - ref)).astype(BF16)
        keys = jnp.concatenate([kt[j].astype(BF16) if j in kt else zero_blk for j in range(nsub)], axis=0)
        sc[i] = _dot_nt(qi, keys)
    scores = jnp.concatenate(sc, axis=0)
    rr = lax.broadcasted_iota(jnp.int32, (c, c), 0)
    cc = lax.broadcasted_iota(jnp.int32, (c, c), 1)
    scores = jnp.where((cc >= rr) if reverse else (cc <= rr), scores, 0.0)
    o = o_inter + _dot(scores.astype(BF16), v_bf)

    b_end = b[0:1] if reverse else b[c - 1:c]
    kd = (k * jnp.exp2(b_end - b)).astype(BF16)
    st_ref[h] = st * jnp.exp2(b_end) + _dot_tn(v_bf, kd)
    return o


def _scan_chunk_exact(q, lf2, k, v, st_ref, h, b_scr, q_scr, reverse):
    c = SCAN_CHUNK
    s = SCAN_SUB
    nsub = c // s
    b = _cumsum_rows(lf2, reverse)
    qs = _silu(q) * (HG_HEAD_DIM ** -0.5)
    st = st_ref[h]
    o_inter = _dot_nt((qs * jnp.exp2(b)).astype(BF16), st.astype(BF16))

    b_scr[...] = b
    q_scr[...] = qs
    v_bf = v.astype(BF16)
    ones = jnp.ones((HG_HEAD_DIM, HG_HEAD_DIM), BF16)
    sub_iota = lax.broadcasted_iota(jnp.int32, (s, HG_HEAD_DIM), 0)
    row_iota = lax.broadcasted_iota(jnp.int32, (c, HG_HEAD_DIM), 0)
    grp = (lax.broadcasted_iota(jnp.int32, (s, s * s), 1) // s ==
           lax.broadcasted_iota(jnp.int32, (s, s * s), 0)).astype(BF16)
    o_parts = []
    for i in range(nsub):
        r0 = i * s
        bi = b[r0:r0 + s]
        ki = k[r0:r0 + s]
        vi = v[r0:r0 + s]
        prods = []
        for t in range(s):
            bt = b_scr[pl.ds(r0 + t, 1), :]
            qt = q_scr[pl.ds(r0 + t, 1), :]
            keep = (sub_iota >= t) if reverse else (sub_iota <= t)
            e = jnp.exp2(jnp.where(keep, bt - bi, NEG_BIG))
            prods.append(((qt * ki) * e).astype(BF16))
        p = jnp.concatenate(prods, axis=0)
        rsum = _dot(p, ones)
        zt = (rsum * jnp.concatenate([vi] * s, axis=0)).astype(BF16)
        o_i = _dot(grp, zt)
        if reverse and i < nsub - 1:
            ref_row = b[r0 + s:r0 + s + 1]
            key_rows = row_iota >= r0 + s
        elif (not reverse) and i > 0:
            ref_row = b[r0 - 1:r0]
            key_rows = row_iota < r0
        else:
            ref_row = None
        if ref_row is not None:
            qi = qs[r0:r0 + s] * jnp.exp2(bi - ref_row)
            ks = k * jnp.exp2(jnp.where(key_rows, ref_row - b, NEG_BIG))
            sc = _dot_nt(qi.astype(BF16), ks.astype(BF16))
            o_i = o_i + _dot(sc.astype(BF16), v_bf)
        o_parts.append(o_i)
    o = o_inter + jnp.concatenate(o_parts, axis=0)

    b_end = b[0:1] if reverse else b[c - 1:c]
    kd = (k * jnp.exp2(b_end - b)).astype(BF16)
    st_ref[h] = st * jnp.exp2(b_end) + _dot_tn(v_bf, kd)
    return o


def _scan_kernel(qf_ref, qb_ref, ff_ref, fb_ref, vf_ref, vb_ref, lb_ref, s0f_ref, s0b_ref,
                 of_ref, ob_ref, sf_ref, sb_ref, stf, stb, lf_scr, k_scr, bf_scr, qf_scr, bb_scr, qb_scr,
                 *, n_heads, n_chunks):
    n = pl.program_id(1)

    @pl.when(n == 0)
    def _():
        stf[...] = s0f_ref[0]
        stb[...] = s0b_ref[0]

    tb = n_chunks * SCAN_CHUNK
    nblk = tb // SCAN_SUB
    sel = (lax.broadcasted_iota(jnp.int32, (nblk, tb), 1) // SCAN_SUB ==
           lax.broadcasted_iota(jnp.int32, (nblk, tb), 0)).astype(BF16)
    worst = jnp.zeros((nblk, lb_ref.shape[1]), F32)
    for d, f_ref in enumerate((ff_ref, fb_ref)):
        lb = lb_ref[d:d + 1, :]
        forget = lb + (1.0 - lb) * jax.nn.sigmoid(f_ref[0])
        lf2 = jnp.log(forget) * LOG2_E
        lf_scr[d] = lf2
        k_scr[d] = 1.0 - forget
        worst = jnp.maximum(worst, -_dot(sel, lf2.astype(BF16)))
    safe = jnp.max(worst) <= SCAN_SAFE_LOG2_RANGE

    def rows_of(ci):
        rf = pl.ds(pl.multiple_of(ci * SCAN_CHUNK, SCAN_CHUNK), SCAN_CHUNK)
        rb = pl.ds(pl.multiple_of((n_chunks - 1 - ci) * SCAN_CHUNK, SCAN_CHUNK), SCAN_CHUNK)
        return rf, rb

    @pl.when(safe)
    def _():
        def chunk_body(ci, carry):
            rf, rb = rows_of(ci)
            for h in range(n_heads):
                cols = slice(h * HG_HEAD_DIM, (h + 1) * HG_HEAD_DIM)
                of_ref[0, rf, cols] = _scan_chunk_fast(qf_ref[0, rf, cols], lf_scr[0, rf, cols],
                                                       k_scr[0, rf, cols], vf_ref[0, rf, cols], stf, h, False)
                ob_ref[0, rb, cols] = _scan_chunk_fast(qb_ref[0, rb, cols], lf_scr[1, rb, cols],
                                                       k_scr[1, rb, cols], vb_ref[0, rb, cols], stb, h, True)
            return carry

        lax.fori_loop(0, n_chunks, chunk_body, 0)

    @pl.when(jnp.logical_not(safe))
    def _():
        def head_body(h, carry):
            cols = pl.ds(pl.multiple_of(h * HG_HEAD_DIM, HG_HEAD_DIM), HG_HEAD_DIM)

            def chunk_body(ci, carry2):
                rf, rb = rows_of(ci)
                of_ref[0, rf, cols] = _scan_chunk_exact(qf_ref[0, rf, cols], lf_scr[0, rf, cols],
                                                        k_scr[0, rf, cols], vf_ref[0, rf, cols], stf, h,
                                                        bf_scr, qf_scr, False)
                ob_ref[0, rb, cols] = _scan_chunk_exact(qb_ref[0, rb, cols], lf_scr[1, rb, cols],
                                                        k_scr[1, rb, cols], vb_ref[0, rb, cols], stb, h,
                                                        bb_scr, qb_scr, True)
                return carry2

            return lax.fori_loop(0, n_chunks, chunk_body, carry)

        lax.fori_loop(0, n_heads, head_body, 0)

    @pl.when(n == pl.num_programs(1) - 1)
    def _():
        sf_ref[0] = stf[...]
        sb_ref[0] = stb[...]


def _hgrn2_scan(z_hg, lb, s0f, s0b, tb):
    bsz, seq, w5 = z_hg.shape
    w = w5 // 5
    nh = w // HG_HEAD_DIM
    nb = seq // tb
    blk = (1, tb, w)
    fwd = lambda j: pl.BlockSpec(blk, lambda b, n: (b, n, j))
    bwd = lambda j: pl.BlockSpec(blk, lambda b, n: (b, nb - 1 - n, j))
    st_spec = pl.BlockSpec((1, nh, HG_HEAD_DIM, HG_HEAD_DIM), lambda b, n: (b, 0, 0, 0))
    st_shape = jax.ShapeDtypeStruct((bsz, nh, HG_HEAD_DIM, HG_HEAD_DIM), F32)
    o_shape = jax.ShapeDtypeStruct((bsz, seq, w), F32)
    return pl.pallas_call(
        functools.partial(_scan_kernel, n_heads=nh, n_chunks=tb // SCAN_CHUNK),
        out_shape=[o_shape, o_shape, st_shape, st_shape],
        grid=(bsz, nb),
        in_specs=[fwd(0), bwd(0), fwd(1), bwd(2), fwd(3), bwd(3),
                  pl.BlockSpec((2, w), lambda b, n: (0, 0)), st_spec, st_spec],
        out_specs=[fwd(0), bwd(0), st_spec, st_spec],
        scratch_shapes=[pltpu.VMEM((nh, HG_HEAD_DIM, HG_HEAD_DIM), F32),
                        pltpu.VMEM((nh, HG_HEAD_DIM, HG_HEAD_DIM), F32),
                        pltpu.VMEM((2, tb, w), F32),
                        pltpu.VMEM((2, tb, w), F32),
                        pltpu.VMEM((SCAN_CHUNK, HG_HEAD_DIM), F32),
                        pltpu.VMEM((SCAN_CHUNK, HG_HEAD_DIM), F32),
                        pltpu.VMEM((SCAN_CHUNK, HG_HEAD_DIM), F32),
                        pltpu.VMEM((SCAN_CHUNK, HG_HEAD_DIM), F32)],
        compiler_params=_cparams(("parallel", "arbitrary")),
        name="hgrn2_scan",
    )(z_hg, z_hg, z_hg, z_hg, z_hg, z_hg, lb, s0f, s0b)


def _dft_mats(n_rows, period):
    f = jnp.arange(n_rows, dtype=jnp.int32)
    m = (f[:, None] * f[None, :]) % period
    ang = m.astype(F32) * (2.0 * math.pi / period)
    return jnp.cos(ang).astype(BF16), jnp.sin(ang).astype(BF16)


def _split_bf16(x):
    hi = x.astype(BF16)
    lo = (x - hi.astype(F32)).astype(BF16)
    return hi, lo


def _filter_spectrum_kernel(hs_ref, hd_ref, c_ref, s_ref, kr_ref, ki_ref, kn_ref, *, scale):
    hs = hs_ref[...]
    hd = hd_ref[...]
    hs_hi, hs_lo = _split_bf16(hs)
    hd_hi, hd_lo = _split_bf16(hd)
    c = c_ref[...]
    s = s_ref[...]
    kr_ref[...] = ((_dot(c, hs_hi) + _dot(c, hs_lo)) * scale).astype(kr_ref.dtype)
    ki_ref[...] = ((_dot(s, hd_hi) + _dot(s, hd_lo)) * (-scale)).astype(ki_ref.dtype)
    row = lax.broadcasted_iota(jnp.int32, hs.shape, 0)
    sign = jnp.where(row % 2 == 0, 1.0, -1.0)
    kn = jnp.sum(hs * sign, axis=0, keepdims=True) * scale
    kn_ref[...] = jnp.broadcast_to(kn, kn_ref.shape)


def _filter_spectrum(hs, hd, cmat, smat):
    seq, ch = hs.shape
    scale = 2.0 / (2 * seq)
    return pl.pallas_call(
        functools.partial(_filter_spectrum_kernel, scale=scale),
        out_shape=[jax.ShapeDtypeStruct((seq, ch), BF16), jax.ShapeDtypeStruct((seq, ch), BF16),
                   jax.ShapeDtypeStruct((8, ch), F32)],
        compiler_params=pltpu.CompilerParams(vmem_limit_bytes=V7X_VMEM_LIMIT_BYTES),
        name="hyena_filter_spectrum",
    )(hs, hd, cmat, smat)


def _hyena_kernel(z_ref, cw_ref, cb_ref, c_ref, s_ref, kr_ref, ki_ref, kn_ref, g_ref, o_ref, *, seq, ch):
    row = lax.broadcasted_iota(jnp.int32, (seq, 1), 0)
    sign = jnp.where(row % 2 == 0, 1.0, -1.0)

    def short_conv(part):
        cols = slice(part * ch, (part + 1) * ch)
        z = z_ref[0, :, cols].astype(F32)
        z_prev = jnp.where(row >= 1, pltpu.roll(z, 1, 0), 0.0)
        z_next = jnp.where(row <= seq - 2, pltpu.roll(z, seq - 1, 0), 0.0)
        return z_prev * cw_ref[0:1, cols] + z * cw_ref[1:2, cols] + z_next * cw_ref[2:3, cols] + cb_ref[:, cols]

    def long_conv(x, order):
        cols = slice(order * ch, (order + 1) * ch)
        xb = x.astype(BF16)
        a = _dot(c_ref[...], xb)
        bm = _dot(s_ref[...], xb)
        kr = kr_ref[:, cols]
        ki = ki_ref[:, cols]
        yr = a * kr + bm * ki
        yi = (a * ki - bm * kr).astype(BF16)
        dc = 0.5 * yr[0:1, :]
        x_nyq = jnp.sum(x * sign, axis=0, keepdims=True)
        y = _dot(c_ref[...], yr.astype(BF16)) - _dot(s_ref[...], yi)
        return y - dc + (0.5 * x_nyq * kn_ref[0:1, cols]) * sign

    y = short_conv(1) * long_conv(short_conv(0), 0)
    y = short_conv(2) * long_conv(y, 1)
    ms = jnp.mean(y * y, axis=-1, keepdims=True)
    o_ref[0] = y * lax.rsqrt(ms + RMS_EPS) * g_ref[...]


def _hyena(z_hy, conv_w, conv_b, cmat, smat, kr, ki, kn, norm_g):
    bsz, seq, c3 = z_hy.shape
    ch = c3 // 3
    return pl.pallas_call(
        functools.partial(_hyena_kernel, seq=seq, ch=ch),
        out_shape=jax.ShapeDtypeStruct((bsz, seq, ch), F32),
        grid=(bsz,),
        in_specs=[pl.BlockSpec((1, seq, c3), lambda b: (b, 0, 0)),
                  _const_spec(conv_w.shape), _const_spec((1, c3)),
                  _const_spec(cmat.shape), _const_spec(smat.shape),
                  _const_spec(kr.shape), _const_spec(ki.shape), _const_spec(kn.shape),
                  _const_spec((1, ch))],
        out_specs=pl.BlockSpec((1, seq, ch), lambda b: (b, 0, 0)),
        compiler_params=_cparams(("parallel",)),
        name="hyena_mixer",
    )(z_hy, conv_w, conv_b.reshape(1, c3), cmat, smat, kr, ki, kn, norm_g.reshape(1, ch))


def _fnet_kernel(z_ref, c_ref, s_ref, bdc_ref, bds_ref, bdw_ref, b_ref, g_ref, o_ref, *, scale):
    zb = z_ref[0].astype(BF16)
    y1 = _dot(c_ref[...], zb)
    y2 = _dot(s_ref[...], zb)
    r = (_dot(y1.astype(BF16), bdc_ref[...]) - _dot(y2.astype(BF16), bds_ref[...])) * scale
    y = _dot(r.astype(BF16), bdw_ref[...]) + b_ref[...]
    ms = jnp.mean(y * y, axis=-1, keepdims=True)
    o_ref[0] = y * lax.rsqrt(ms + RMS_EPS) * g_ref[...]


def _block_diag(blocks):
    g, a, b = blocks.shape
    out = jnp.zeros((g * a, g * b), blocks.dtype)
    for i in range(g):
        out = out.at[i * a:(i + 1) * a, i * b:(i + 1) * b].set(blocks[i])
    return out


def _fnet(z_fn, cmat, smat, fn_w, fn_b, norm_g):
    bsz, seq, ch = z_fn.shape
    gd = ch // FN_GROUPS
    k = np.arange(gd)
    ang = 2.0 * np.pi * ((k[:, None] * k[None, :]) % gd) / gd
    eye = np.eye(FN_GROUPS)
    bdc = jnp.asarray(np.kron(eye, np.cos(ang)), BF16)
    bds = jnp.asarray(np.kron(eye, np.sin(ang)), BF16)
    bdw = _block_diag(fn_w).astype(BF16)
    scale = 1.0 / math.sqrt(seq * gd)
    return pl.pallas_call(
        functools.partial(_fnet_kernel, scale=scale),
        out_shape=jax.ShapeDtypeStruct((bsz, seq, ch), F32),
        grid=(bsz,),
        in_specs=[pl.BlockSpec((1, seq, ch), lambda b: (b, 0, 0)),
                  _const_spec(cmat.shape), _const_spec(smat.shape),
                  _const_spec((ch, ch)), _const_spec((ch, ch)), _const_spec((ch, ch)),
                  _const_spec((1, ch)), _const_spec((1, ch))],
        out_specs=pl.BlockSpec((1, seq, ch), lambda b: (b, 0, 0)),
        compiler_params=_cparams(("parallel",)),
        name="fnet_mixer",
    )(z_fn, cmat, smat, bdc, bds, bdw, fn_b.reshape(1, ch), norm_g.reshape(1, ch))


def _route(logits_t, rb_ref):
    rows = [logits_t[e:e + 1, :] for e in range(N_EXPERTS)]
    mx = functools.reduce(jnp.maximum, rows)
    ex = [jnp.exp(r - mx) for r in rows]
    inv = 1.0 / functools.reduce(lambda a, b: a + b, ex)
    probs = [e * inv for e in ex]
    sel = [probs[e] + rb_ref[e:e + 1, 0:1] for e in range(N_EXPERTS)]
    epg = EXPERTS_PER_GROUP
    gscore = []
    for g in range(N_GROUPS):
        s = sel[g * epg:(g + 1) * epg]
        pairs = [s[i] + s[j] for i in range(epg) for j in range(i + 1, epg)]
        gscore.append(functools.reduce(jnp.maximum, pairs))
    best = gscore[0]
    best_g = jnp.zeros_like(best, dtype=jnp.int32)
    for g in range(1, N_GROUPS):
        better = gscore[g] > best
        best_g = jnp.where(better, g, best_g)
        best = jnp.where(better, gscore[g], best)
    chosen = []
    for e in range(N_EXPERTS):
        g = e // epg
        beaten = jnp.zeros_like(best_g)
        for e2 in range(g * epg, (g + 1) * epg):
            if e2 == e:
                continue
            wins = (sel[e2] > sel[e]) | ((sel[e2] == sel[e]) & (e2 < e))
            beaten = beaten + wins.astype(jnp.int32)
        chosen.append(jnp.where((best_g == g) & (beaten < 2), probs[e], 0.0))
    tot = functools.reduce(lambda a, b: a + b, chosen)
    inv_tot = 1.0 / tot
    return jnp.concatenate([c * inv_tot for c in chosen], axis=0)


def _outproj_kernel(*refs, add_pos, n_heads):
    if add_pos:
        x_ref, pos_ref = refs[:2]
        rest = refs[2:]
        x = x_ref[0] + pos_ref[...]
    else:
        x_ref = refs[0]
        rest = refs[1:]
        x = x_ref[0]
    (yhy_ref, of_ref, ob_ref, g_ref, yfn_ref, wout_ref, hgg_ref, gate_ref, n2g_ref, sh_ref, sc_ref,
     rwt_ref, rb_ref, xo_ref, h_ref, comb_ref) = rest
    o = of_ref[0] + ob_ref[0]
    parts = []
    for h in range(n_heads):
        oh = o[:, h * HG_HEAD_DIM:(h + 1) * HG_HEAD_DIM]
        parts.append(oh * lax.rsqrt(jnp.mean(oh * oh, axis=-1, keepdims=True) + RMS_EPS))
    y_hg = jnp.concatenate(parts, axis=-1) * hgg_ref[...] * _silu(g_ref[0])
    c_hy = yhy_ref.shape[-1]
    c_hg = y_hg.shape[-1]
    mix = (_dot(yhy_ref[0].astype(BF16), wout_ref[0:c_hy, :]) +
           _dot(y_hg.astype(BF16), wout_ref[c_hy:c_hy + c_hg, :]) +
           _dot(yfn_ref[0].astype(BF16), wout_ref[c_hy + c_hg:, :]))
    xn = x + gate_ref[0] * mix
    xo_ref[0] = xn
    ms = jnp.mean(xn * xn, axis=-1, keepdims=True)
    h2 = xn * lax.rsqrt(ms + RMS_EPS) * n2g_ref[...]
    h2 = h2 * (1.0 + sc_ref[0]) + sh_ref[0]
    h_ref[0] = h2.astype(BF16)
    logits_t = _dot_nt(rwt_ref[...], h2, precision=HIGHEST)
    comb_ref[0] = _route(logits_t, rb_ref)


def _out_proj(x, pos, y_hy, o_f, o_b, z_hg, y_fn, w_out, hg_norm_g, mod, mod_row, norm2_g, rwt, rb, tm):
    bsz, seq, d = x.shape
    add_pos = pos is not None
    c_hy, c_hg, c_fn = y_hy.shape[-1], o_f.shape[-1], y_fn.shape[-1]
    tok = lambda c, j=0: pl.BlockSpec((1, tm, c), lambda b, i: (b, i, j))
    modk = lambda k: pl.BlockSpec((1, 1, d), lambda b, i: (mod_row(b), 0, k))
    in_specs = [tok(d)]
    args = [x]
    if add_pos:
        in_specs.append(pl.BlockSpec((tm, d), lambda b, i: (i, 0)))
        args.append(pos)
    in_specs += [tok(c_hy), tok(c_hg), tok(c_hg), tok(c_hg, 4), tok(c_fn),
                 _const_spec(w_out.shape), _const_spec((1, c_hg)), modk(2), _const_spec((1, d)),
                 modk(3), modk(4), _const_spec(rwt.shape), _const_spec(rb.shape)]
    args += [y_hy, o_f, o_b, z_hg, y_fn, w_out, hg_norm_g.reshape(1, c_hg), mod, norm2_g.reshape(1, d),
             mod, mod, rwt, rb]
    nt = seq // tm
    out_shape = [jax.ShapeDtypeStruct((bsz, seq, d), F32), jax.ShapeDtypeStruct((bsz, seq, d), BF16),
                 jax.ShapeDtypeStruct((bsz * nt, N_EXPERTS, tm), F32)]
    out_specs = [tok(d), tok(d), pl.BlockSpec((1, N_EXPERTS, tm), lambda b, i: (b * nt + i, 0, 0))]
    return pl.pallas_call(
        functools.partial(_outproj_kernel, add_pos=add_pos, n_heads=c_hg // HG_HEAD_DIM),
        out_shape=out_shape, grid=(bsz, nt), in_specs=in_specs, out_specs=out_specs,
        compiler_params=_cparams(("parallel", "parallel")),
        name="out_proj_router",
    )(*args)


def _moe_kernel(*refs, final):
    if final:
        h_ref, comb_ref, wg_ref, wu_ref, wd_ref, x_ref, gate_ref, fg_ref, o_ref, acc_ref = refs
    else:
        h_ref, comb_ref, wg_ref, wu_ref, wd_ref, x_ref, gate_ref, o_ref, acc_ref = refs
    e = pl.program_id(1)

    @pl.when(e == 0)
    def _():
        acc_ref[...] = jnp.zeros_like(acc_ref)

    h = h_ref[...]
    a = _silu(_dot(h, wg_ref[0])) * _dot(h, wu_ref[0])
    he = _dot(a.astype(BF16), wd_ref[0])
    comb = comb_ref[...]
    lane = lax.broadcasted_iota(jnp.int32, comb.shape, 1)
    ce = jnp.sum(jnp.where(lane == e, comb, 0.0), axis=-1, keepdims=True)
    acc_ref[...] += ce * he

    @pl.when(e == pl.num_programs(1) - 1)
    def _():
        y = x_ref[...] + gate_ref[0] * acc_ref[...]
        if final:
            ms = jnp.mean(y * y, axis=-1, keepdims=True)
            y = y * lax.rsqrt(ms + RMS_EPS) * fg_ref[...]
        o_ref[...] = y


def _moe(h, comb, wg, wu, wd, x, mod, mod_row_of_tile, final_g, tm):
    n, d = h.shape
    ne, _, de = wg.shape
    final = final_g is not None
    in_specs = [pl.BlockSpec((tm, d), lambda i, e: (i, 0)),
                pl.BlockSpec((tm, ne), lambda i, e: (i, 0)),
                pl.BlockSpec((1, d, de), lambda i, e: (e, 0, 0)),
                pl.BlockSpec((1, d, de), lambda i, e: (e, 0, 0)),
                pl.BlockSpec((1, de, d), lambda i, e: (e, 0, 0)),
                pl.BlockSpec((tm, d), lambda i, e: (i, 0)),
                pl.BlockSpec((1, 1, d), lambda i, e: (mod_row_of_tile(i), 0, 5))]
    args = [h, comb, wg, wu, wd, x, mod]
    if final:
        in_specs.append(pl.BlockSpec((1, d), lambda i, e: (0, 0)))
        args.append(final_g.reshape(1, d))
    return pl.pallas_call(
        functools.partial(_moe_kernel, final=final),
        out_shape=jax.ShapeDtypeStruct((n, d), F32),
        grid=(n // tm, ne), in_specs=in_specs,
        out_specs=pl.BlockSpec((tm, d), lambda i, e: (i, 0)),
        scratch_shapes=[pltpu.VMEM((tm, d), F32)],
        compiler_params=_cparams(("parallel", "arbitrary")),
        name="moe_experts",
    )(*args)


def _grid_sincos(n_tok, d):
    rows = n_tok // GRID_W
    row = jnp.repeat(jnp.arange(rows, dtype=F32), GRID_W)
    col = jnp.tile(jnp.arange(GRID_W, dtype=F32), rows)
    quarter = d // 4
    omega = 1.0 / (POS_BASE ** (jnp.arange(quarter, dtype=F32) / quarter))
    ar = row[:, None] * omega
    ac = col[:, None] * omega
    return jnp.concatenate([jnp.sin(ar), jnp.cos(ar), jnp.sin(ac), jnp.cos(ac)], axis=-1)


def _hyena_filters(seq, ch, w1, b1, w2, b2, w3, freq):
    t = jnp.linspace(0.0, 1.0, seq, dtype=F32)[:, None]
    w = 2.0 * math.pi * jnp.arange(seq, dtype=F32)[:, None] / seq
    bands = jnp.linspace(1e-4, HY_BANDS - 1, HY_BANDS, dtype=F32)[None, :]
    z = jnp.concatenate([t, jnp.cos(bands * w), -jnp.sin(bands * w)], axis=-1)
    hp = functools.partial(jnp.dot, precision=HIGHEST)
    h = jnp.sin(freq * (hp(z, w1) + b1))
    h = jnp.sin(freq * (hp(h, w2) + b2))
    h = hp(h, w3).reshape(seq, 2 * HY_ORDER, ch)
    max_decay = math.log(HY_DECAY_TARGET) / HY_FAST_DECAY_PCT
    min_decay = math.log(HY_DECAY_TARGET) / HY_SLOW_DECAY_PCT
    deltas = jnp.linspace(min_decay, max_decay, ch, dtype=F32)
    return h * jnp.exp(-t[:, :, None] * jnp.abs(deltas))


def _filter_halves(h, bias):
    seq = h.shape[0]
    not0 = (jnp.arange(seq) > 0)[:, None].astype(F32)
    hs, hd = [], []
    for o in range(HY_ORDER):
        hf = h[:, 2 * o].at[0].add(bias[o])
        hb = h[:, 2 * o + 1] * not0
        hs.append(hf + hb)
        hd.append(hf - hb)
    return jnp.concatenate(hs, axis=-1), jnp.concatenate(hd, axis=-1)


def kernel(x, c, ctx, c_ctx, ada_w, ada_b, norm1_g, norm2_g, w_in, w_out, hy_conv_w, hy_conv_b, hy_filt_w1, hy_filt_b1, hy_filt_w2, hy_filt_b2, hy_filt_w3, hy_filt_freq, hy_bias, hy_norm_g, hg_lower_bounds, hg_norm_g, fn_w, fn_b, fn_norm_g, router_w, router_b, moe_w_gate, moe_w_up, moe_w_down, final_norm_g):
    bsz, seq, d = x.shape
    n_ctx = ctx.shape[1]
    depth = ada_w.shape[0]
    c_hy = hy_norm_g.shape[-1]
    c_hyp = hy_conv_w.shape[-1]
    w_hg = hg_norm_g.shape[-1]
    c_fn = fn_norm_g.shape[-1]
    hg_lo, hg_hi = c_hyp, c_hyp + 5 * w_hg
    nh = w_hg // HG_HEAD_DIM

    cs = jnp.cumsum(jax.nn.softmax(hg_lower_bounds.astype(F32), axis=0), axis=0)
    lower_bounds = cs - cs[0:1]
    pos = _grid_sincos(seq, d)

    n_rows = -(-(bsz + 1) // 8) * 8
    cond = jnp.zeros((n_rows, d), F32).at[:bsz].set(c).at[bsz].set(c_ctx)
    mods = _ada_modulation(cond, ada_w, ada_b)
    x_row = lambda b: b
    c_row = lambda b: bsz

    dft = {}
    for n in (seq, n_ctx):
        dft[("hy", n)] = _dft_mats(n, 2 * n)
        dft[("fn", n)] = _dft_mats(n, n)

    rwt = jnp.transpose(router_w).astype(F32)
    rb = jnp.broadcast_to(router_b.astype(F32)[:, None], (N_EXPERTS, 128))
    s_zero = jnp.zeros((bsz, nh, HG_HEAD_DIM, HG_HEAD_DIM), F32)

    tm_x = min(512, seq)
    tm_c = min(256, n_ctx)
    tb_x = min(256, seq)
    tb_c = min(256, n_ctx)
    segs = [(hg_lo, hg_hi, F32), (0, c_hyp, BF16), (hg_hi, hg_hi + c_fn, BF16)]

    def mixer(z_hy, z_fn, n, layer):
        cmat, smat = dft[("hy", n)]
        h = _hyena_filters(n, c_hy, hy_filt_w1[layer], hy_filt_b1[layer], hy_filt_w2[layer],
                           hy_filt_b2[layer], hy_filt_w3[layer], hy_filt_freq[layer])
        hs, hd = _filter_halves(h, hy_bias[layer])
        kr, ki, kn = _filter_spectrum(hs, hd, cmat, smat)
        y_hy = _hyena(z_hy, hy_conv_w[layer], hy_conv_b[layer], cmat, smat, kr, ki, kn, hy_norm_g[layer])
        fc, fs = dft[("fn", n)]
        y_fn = _fnet(z_fn, fc, fs, fn_w[layer], fn_b[layer], fn_norm_g[layer])
        return y_hy, y_fn

    for layer in range(depth):
        last = layer == depth - 1
        mod = mods[layer].reshape(n_rows, 1, N_MOD * d)
        w_in_b = w_in[layer].astype(BF16)
        w_out_b = w_out[layer].astype(BF16)
        wg = moe_w_gate[layer].astype(BF16)
        wu = moe_w_up[layer].astype(BF16)
        wd = moe_w_down[layer].astype(BF16)
        lb = lower_bounds[layer]
        x_pos = pos if layer == 0 else None

        if last:
            (zc_hg,) = _in_proj(ctx, None, norm1_g[layer], mod, c_row, w_in_b[:, hg_lo:hg_hi],
                                [(0, 5 * w_hg, F32)], tm_c)
        else:
            zc_hg, zc_hy, zc_fn = _in_proj(ctx, None, norm1_g[layer], mod, c_row, w_in_b, segs, tm_c)
        oc_f, oc_b, s_f, s_b = _hgrn2_scan(zc_hg, lb, s_zero, s_zero, tb_c)

        zx_hg, zx_hy, zx_fn = _in_proj(x, x_pos, norm1_g[layer], mod, x_row, w_in_b, segs, tm_x)
        ox_f, ox_b, _, _ = _hgrn2_scan(zx_hg, lb, s_f, s_b, tb_x)
        yx_hy, yx_fn = mixer(zx_hy, zx_fn, seq, layer)
        x_mid, hx2, comb_x = _out_proj(x, x_pos, yx_hy, ox_f, ox_b, zx_hg, yx_fn, w_out_b, hg_norm_g[layer],
                                       mod, x_row, norm2_g[layer], rwt, rb, tm_x)
        comb_x = jnp.transpose(comb_x, (0, 2, 1)).reshape(bsz * seq, N_EXPERTS)
        tm_moe = min(1024, seq)
        tiles_per_seq = seq // tm_moe
        x = _moe(hx2.reshape(bsz * seq, d), comb_x, wg, wu, wd, x_mid.reshape(bsz * seq, d), mod,
                 lambda i: i // tiles_per_seq, final_norm_g if last else None, tm_moe).reshape(bsz, seq, d)

        if not last:
            yc_hy, yc_fn = mixer(zc_hy, zc_fn, n_ctx, layer)
            c_mid, hc2, comb_c = _out_proj(ctx, None, yc_hy, oc_f, oc_b, zc_hg, yc_fn, w_out_b,
                                           hg_norm_g[layer], mod, c_row, norm2_g[layer], rwt, rb, tm_c)
            comb_c = jnp.transpose(comb_c, (0, 2, 1)).reshape(bsz * n_ctx, N_EXPERTS)
            tm_mc = min(1024, bsz * n_ctx)
            ctx = _moe(hc2.reshape(bsz * n_ctx, d), comb_c, wg, wu, wd, c_mid.reshape(bsz * n_ctx, d), mod,
                       lambda i: bsz, None, tm_mc).reshape(bsz, n_ctx, d)
    return x
```

```python
import functools
import math

import numpy as np
import jax
import jax.numpy as jnp
from jax import lax
from jax.experimental import pallas as pl
from jax.experimental.pallas import tpu as pltpu

F32 = jnp.float32
BF16 = jnp.bfloat16
HIGHEST = lax.Precision.HIGHEST

GRID_W = 64
HY_ORDER = 2
HY_BANDS = 16
HY_DECAY_TARGET = 1e-2
HY_FAST_DECAY_PCT = 0.3
HY_SLOW_DECAY_PCT = 1.5
HG_HEAD_DIM = 128
FN_GROUPS = 4
N_EXPERTS = 16
N_GROUPS = 4
EXPERTS_PER_GROUP = N_EXPERTS // N_GROUPS
N_MOD = 6
RMS_EPS = 1e-6
POS_BASE = 10000.0

V7X_VMEM_LIMIT_BYTES = 56 * 1024 * 1024
SCAN_CHUNK = 64
SCAN_SUB = 16
SCAN_SAFE_LOG2_RANGE = 80.0
LOG2_E = 1.0 / math.log(2.0)
MOE_CAP = 256
NEG_BIG = -1e30


def _cparams(sem):
    return pltpu.CompilerParams(dimension_semantics=sem, vmem_limit_bytes=V7X_VMEM_LIMIT_BYTES)


def _const_spec(shape):
    nd = len(shape)
    return pl.BlockSpec(shape, lambda *_: (0,) * nd, pipeline_mode=pl.Buffered(1))


def _silu(x):
    return x * jax.nn.sigmoid(x)


def _dot(a, b):
    return jnp.dot(a, b, preferred_element_type=F32)


def _dot_nt(a, b, precision=None):
    return lax.dot_general(a, b, (((1,), (1,)), ((), ())), precision=precision,
                           preferred_element_type=F32)


def _dot_tn(a, b):
    return lax.dot_general(a, b, (((0,), (0,)), ((), ())), preferred_element_type=F32)


def _ada_kernel(c_ref, w_ref, b_ref, o_ref):
    s = _silu(c_ref[...])
    o_ref[0] = jnp.dot(s, w_ref[0], precision=HIGHEST, preferred_element_type=F32) + b_ref[0]


def _ada_modulation(cond, ada_w, ada_b):
    depth, d, nd = ada_w.shape
    rows = cond.shape[0]
    tn = 1536
    return pl.pallas_call(
        _ada_kernel,
        out_shape=jax.ShapeDtypeStruct((depth, rows, nd), F32),
        grid=(depth, nd // tn),
        in_specs=[pl.BlockSpec((rows, d), lambda l, j: (0, 0)),
                  pl.BlockSpec((1, d, tn), lambda l, j: (l, 0, j)),
                  pl.BlockSpec((1, 1, tn), lambda l, j: (l, 0, j))],
        out_specs=pl.BlockSpec((1, rows, tn), lambda l, j: (l, 0, j)),
        compiler_params=_cparams(("parallel", "parallel")),
        name="ada_modulation",
    )(cond, ada_w, ada_b.reshape(depth, 1, nd))


def _inproj_kernel(*refs, add_pos, segs):
    if add_pos:
        x_ref, pos_ref, g_ref, sh_ref, sc_ref, w_ref = refs[:6]
        outs = refs[6:]
        x = x_ref[0] + pos_ref[...]
    else:
        x_ref, g_ref, sh_ref, sc_ref, w_ref = refs[:5]
        outs = refs[5:]
        x = x_ref[0]
    ms = jnp.mean(x * x, axis=-1, keepdims=True)
    h = x * lax.rsqrt(ms + RMS_EPS) * g_ref[...]
    h = (h * (1.0 + sc_ref[0]) + sh_ref[0]).astype(BF16)
    for o_ref, (a, b, _) in zip(outs, segs):
        o_ref[0] = _dot(h, w_ref[:, a:b]).astype(o_ref.dtype)


def _in_proj(x, pos, norm_g, mod, mod_row, w, segs, tm):
    bsz, seq, d = x.shape
    add_pos = pos is not None
    in_specs = [pl.BlockSpec((1, tm, d), lambda b, i: (b, i, 0))]
    args = [x]
    if add_pos:
        in_specs.append(pl.BlockSpec((tm, d), lambda b, i: (i, 0)))
        args.append(pos)
    in_specs += [pl.BlockSpec((1, d), lambda b, i: (0, 0)),
                 pl.BlockSpec((1, 1, d), lambda b, i: (mod_row(b), 0, 0)),
                 pl.BlockSpec((1, 1, d), lambda b, i: (mod_row(b), 0, 1)),
                 _const_spec(w.shape)]
    args += [norm_g.reshape(1, d), mod, mod, w]
    out_shape = [jax.ShapeDtypeStruct((bsz, seq, b - a), dt) for a, b, dt in segs]
    out_specs = [pl.BlockSpec((1, tm, b - a), lambda bb, i: (bb, i, 0)) for a, b, _ in segs]
    return pl.pallas_call(
        functools.partial(_inproj_kernel, add_pos=add_pos, segs=tuple(segs)),
        out_shape=out_shape, grid=(bsz, seq // tm), in_specs=in_specs, out_specs=out_specs,
        compiler_params=_cparams(("parallel", "parallel")),
        name="in_proj",
    )(*args)


def _cumsum_rows(x, reverse):
    c = x.shape[0]
    row = lax.broadcasted_iota(jnp.int32, x.shape, 0)
    sh = 1
    while sh < c:
        if reverse:
            x = x + jnp.where(row < c - sh, pltpu.roll(x, c - sh, 0), 0.0)
        else:
            x = x + jnp.where(row >= sh, pltpu.roll(x, sh, 0), 0.0)
        sh *= 2
    return x


def _scan_chunk_fast(q, lf2, k, v, st_ref, h, reverse):
    c = SCAN_CHUNK
    s = SCAN_SUB
    nsub = c // s
    b = _cumsum_rows(lf2, reverse)
    qs = _silu(q) * (HG_HEAD_DIM ** -0.5)
    st = st_ref[h]
    o_inter = _dot_nt((qs * jnp.exp2(b)).astype(BF16), st.astype(BF16))
    v_bf = v.astype(BF16)
    zero_row = jnp.zeros((1, HG_HEAD_DIM), F32)
    zero_blk = jnp.zeros((s, HG_HEAD_DIM), BF16)
    kt = {}
    prev_ref = None
    sc = [None] * nsub
    for i in (range(nsub - 1, -1, -1) if reverse else range(nsub)):
        r0 = i * s
        bi = b[r0:r0 + s]
        if reverse:
            ref = b[r0 + s:r0 + s + 1] if i < nsub - 1 else zero_row
        else:
            ref = b[r0 - 1:r0] if i > 0 else zero_row
        if prev_ref is not None:
            step = jnp.exp2(ref - prev_ref)
            kt = {j: blk * step for j, blk in kt.items()}
        kt[i] = k[r0:r0 + s] * jnp.exp2(ref - bi)
        prev_ref = ref
        qi = (qs[r0:r0 + s] * jnp.exp2(bi - ref)).astype(BF16)
        keys = jnp.concatenate([kt[j].astype(BF16) if j in kt else zero_blk for j in range(nsub)], axis=0)
        sc[i] = _dot_nt(qi, keys)
    scores = jnp.concatenate(sc, axis=0)
    rr = lax.broadcasted_iota(jnp.int32, (c, c), 0)
    cc = lax.broadcasted_iota(jnp.int32, (c, c), 1)
    scores = jnp.where((cc >= rr) if reverse else (cc <= rr), scores, 0.0)
    o = o_inter + _dot(scores.astype(BF16), v_bf)

    b_end = b[0:1] if reverse else b[c - 1:c]
    kd = (k * jnp.exp2(b_end - b)).astype(BF16)
    st_ref[h] = st * jnp.exp2(b_end) + _dot_tn(v_bf, kd)
    return o


def _scan_chunk_exact(q, lf2, k, v, st_ref, h, b_scr, q_scr, reverse):
    c = SCAN_CHUNK
    s = SCAN_SUB
    nsub = c // s
    b = _cumsum_rows(lf2, reverse)
    qs = _silu(q) * (HG_HEAD_DIM ** -0.5)
    st = st_ref[h]
    o_inter = _dot_nt((qs * jnp.exp2(b)).astype(BF16), st.astype(BF16))

    b_scr[...] = b
    q_scr[...] = qs
    v_bf = v.astype(BF16)
    ones = jnp.ones((HG_HEAD_DIM, HG_HEAD_DIM), BF16)
    sub_iota = lax.broadcasted_iota(jnp.int32, (s, HG_HEAD_DIM), 0)
    row_iota = lax.broadcasted_iota(jnp.int32, (c, HG_HEAD_DIM), 0)
    grp = (lax.broadcasted_iota(jnp.int32, (s, s * s), 1) // s ==
           lax.broadcasted_iota(jnp.int32, (s, s * s), 0)).astype(BF16)
    o_parts = []
    for i in range(nsub):
        r0 = i * s
        bi = b[r0:r0 + s]
        ki = k[r0:r0 + s]
        vi = v[r0:r0 + s]
        prods = []
        for t in range(s):
            bt = b_scr[pl.ds(r0 + t, 1), :]
            qt = q_scr[pl.ds(r0 + t, 1), :]
            keep = (sub_iota >= t) if reverse else (sub_iota <= t)
            e = jnp.exp2(jnp.where(keep, bt - bi, NEG_BIG))
            prods.append(((qt * ki) * e).astype(BF16))
        p = jnp.concatenate(prods, axis=0)
        rsum = _dot(p, ones)
        zt = (rsum * jnp.concatenate([vi] * s, axis=0)).astype(BF16)
        o_i = _dot(grp, zt)
        if reverse and i < nsub - 1:
            ref_row = b[r0 + s:r0 + s + 1]
            key_rows = row_iota >= r0 + s
        elif (not reverse) and i > 0:
            ref_row = b[r0 - 1:r0]
            key_rows = row_iota < r0
        else:
            ref_row = None
        if ref_row is not None:
            qi = qs[r0:r0 + s] * jnp.exp2(bi - ref_row)
            ks = k * jnp.exp2(jnp.where(key_rows, ref_row - b, NEG_BIG))
            sc = _dot_nt(qi.astype(BF16), ks.astype(BF16))
            o_i = o_i + _dot(sc.astype(BF16), v_bf)
        o_parts.append(o_i)
    o = o_inter + jnp.concatenate(o_parts, axis=0)

    b_end = b[0:1] if reverse else b[c - 1:c]
    kd = (k * jnp.exp2(b_end - b)).astype(BF16)
    st_ref[h] = st * jnp.exp2(b_end) + _dot_tn(v_bf, kd)
    return o


def _scan_kernel(qf_ref, qb_ref, ff_ref, fb_ref, vf_ref, vb_ref, lb_ref, s0f_ref, s0b_ref,
                 of_ref, ob_ref, sf_ref, sb_ref, stf, stb, lf_scr, k_scr, bf_scr, qf_scr, bb_scr, qb_scr,
                 *, n_heads, n_chunks):
    n = pl.program_id(1)

    @pl.when(n == 0)
    def _():
        stf[...] = s0f_ref[0]
        stb[...] = s0b_ref[0]

    tb = n_chunks * SCAN_CHUNK
    nblk = tb // SCAN_SUB
    sel = (lax.broadcasted_iota(jnp.int32, (nblk, tb), 1) // SCAN_SUB ==
           lax.broadcasted_iota(jnp.int32, (nblk, tb), 0)).astype(BF16)
    worst = jnp.zeros((nblk, lb_ref.shape[1]), F32)
    for d, f_ref in enumerate((ff_ref, fb_ref)):
        lb = lb_ref[d:d + 1, :]
        forget = lb + (1.0 - lb) * jax.nn.sigmoid(f_ref[0])
        lf2 = jnp.log(forget) * LOG2_E
        lf_scr[d] = lf2
        k_scr[d] = 1.0 - forget
        worst = jnp.maximum(worst, -_dot(sel, lf2.astype(BF16)))
    safe = jnp.max(worst) <= SCAN_SAFE_LOG2_RANGE

    def rows_of(ci):
        rf = pl.ds(pl.multiple_of(ci * SCAN_CHUNK, SCAN_CHUNK), SCAN_CHUNK)
        rb = pl.ds(pl.multiple_of((n_chunks - 1 - ci) * SCAN_CHUNK, SCAN_CHUNK), SCAN_CHUNK)
        return rf, rb

    @pl.when(safe)
    def _():
        def chunk_body(ci, carry):
            rf, rb = rows_of(ci)
            for h in range(n_heads):
                cols = slice(h * HG_HEAD_DIM, (h + 1) * HG_HEAD_DIM)
                of_ref[0, rf, cols] = _scan_chunk_fast(qf_ref[0, rf, cols], lf_scr[0, rf, cols],
                                                       k_scr[0, rf, cols], vf_ref[0, rf, cols], stf, h, False)
                ob_ref[0, rb, cols] = _scan_chunk_fast(qb_ref[0, rb, cols], lf_scr[1, rb, cols],
                                                       k_scr[1, rb, cols], vb_ref[0, rb, cols], stb, h, True)
            return carry

        lax.fori_loop(0, n_chunks, chunk_body, 0)

    @pl.when(jnp.logical_not(safe))
    def _():
        def head_body(h, carry):
            cols = pl.ds(pl.multiple_of(h * HG_HEAD_DIM, HG_HEAD_DIM), HG_HEAD_DIM)

            def chunk_body(ci, carry2):
                rf, rb = rows_of(ci)
                of_ref[0, rf, cols] = _scan_chunk_exact(qf_ref[0, rf, cols], lf_scr[0, rf, cols],
                                                        k_scr[0, rf, cols], vf_ref[0, rf, cols], stf, h,
                                                        bf_scr, qf_scr, False)
                ob_ref[0, rb, cols] = _scan_chunk_exact(qb_ref[0, rb, cols], lf_scr[1, rb, cols],
                                                        k_scr[1, rb, cols], vb_ref[0, rb, cols], stb, h,
                                                        bb_scr, qb_scr, True)
                return carry2

            return lax.fori_loop(0, n_chunks, chunk_body, carry)

        lax.fori_loop(0, n_heads, head_body, 0)

    @pl.when(n == pl.num_programs(1) - 1)
    def _():
        sf_ref[0] = stf[...]
        sb_ref[0] = stb[...]


def _hgrn2_scan(z_hg, lb, s0f, s0b, tb):
    bsz, seq, w5 = z_hg.shape
    w = w5 // 5
    nh = w // HG_HEAD_DIM
    nb = seq // tb
    blk = (1, tb, w)
    fwd = lambda j: pl.BlockSpec(blk, lambda b, n: (b, n, j))
    bwd = lambda j: pl.BlockSpec(blk, lambda b, n: (b, nb - 1 - n, j))
    st_spec = pl.BlockSpec((1, nh, HG_HEAD_DIM, HG_HEAD_DIM), lambda b, n: (b, 0, 0, 0))
    st_shape = jax.ShapeDtypeStruct((bsz, nh, HG_HEAD_DIM, HG_HEAD_DIM), F32)
    o_shape = jax.ShapeDtypeStruct((bsz, seq, w), F32)
    return pl.pallas_call(
        functools.partial(_scan_kernel, n_heads=nh, n_chunks=tb // SCAN_CHUNK),
        out_shape=[o_shape, o_shape, st_shape, st_shape],
        grid=(bsz, nb),
        in_specs=[fwd(0), bwd(0), fwd(1), bwd(2), fwd(3), bwd(3),
                  pl.BlockSpec((2, w), lambda b, n: (0, 0)), st_spec, st_spec],
        out_specs=[fwd(0), bwd(0), st_spec, st_spec],
        scratch_shapes=[pltpu.VMEM((nh, HG_HEAD_DIM, HG_HEAD_DIM), F32),
                        pltpu.VMEM((nh, HG_HEAD_DIM, HG_HEAD_DIM), F32),
                        pltpu.VMEM((2, tb, w), F32),
                        pltpu.VMEM((2, tb, w), F32),
                        pltpu.VMEM((SCAN_CHUNK, HG_HEAD_DIM), F32),
                        pltpu.VMEM((SCAN_CHUNK, HG_HEAD_DIM), F32),
                        pltpu.VMEM((SCAN_CHUNK, HG_HEAD_DIM), F32),
                        pltpu.VMEM((SCAN_CHUNK, HG_HEAD_DIM), F32)],
        compiler_params=_cparams(("parallel", "arbitrary")),
        name="hgrn2_scan",
    )(z_hg, z_hg, z_hg, z_hg, z_hg, z_hg, lb, s0f, s0b)


def _dft_mats(n_rows, period):
    f = jnp.arange(n_rows, dtype=jnp.int32)
    m = (f[:, None] * f[None, :]) % period
    ang = m.astype(F32) * (2.0 * math.pi / period)
    return jnp.cos(ang).astype(BF16), jnp.sin(ang).astype(BF16)


def _split_bf16(x):
    hi = x.astype(BF16)
    lo = (x - hi.astype(F32)).astype(BF16)
    return hi, lo


def _filter_spectrum_kernel(hs_ref, hd_ref, c_ref, s_ref, kr_ref, ki_ref, kn_ref, *, scale):
    hs = hs_ref[...]
    hd = hd_ref[...]
    hs_hi, hs_lo = _split_bf16(hs)
    hd_hi, hd_lo = _split_bf16(hd)
    c = c_ref[...]
    s = s_ref[...]
    kr_ref[...] = ((_dot(c, hs_hi) + _dot(c, hs_lo)) * scale).astype(kr_ref.dtype)
    ki_ref[...] = ((_dot(s, hd_hi) + _dot(s, hd_lo)) * (-scale)).astype(ki_ref.dtype)
    row = lax.broadcasted_iota(jnp.int32, hs.shape, 0)
    sign = jnp.where(row % 2 == 0, 1.0, -1.0)
    kn = jnp.sum(hs * sign, axis=0, keepdims=True) * scale
    kn_ref[...] = jnp.broadcast_to(kn, kn_ref.shape)


def _filter_spectrum(hs, hd, cmat, smat):
    seq, ch = hs.shape
    scale = 2.0 / (2 * seq)
    return pl.pallas_call(
        functools.partial(_filter_spectrum_kernel, scale=scale),
        out_shape=[jax.ShapeDtypeStruct((seq, ch), BF16), jax.ShapeDtypeStruct((seq, ch), BF16),
                   jax.ShapeDtypeStruct((8, ch), F32)],
        compiler_params=pltpu.CompilerParams(vmem_limit_bytes=V7X_VMEM_LIMIT_BYTES),
        name="hyena_filter_spectrum",
    )(hs, hd, cmat, smat)


def _hyena_kernel(z_ref, cw_ref, cb_ref, c_ref, s_ref, kr_ref, ki_ref, kn_ref, g_ref, o_ref, *, seq, ch):
    row = lax.broadcasted_iota(jnp.int32, (seq, 1), 0)
    sign = jnp.where(row % 2 == 0, 1.0, -1.0)

    def short_conv(part):
        cols = slice(part * ch, (part + 1) * ch)
        z = z_ref[0, :, cols].astype(F32)
        z_prev = jnp.where(row >= 1, pltpu.roll(z, 1, 0), 0.0)
        z_next = jnp.where(row <= seq - 2, pltpu.roll(z, seq - 1, 0), 0.0)
        return z_prev * cw_ref[0:1, cols] + z * cw_ref[1:2, cols] + z_next * cw_ref[2:3, cols] + cb_ref[:, cols]

    def long_conv(x, order):
        cols = slice(order * ch, (order + 1) * ch)
        xb = x.astype(BF16)
        a = _dot(c_ref[...], xb)
        bm = _dot(s_ref[...], xb)
        kr = kr_ref[:, cols]
        ki = ki_ref[:, cols]
        yr = a * kr + bm * ki
        yi = (a * ki - bm * kr).astype(BF16)
        dc = 0.5 * yr[0:1, :]
        x_nyq = jnp.sum(x * sign, axis=0, keepdims=True)
        y = _dot(c_ref[...], yr.astype(BF16)) - _dot(s_ref[...], yi)
        return y - dc + (0.5 * x_nyq * kn_ref[0:1, cols]) * sign

    y = short_conv(1) * long_conv(short_conv(0), 0)
    y = short_conv(2) * long_conv(y, 1)
    ms = jnp.mean(y * y, axis=-1, keepdims=True)
    o_ref[0] = y * lax.rsqrt(ms + RMS_EPS) * g_ref[...]


def _hyena(z_hy, conv_w, conv_b, cmat, smat, kr, ki, kn, norm_g):
    bsz, seq, c3 = z_hy.shape
    ch = c3 // 3
    return pl.pallas_call(
        functools.partial(_hyena_kernel, seq=seq, ch=ch),
        out_shape=jax.ShapeDtypeStruct((bsz, seq, ch), F32),
        grid=(bsz,),
        in_specs=[pl.BlockSpec((1, seq, c3), lambda b: (b, 0, 0)),
                  _const_spec(conv_w.shape), _const_spec((1, c3)),
                  _const_spec(cmat.shape), _const_spec(smat.shape),
                  _const_spec(kr.shape), _const_spec(ki.shape), _const_spec(kn.shape),
                  _const_spec((1, ch))],
        out_specs=pl.BlockSpec((1, seq, ch), lambda b: (b, 0, 0)),
        compiler_params=_cparams(("parallel",)),
        name="hyena_mixer",
    )(z_hy, conv_w, conv_b.reshape(1, c3), cmat, smat, kr, ki, kn, norm_g.reshape(1, ch))


def _fnet_kernel(z_ref, c_ref, s_ref, bdc_ref, bds_ref, bdw_ref, b_ref, g_ref, o_ref, *, scale):
    zb = z_ref[0].astype(BF16)
    y1 = _dot(c_ref[...], zb)
    y2 = _dot(s_ref[...], zb)
    r = (_dot(y1.astype(BF16), bdc_ref[...]) - _dot(y2.astype(BF16), bds_ref[...])) * scale
    y = _dot(r.astype(BF16), bdw_ref[...]) + b_ref[...]
    ms = jnp.mean(y * y, axis=-1, keepdims=True)
    o_ref[0] = y * lax.rsqrt(ms + RMS_EPS) * g_ref[...]


def _block_diag(blocks):
    g, a, b = blocks.shape
    out = jnp.zeros((g * a, g * b), blocks.dtype)
    for i in range(g):
        out = out.at[i * a:(i + 1) * a, i * b:(i + 1) * b].set(blocks[i])
    return out


def _fnet(z_fn, cmat, smat, fn_w, fn_b, norm_g):
    bsz, seq, ch = z_fn.shape
    gd = ch // FN_GROUPS
    k = np.arange(gd)
    ang = 2.0 * np.pi * ((k[:, None] * k[None, :]) % gd) / gd
    eye = np.eye(FN_GROUPS)
    bdc = jnp.asarray(np.kron(eye, np.cos(ang)), BF16)
    bds = jnp.asarray(np.kron(eye, np.sin(ang)), BF16)
    bdw = _block_diag(fn_w).astype(BF16)
    scale = 1.0 / math.sqrt(seq * gd)
    return pl.pallas_call(
        functools.partial(_fnet_kernel, scale=scale),
        out_shape=jax.ShapeDtypeStruct((bsz, seq, ch), F32),
        grid=(bsz,),
        in_specs=[pl.BlockSpec((1, seq, ch), lambda b: (b, 0, 0)),
                  _const_spec(cmat.shape), _const_spec(smat.shape),
                  _const_spec((ch, ch)), _const_spec((ch, ch)), _const_spec((ch, ch)),
                  _const_spec((1, ch)), _const_spec((1, ch))],
        out_specs=pl.BlockSpec((1, seq, ch), lambda b: (b, 0, 0)),
        compiler_params=_cparams(("parallel",)),
        name="fnet_mixer",
    )(z_fn, cmat, smat, bdc, bds, bdw, fn_b.reshape(1, ch), norm_g.reshape(1, ch))


def _route(logits_t, rb_ref):
    rows = [logits_t[e:e + 1, :] for e in range(N_EXPERTS)]
    mx = functools.reduce(jnp.maximum, rows)
    ex = [jnp.exp(r - mx) for r in rows]
    inv = 1.0 / functools.reduce(lambda a, b: a + b, ex)
    probs = [e * inv for e in ex]
    sel = [probs[e] + rb_ref[e:e + 1, 0:1] for e in range(N_EXPERTS)]
    epg = EXPERTS_PER_GROUP
    gscore = []
    for g in range(N_GROUPS):
        s = sel[g * epg:(g + 1) * epg]
        pairs = [s[i] + s[j] for i in range(epg) for j in range(i + 1, epg)]
        gscore.append(functools.reduce(jnp.maximum, pairs))
    best = gscore[0]
    best_g = jnp.zeros_like(best, dtype=jnp.int32)
    for g in range(1, N_GROUPS):
        better = gscore[g] > best
        best_g = jnp.where(better, g, best_g)
        best = jnp.where(better, gscore[g], best)
    chosen = []
    for e in range(N_EXPERTS):
        g = e // epg
        beaten = jnp.zeros_like(best_g)
        for e2 in range(g * epg, (g + 1) * epg):
            if e2 == e:
                continue
            wins = (sel[e2] > sel[e]) | ((sel[e2] == sel[e]) & (e2 < e))
            beaten = beaten + wins.astype(jnp.int32)
        chosen.append(jnp.where((best_g == g) & (beaten < 2), probs[e], 0.0))
    tot = functools.reduce(lambda a, b: a + b, chosen)
    inv_tot = 1.0 / tot
    return jnp.concatenate([c * inv_tot for c in chosen], axis=0)


def _outproj_kernel(*refs, add_pos, n_heads):
    if add_pos:
        x_ref, pos_ref = refs[:2]
        rest = refs[2:]
        x = x_ref[0] + pos_ref[...]
    else:
        x_ref = refs[0]
        rest = refs[1:]
        x = x_ref[0]
    (yhy_ref, of_ref, ob_ref, g_ref, yfn_ref, wout_ref, hgg_ref, gate_ref, n2g_ref, sh_ref, sc_ref,
     rwt_ref, rb_ref, xo_ref, h_ref, comb_ref) = rest
    o = of_ref[0] + ob_ref[0]
    parts = []
    for h in range(n_heads):
        oh = o[:, h * HG_HEAD_DIM:(h + 1) * HG_HEAD_DIM]
        parts.append(oh * lax.rsqrt(jnp.mean(oh * oh, axis=-1, keepdims=True) + RMS_EPS))
    y_hg = jnp.concatenate(parts, axis=-1) * hgg_ref[...] * _silu(g_ref[0])
    c_hy = yhy_ref.shape[-1]
    c_hg = y_hg.shape[-1]
    mix = (_dot(yhy_ref[0].astype(BF16), wout_ref[0:c_hy, :]) +
           _dot(y_hg.astype(BF16), wout_ref[c_hy:c_hy + c_hg, :]) +
           _dot(yfn_ref[0].astype(BF16), wout_ref[c_hy + c_hg:, :]))
    xn = x + gate_ref[0] * mix
    xo_ref[0] = xn
    ms = jnp.mean(xn * xn, axis=-1, keepdims=True)
    h2 = xn * lax.rsqrt(ms + RMS_EPS) * n2g_ref[...]
    h2 = h2 * (1.0 + sc_ref[0]) + sh_ref[0]
    h_ref[0] = h2.astype(BF16)
    logits_t = _dot_nt(rwt_ref[...], h2, precision=HIGHEST)
    comb_ref[0] = _route(logits_t, rb_ref)


def _out_proj(x, pos, y_hy, o_f, o_b, z_hg, y_fn, w_out, hg_norm_g, mod, mod_row, norm2_g, rwt, rb, tm):
    bsz, seq, d = x.shape
    add_pos = pos is not None
    c_hy, c_hg, c_fn = y_hy.shape[-1], o_f.shape[-1], y_fn.shape[-1]
    tok = lambda c, j=0: pl.BlockSpec((1, tm, c), lambda b, i: (b, i, j))
    modk = lambda k: pl.BlockSpec((1, 1, d), lambda b, i: (mod_row(b), 0, k))
    in_specs = [tok(d)]
    args = [x]
    if add_pos:
        in_specs.append(pl.BlockSpec((tm, d), lambda b, i: (i, 0)))
        args.append(pos)
    in_specs += [tok(c_hy), tok(c_hg), tok(c_hg), tok(c_hg, 4), tok(c_fn),
                 _const_spec(w_out.shape), _const_spec((1, c_hg)), modk(2), _const_spec((1, d)),
                 modk(3), modk(4), _const_spec(rwt.shape), _const_spec(rb.shape)]
    args += [y_hy, o_f, o_b, z_hg, y_fn, w_out, hg_norm_g.reshape(1, c_hg), mod, norm2_g.reshape(1, d),
             mod, mod, rwt, rb]
    nt = seq // tm
    out_shape = [jax.ShapeDtypeStruct((bsz, seq, d), F32), jax.ShapeDtypeStruct((bsz, seq, d), BF16),
                 jax.ShapeDtypeStruct((bsz * nt, N_EXPERTS, tm), F32)]
    out_specs = [tok(d), tok(d), pl.BlockSpec((1, N_EXPERTS, tm), lambda b, i: (b * nt + i, 0, 0))]
    return pl.pallas_call(
        functools.partial(_outproj_kernel, add_pos=add_pos, n_heads=c_hg // HG_HEAD_DIM),
        out_shape=out_shape, grid=(bsz, nt), in_specs=in_specs, out_specs=out_specs,
        compiler_params=_cparams(("parallel", "parallel")),
        name="out_proj_router",
    )(*args)


def _moe_kernel(*refs, final, cap):
    if final:
        (h_ref, comb_ref, combt_ref, wg_ref, wu_ref, wd_ref, x_ref, gate_ref, fg_ref, o_ref,
         acc_ref, rcol_ref, rrow_ref) = refs
    else:
        (h_ref, comb_ref, combt_ref, wg_ref, wu_ref, wd_ref, x_ref, gate_ref, o_ref,
         acc_ref, rcol_ref, rrow_ref) = refs
    e = pl.program_id(1)
    tm = h_ref.shape[0]
    n_rt = combt_ref.shape[0]

    @pl.when(e == 0)
    def _():
        acc_ref[...] = jnp.zeros_like(acc_ref)
        ti = lax.broadcasted_iota(jnp.int32, (tm, tm), 0)
        tj = lax.broadcasted_iota(jnp.int32, (tm, tm), 1)
        before_col = jnp.where(tj < ti, 1.0, 0.0).astype(BF16)
        before_row = jnp.where(ti < tj, 1.0, 0.0).astype(BF16)
        sel_col = jnp.where(comb_ref[...] > 0.0, 1.0, 0.0).astype(BF16)
        combt = jnp.concatenate([combt_ref[j] for j in range(n_rt)], axis=1)
        sel_row = jnp.where(combt > 0.0, 1.0, 0.0).astype(BF16)
        rcol_ref[...] = _dot(before_col, sel_col)
        rrow_ref[...] = _dot(sel_row, before_row)

    h = h_ref[...]
    comb = comb_ref[...]
    lane = lax.broadcasted_iota(jnp.int32, comb.shape, 1)
    ce = jnp.sum(jnp.where(lane == e, comb, 0.0), axis=-1, keepdims=True)
    ce_row = jnp.concatenate([combt_ref[j, pl.ds(e, 1), :] for j in range(n_rt)], axis=1)
    count = jnp.sum(jnp.where(ce_row > 0.0, 1.0, 0.0))
    fits = count <= cap

    @pl.when(fits)
    def _():
        rank_col = jnp.sum(jnp.where(lane == e, rcol_ref[...], 0.0), axis=-1, keepdims=True)
        rank_row = rrow_ref[pl.ds(e, 1), :]
        slot_r = lax.broadcasted_iota(jnp.int32, (cap, tm), 0).astype(F32)
        pick = jnp.where((slot_r == rank_row) & (ce_row > 0.0), 1.0, 0.0).astype(BF16)
        hc = _dot(pick, h).astype(BF16)
        a = _silu(_dot(hc, wg_ref[0])) * _dot(hc, wu_ref[0])
        y = _dot(a.astype(BF16), wd_ref[0]).astype(BF16)
        slot_c = lax.broadcasted_iota(jnp.int32, (tm, cap), 1).astype(F32)
        spread = jnp.where((slot_c == rank_col) & (ce > 0.0), 1.0, 0.0).astype(BF16)
        acc_ref[...] += ce * _dot(spread, y)

    @pl.when(jnp.logical_not(fits))
    def _():
        a = _silu(_dot(h, wg_ref[0])) * _dot(h, wu_ref[0])
        acc_ref[...] += ce * _dot(a.astype(BF16), wd_ref[0])

    @pl.when(e == pl.num_programs(1) - 1)
    def _():
        y = x_ref[...] + gate_ref[0] * acc_ref[...]
        if final:
            ms = jnp.mean(y * y, axis=-1, keepdims=True)
            y = y * lax.rsqrt(ms + RMS_EPS) * fg_ref[...]
        o_ref[...] = y


def _moe(h, comb_t, wg, wu, wd, x, mod, mod_row_of_tile, final_g, tm):
    n, d = h.shape
    ne, _, de = wg.shape
    final = final_g is not None
    tr = comb_t.shape[-1]
    n_rt = tm // tr
    lanes = 128
    comb = jnp.transpose(comb_t, (0, 2, 1)).reshape(n, ne)
    comb = jnp.pad(comb, ((0, 0), (0, lanes - ne)))
    cap = min(MOE_CAP, tm)
    in_specs = [pl.BlockSpec((tm, d), lambda i, e: (i, 0)),
                pl.BlockSpec((tm, lanes), lambda i, e: (i, 0)),
                pl.BlockSpec((n_rt, ne, tr), lambda i, e: (i, 0, 0)),
                pl.BlockSpec((1, d, de), lambda i, e: (e, 0, 0)),
                pl.BlockSpec((1, d, de), lambda i, e: (e, 0, 0)),
                pl.BlockSpec((1, de, d), lambda i, e: (e, 0, 0)),
                pl.BlockSpec((tm, d), lambda i, e: (i, 0)),
                pl.BlockSpec((1, 1, d), lambda i, e: (mod_row_of_tile(i), 0, 5))]
    args = [h, comb, comb_t, wg, wu, wd, x, mod]
    if final:
        in_specs.append(pl.BlockSpec((1, d), lambda i, e: (0, 0)))
        args.append(final_g.reshape(1, d))
    return pl.pallas_call(
        functools.partial(_moe_kernel, final=final, cap=cap),
        out_shape=jax.ShapeDtypeStruct((n, d), F32),
        grid=(n // tm, ne), in_specs=in_specs,
        out_specs=pl.BlockSpec((tm, d), lambda i, e: (i, 0)),
        scratch_shapes=[pltpu.VMEM((tm, d), F32), pltpu.VMEM((tm, lanes), F32), pltpu.VMEM((ne, tm), F32)],
        compiler_params=_cparams(("parallel", "arbitrary")),
        name="moe_experts",
    )(*args)


def _grid_sincos(n_tok, d):
    rows = n_tok // GRID_W
    row = jnp.repeat(jnp.arange(rows, dtype=F32), GRID_W)
    col = jnp.tile(jnp.arange(GRID_W, dtype=F32), rows)
    quarter = d // 4
    omega = 1.0 / (POS_BASE ** (jnp.arange(quarter, dtype=F32) / quarter))
    ar = row[:, None] * omega
    ac = col[:, None] * omega
    return jnp.concatenate([jnp.sin(ar), jnp.cos(ar), jnp.sin(ac), jnp.cos(ac)], axis=-1)


def _hyena_filters(seq, ch, w1, b1, w2, b2, w3, freq):
    t = jnp.linspace(0.0, 1.0, seq, dtype=F32)[:, None]
    w = 2.0 * math.pi * jnp.arange(seq, dtype=F32)[:, None] / seq
    bands = jnp.linspace(1e-4, HY_BANDS - 1, HY_BANDS, dtype=F32)[None, :]
    z = jnp.concatenate([t, jnp.cos(bands * w), -jnp.sin(bands * w)], axis=-1)
    hp = functools.partial(jnp.dot, precision=HIGHEST)
    h = jnp.sin(freq * (hp(z, w1) + b1))
    h = jnp.sin(freq * (hp(h, w2) + b2))
    h = hp(h, w3).reshape(seq, 2 * HY_ORDER, ch)
    max_decay = math.log(HY_DECAY_TARGET) / HY_FAST_DECAY_PCT
    min_decay = math.log(HY_DECAY_TARGET) / HY_SLOW_DECAY_PCT
    deltas = jnp.linspace(min_decay, max_decay, ch, dtype=F32)
    return h * jnp.exp(-t[:, :, None] * jnp.abs(deltas))


def _filter_halves(h, bias):
    seq = h.shape[0]
    not0 = (jnp.arange(seq) > 0)[:, None].astype(F32)
    hs, hd = [], []
    for o in range(HY_ORDER):
        hf = h[:, 2 * o].at[0].add(bias[o])
        hb = h[:, 2 * o + 1] * not0
        hs.append(hf + hb)
        hd.append(hf - hb)
    return jnp.concatenate(hs, axis=-1), jnp.concatenate(hd, axis=-1)


def kernel(x, c, ctx, c_ctx, ada_w, ada_b, norm1_g, norm2_g, w_in, w_out, hy_conv_w, hy_conv_b, hy_filt_w1, hy_filt_b1, hy_filt_w2, hy_filt_b2, hy_filt_w3, hy_filt_freq, hy_bias, hy_norm_g, hg_lower_bounds, hg_norm_g, fn_w, fn_b, fn_norm_g, router_w, router_b, moe_w_gate, moe_w_up, moe_w_down, final_norm_g):
    bsz, seq, d = x.shape
    n_ctx = ctx.shape[1]
    depth = ada_w.shape[0]
    c_hy = hy_norm_g.shape[-1]
    c_hyp = hy_conv_w.shape[-1]
    w_hg = hg_norm_g.shape[-1]
    c_fn = fn_norm_g.shape[-1]
    hg_lo, hg_hi = c_hyp, c_hyp + 5 * w_hg
    nh = w_hg // HG_HEAD_DIM

    cs = jnp.cumsum(jax.nn.softmax(hg_lower_bounds.astype(F32), axis=0), axis=0)
    lower_bounds = cs - cs[0:1]
    pos = _grid_sincos(seq, d)

    n_rows = -(-(bsz + 1) // 8) * 8
    cond = jnp.zeros((n_rows, d), F32).at[:bsz].set(c).at[bsz].set(c_ctx)
    mods = _ada_modulation(cond, ada_w, ada_b)
    x_row = lambda b: b
    c_row = lambda b: bsz

    dft = {}
    for n in (seq, n_ctx):
        dft[("hy", n)] = _dft_mats(n, 2 * n)
        dft[("fn", n)] = _dft_mats(n, n)

    rwt = jnp.transpose(router_w).astype(F32)
    rb = jnp.broadcast_to(router_b.astype(F32)[:, None], (N_EXPERTS, 128))
    s_zero = jnp.zeros((bsz, nh, HG_HEAD_DIM, HG_HEAD_DIM), F32)

    tm_x = min(512, seq)
    tm_c = min(256, n_ctx)
    tb_x = min(256, seq)
    tb_c = min(256, n_ctx)
    segs = [(hg_lo, hg_hi, F32), (0, c_hyp, BF16), (hg_hi, hg_hi + c_fn, BF16)]

    def mixer(z_hy, z_fn, n, layer):
        cmat, smat = dft[("hy", n)]
        h = _hyena_filters(n, c_hy, hy_filt_w1[layer], hy_filt_b1[layer], hy_filt_w2[layer],
                           hy_filt_b2[layer], hy_filt_w3[layer], hy_filt_freq[layer])
        hs, hd = _filter_halves(h, hy_bias[layer])
        kr, ki, kn = _filter_spectrum(hs, hd, cmat, smat)
        y_hy = _hyena(z_hy, hy_conv_w[layer], hy_conv_b[layer], cmat, smat, kr, ki, kn, hy_norm_g[layer])
        fc, fs = dft[("fn", n)]
        y_fn = _fnet(z_fn, fc, fs, fn_w[layer], fn_b[layer], fn_norm_g[layer])
        return y_hy, y_fn

    for layer in range(depth):
        last = layer == depth - 1
        mod = mods[layer].reshape(n_rows, 1, N_MOD * d)
        w_in_b = w_in[layer].astype(BF16)
        w_out_b = w_out[layer].astype(BF16)
        wg = moe_w_gate[layer].astype(BF16)
        wu = moe_w_up[layer].astype(BF16)
        wd = moe_w_down[layer].astype(BF16)
        lb = lower_bounds[layer]
        x_pos = pos if layer == 0 else None

        if last:
            (zc_hg,) = _in_proj(ctx, None, norm1_g[layer], mod, c_row, w_in_b[:, hg_lo:hg_hi],
                                [(0, 5 * w_hg, F32)], tm_c)
        else:
            zc_hg, zc_hy, zc_fn = _in_proj(ctx, None, norm1_g[layer], mod, c_row, w_in_b, segs, tm_c)
        oc_f, oc_b, s_f, s_b = _hgrn2_scan(zc_hg, lb, s_zero, s_zero, tb_c)

        zx_hg, zx_hy, zx_fn = _in_proj(x, x_pos, norm1_g[layer], mod, x_row, w_in_b, segs, tm_x)
        ox_f, ox_b, _, _ = _hgrn2_scan(zx_hg, lb, s_f, s_b, tb_x)
        yx_hy, yx_fn = mixer(zx_hy, zx_fn, seq, layer)
        x_mid, hx2, comb_x = _out_proj(x, x_pos, yx_hy, ox_f, ox_b, zx_hg, yx_fn, w_out_b, hg_norm_g[layer],
                                       mod, x_row, norm2_g[layer], rwt, rb, tm_x)
        tm_moe = min(1024, seq)
        tiles_per_seq = seq // tm_moe
        x = _moe(hx2.reshape(bsz * seq, d), comb_x, wg, wu, wd, x_mid.reshape(bsz * seq, d), mod,
                 lambda i: i // tiles_per_seq, final_norm_g if last else None, tm_moe).reshape(bsz, seq, d)

        if not last:
            yc_hy, yc_fn = mixer(zc_hy, zc_fn, n_ctx, layer)
            c_mid, hc2, comb_c = _out_proj(ctx, None, yc_hy, oc_f, oc_b, zc_hg, yc_fn, w_out_b,
                                           hg_norm_g[layer], mod, c_row, norm2_g[layer], rwt, rb, tm_c)
            tm_mc = min(1024, bsz * n_ctx)
            ctx = _moe(hc2.reshape(bsz * n_ctx, d), comb_c, wg, wu, wd, c_mid.reshape(bsz * n_ctx, d), mod,
                       lambda i: bsz, None, tm_mc).reshape(bsz, n_ctx, d)
    return x
```

```python
import functools
import math

import numpy as np
import jax
import jax.numpy as jnp
from jax import lax
from jax.experimental import pallas as pl
from jax.experimental.pallas import tpu as pltpu

F32 = jnp.float32
BF16 = jnp.bfloat16
HIGHEST = lax.Precision.HIGHEST

GRID_W = 64
HY_ORDER = 2
HY_BANDS = 16
HY_DECAY_TARGET = 1e-2
HY_FAST_DECAY_PCT = 0.3
HY_SLOW_DECAY_PCT = 1.5
HG_HEAD_DIM = 128
FN_GROUPS = 4
N_EXPERTS = 16
N_GROUPS = 4
EXPERTS_PER_GROUP = N_EXPERTS // N_GROUPS
N_MOD = 6
RMS_EPS = 1e-6
POS_BASE = 10000.0

V7X_VMEM_LIMIT_BYTES = 56 * 1024 * 1024
SCAN_CHUNK = 64
SCAN_SUB = 16
SCAN_SAFE_LOG2_RANGE = 80.0
LOG2_E = 1.0 / math.log(2.0)
MOE_CAP = 192
NEG_BIG = -1e30


def _cparams(sem):
    return pltpu.CompilerParams(dimension_semantics=sem, vmem_limit_bytes=V7X_VMEM_LIMIT_BYTES)


def _const_spec(shape):
    nd = len(shape)
    return pl.BlockSpec(shape, lambda *_: (0,) * nd, pipeline_mode=pl.Buffered(1))


def _silu(x):
    return x * jax.nn.sigmoid(x)


def _dot(a, b):
    return jnp.dot(a, b, preferred_element_type=F32)


def _dot_nt(a, b, precision=None):
    return lax.dot_general(a, b, (((1,), (1,)), ((), ())), precision=precision,
                           preferred_element_type=F32)


def _dot_tn(a, b):
    return lax.dot_general(a, b, (((0,), (0,)), ((), ())), preferred_element_type=F32)


def _ada_kernel(c_ref, w_ref, b_ref, o_ref):
    s = _silu(c_ref[...])
    o_ref[0] = jnp.dot(s, w_ref[0], precision=HIGHEST, preferred_element_type=F32) + b_ref[0]


def _ada_modulation(cond, ada_w, ada_b):
    depth, d, nd = ada_w.shape
    rows = cond.shape[0]
    tn = 1536
    return pl.pallas_call(
        _ada_kernel,
        out_shape=jax.ShapeDtypeStruct((depth, rows, nd), F32),
        grid=(depth, nd // tn),
        in_specs=[pl.BlockSpec((rows, d), lambda l, j: (0, 0)),
                  pl.BlockSpec((1, d, tn), lambda l, j: (l, 0, j)),
                  pl.BlockSpec((1, 1, tn), lambda l, j: (l, 0, j))],
        out_specs=pl.BlockSpec((1, rows, tn), lambda l, j: (l, 0, j)),
        compiler_params=_cparams(("parallel", "parallel")),
        name="ada_modulation",
    )(cond, ada_w, ada_b.reshape(depth, 1, nd))


def _inproj_kernel(*refs, add_pos, segs):
    if add_pos:
        x_ref, pos_ref, g_ref, sh_ref, sc_ref, w_ref = refs[:6]
        outs = refs[6:]
        x = x_ref[0] + pos_ref[...]
    else:
        x_ref, g_ref, sh_ref, sc_ref, w_ref = refs[:5]
        outs = refs[5:]
        x = x_ref[0]
    ms = jnp.mean(x * x, axis=-1, keepdims=True)
    h = x * lax.rsqrt(ms + RMS_EPS) * g_ref[...]
    h = (h * (1.0 + sc_ref[0]) + sh_ref[0]).astype(BF16)
    for o_ref, (a, b, _) in zip(outs, segs):
        o_ref[0] = _dot(h, w_ref[:, a:b]).astype(o_ref.dtype)


def _in_proj(x, pos, norm_g, mod, mod_row, w, segs, tm):
    bsz, seq, d = x.shape
    add_pos = pos is not None
    in_specs = [pl.BlockSpec((1, tm, d), lambda b, i: (b, i, 0))]
    args = [x]
    if add_pos:
        in_specs.append(pl.BlockSpec((tm, d), lambda b, i: (i, 0)))
        args.append(pos)
    in_specs += [pl.BlockSpec((1, d), lambda b, i: (0, 0)),
                 pl.BlockSpec((1, 1, d), lambda b, i: (mod_row(b), 0, 0)),
                 pl.BlockSpec((1, 1, d), lambda b, i: (mod_row(b), 0, 1)),
                 _const_spec(w.shape)]
    args += [norm_g.reshape(1, d), mod, mod, w]
    out_shape = [jax.ShapeDtypeStruct((bsz, seq, b - a), dt) for a, b, dt in segs]
    out_specs = [pl.BlockSpec((1, tm, b - a), lambda bb, i: (bb, i, 0)) for a, b, _ in segs]
    return pl.pallas_call(
        functools.partial(_inproj_kernel, add_pos=add_pos, segs=tuple(segs)),
        out_shape=out_shape, grid=(bsz, seq // tm), in_specs=in_specs, out_specs=out_specs,
        compiler_params=_cparams(("parallel", "parallel")),
        name="in_proj",
    )(*args)


def _cumsum_rows(x, reverse):
    c = x.shape[0]
    row = lax.broadcasted_iota(jnp.int32, x.shape, 0)
    sh = 1
    while sh < c:
        if reverse:
            x = x + jnp.where(row < c - sh, pltpu.roll(x, c - sh, 0), 0.0)
        else:
            x = x + jnp.where(row >= sh, pltpu.roll(x, sh, 0), 0.0)
        sh *= 2
    return x


def _scan_chunk_fast(q, lf2, k, v, st_ref, h, reverse):
    c = SCAN_CHUNK
    s = SCAN_SUB
    nsub = c // s
    b = _cumsum_rows(lf2, reverse)
    qs = _silu(q) * (HG_HEAD_DIM ** -0.5)
    st = st_ref[h]
    o_inter = _dot_nt((qs * jnp.exp2(b)).astype(BF16), st.astype(BF16))
    v_bf = v.astype(BF16)
    zero_row = jnp.zeros((1, HG_HEAD_DIM), F32)
    zero_blk = jnp.zeros((s, HG_HEAD_DIM), BF16)
    kt = {}
    prev_ref = None
    sc = [None] * nsub
    for i in (range(nsub - 1, -1, -1) if reverse else range(nsub)):
        r0 = i * s
        bi = b[r0:r0 + s]
        if reverse:
            ref = b[r0 + s:r0 + s + 1] if i < nsub - 1 else zero_row
        else:
            ref = b[r0 - 1:r0] if i > 0 else zero_row
        if prev_ref is not None:
            step = jnp.exp2(ref - prev_ref)
            kt = {j: blk * step for j, blk in kt.items()}
        kt[i] = k[r0:r0 + s] * jnp.exp2(ref - bi)
        prev_ref = ref
        qi = (qs[r0:r0 + s] * jnp.exp2(bi - ref)).astype(BF16)
        keys = jnp.concatenate([kt[j].astype(BF16) if j in kt else zero_blk for j in range(nsub)], axis=0)
        sc[i] = _dot_nt(qi, keys)
    scores = jnp.concatenate(sc, axis=0)
    rr = lax.broadcasted_iota(jnp.int32, (c, c), 0)
    cc = lax.broadcasted_iota(jnp.int32, (c, c), 1)
    scores = jnp.where((cc >= rr) if reverse else (cc <= rr), scores, 0.0)
    o = o_inter + _dot(scores.astype(BF16), v_bf)

    b_end = b[0:1] if reverse else b[c - 1:c]
    kd = (k * jnp.exp2(b_end - b)).astype(BF16)
    st_ref[h] = st * jnp.exp2(b_end) + _dot_tn(v_bf, kd)
    return o


def _scan_chunk_exact(q, lf2, k, v, st_ref, h, b_scr, q_scr, reverse):
    c = SCAN_CHUNK
    s = SCAN_SUB
    nsub = c // s
    b = _cumsum_rows(lf2, reverse)
    qs = _silu(q) * (HG_HEAD_DIM ** -0.5)
    st = st_ref[h]
    o_inter = _dot_nt((qs * jnp.exp2(b)).astype(BF16), st.astype(BF16))

    b_scr[...] = b
    q_scr[...] = qs
    v_bf = v.astype(BF16)
    ones = jnp.ones((HG_HEAD_DIM, HG_HEAD_DIM), BF16)
    sub_iota = lax.broadcasted_iota(jnp.int32, (s, HG_HEAD_DIM), 0)
    row_iota = lax.broadcasted_iota(jnp.int32, (c, HG_HEAD_DIM), 0)
    grp = (lax.broadcasted_iota(jnp.int32, (s, s * s), 1) // s ==
           lax.broadcasted_iota(jnp.int32, (s, s * s), 0)).astype(BF16)
    o_parts = []
    for i in range(nsub):
        r0 = i * s
        bi = b[r0:r0 + s]
        ki = k[r0:r0 + s]
        vi = v[r0:r0 + s]
        prods = []
        for t in range(s):
            bt = b_scr[pl.ds(r0 + t, 1), :]
            qt = q_scr[pl.ds(r0 + t, 1), :]
            keep = (sub_iota >= t) if reverse else (sub_iota <= t)
            e = jnp.exp2(jnp.where(keep, bt - bi, NEG_BIG))
            prods.append(((qt * ki) * e).astype(BF16))
        p = jnp.concatenate(prods, axis=0)
        rsum = _dot(p, ones)
        zt = (rsum * jnp.concatenate([vi] * s, axis=0)).astype(BF16)
        o_i = _dot(grp, zt)
        if reverse and i < nsub - 1:
            ref_row = b[r0 + s:r0 + s + 1]
            key_rows = row_iota >= r0 + s
        elif (not reverse) and i > 0:
            ref_row = b[r0 - 1:r0]
            key_rows = row_iota < r0
        else:
            ref_row = None
        if ref_row is not None:
            qi = qs[r0:r0 + s] * jnp.exp2(bi - ref_row)
            ks = k * jnp.exp2(jnp.where(key_rows, ref_row - b, NEG_BIG))
            sc = _dot_nt(qi.astype(BF16), ks.astype(BF16))
            o_i = o_i + _dot(sc.astype(BF16), v_bf)
        o_parts.append(o_i)
    o = o_inter + jnp.concatenate(o_parts, axis=0)

    b_end = b[0:1] if reverse else b[c - 1:c]
    kd = (k * jnp.exp2(b_end - b)).astype(BF16)
    st_ref[h] = st * jnp.exp2(b_end) + _dot_tn(v_bf, kd)
    return o


def _scan_kernel(qf_ref, qb_ref, ff_ref, fb_ref, vf_ref, vb_ref, lb_ref, s0f_ref, s0b_ref,
                 of_ref, ob_ref, sf_ref, sb_ref, stf, stb, lf_scr, k_scr, bf_scr, qf_scr, bb_scr, qb_scr,
                 *, n_heads, n_chunks):
    n = pl.program_id(1)

    @pl.when(n == 0)
    def _():
        stf[...] = s0f_ref[0]
        stb[...] = s0b_ref[0]

    tb = n_chunks * SCAN_CHUNK
    nblk = tb // SCAN_SUB
    sel = (lax.broadcasted_iota(jnp.int32, (nblk, tb), 1) // SCAN_SUB ==
           lax.broadcasted_iota(jnp.int32, (nblk, tb), 0)).astype(BF16)
    worst = jnp.zeros((nblk, lb_ref.shape[1]), F32)
    for d, f_ref in enumerate((ff_ref, fb_ref)):
        lb = lb_ref[d:d + 1, :]
        forget = lb + (1.0 - lb) * jax.nn.sigmoid(f_ref[0])
        lf2 = jnp.log(forget) * LOG2_E
        lf_scr[d] = lf2
        k_scr[d] = 1.0 - forget
        worst = jnp.maximum(worst, -_dot(sel, lf2.astype(BF16)))
    safe = jnp.max(worst) <= SCAN_SAFE_LOG2_RANGE

    def rows_of(ci):
        rf = pl.ds(pl.multiple_of(ci * SCAN_CHUNK, SCAN_CHUNK), SCAN_CHUNK)
        rb = pl.ds(pl.multiple_of((n_chunks - 1 - ci) * SCAN_CHUNK, SCAN_CHUNK), SCAN_CHUNK)
        return rf, rb

    @pl.when(safe)
    def _():
        def chunk_body(ci, carry):
            rf, rb = rows_of(ci)
            for h in range(n_heads):
                cols = slice(h * HG_HEAD_DIM, (h + 1) * HG_HEAD_DIM)
                of_ref[0, rf, cols] = _scan_chunk_fast(qf_ref[0, rf, cols], lf_scr[0, rf, cols],
                                                       k_scr[0, rf, cols], vf_ref[0, rf, cols], stf, h, False)
                ob_ref[0, rb, cols] = _scan_chunk_fast(qb_ref[0, rb, cols], lf_scr[1, rb, cols],
                                                       k_scr[1, rb, cols], vb_ref[0, rb, cols], stb, h, True)
            return carry

        lax.fori_loop(0, n_chunks, chunk_body, 0)

    @pl.when(jnp.logical_not(safe))
    def _():
        def head_body(h, carry):
            cols = pl.ds(pl.multiple_of(h * HG_HEAD_DIM, HG_HEAD_DIM), HG_HEAD_DIM)

            def chunk_body(ci, carry2):
                rf, rb = rows_of(ci)
                of_ref[0, rf, cols] = _scan_chunk_exact(qf_ref[0, rf, cols], lf_scr[0, rf, cols],
                                                        k_scr[0, rf, cols], vf_ref[0, rf, cols], stf, h,
                                                        bf_scr, qf_scr, False)
                ob_ref[0, rb, cols] = _scan_chunk_exact(qb_ref[0, rb, cols], lf_scr[1, rb, cols],
                                                        k_scr[1, rb, cols], vb_ref[0, rb, cols], stb, h,
                                                        bb_scr, qb_scr, True)
                return carry2

            return lax.fori_loop(0, n_chunks, chunk_body, carry)

        lax.fori_loop(0, n_heads, head_body, 0)

    @pl.when(n == pl.num_programs(1) - 1)
    def _():
        sf_ref[0] = stf[...]
        sb_ref[0] = stb[...]


def _hgrn2_scan(z_hg, lb, s0f, s0b, tb):
    bsz, seq, w5 = z_hg.shape
    w = w5 // 5
    nh = w // HG_HEAD_DIM
    nb = seq // tb
    blk = (1, tb, w)
    fwd = lambda j: pl.BlockSpec(blk, lambda b, n: (b, n, j))
    bwd = lambda j: pl.BlockSpec(blk, lambda b, n: (b, nb - 1 - n, j))
    st_spec = pl.BlockSpec((1, nh, HG_HEAD_DIM, HG_HEAD_DIM), lambda b, n: (b, 0, 0, 0))
    st_shape = jax.ShapeDtypeStruct((bsz, nh, HG_HEAD_DIM, HG_HEAD_DIM), F32)
    o_shape = jax.ShapeDtypeStruct((bsz, seq, w), F32)
    return pl.pallas_call(
        functools.partial(_scan_kernel, n_heads=nh, n_chunks=tb // SCAN_CHUNK),
        out_shape=[o_shape, o_shape, st_shape, st_shape],
        grid=(bsz, nb),
        in_specs=[fwd(0), bwd(0), fwd(1), bwd(2), fwd(3), bwd(3),
                  pl.BlockSpec((2, w), lambda b, n: (0, 0)), st_spec, st_spec],
        out_specs=[fwd(0), bwd(0), st_spec, st_spec],
        scratch_shapes=[pltpu.VMEM((nh, HG_HEAD_DIM, HG_HEAD_DIM), F32),
                        pltpu.VMEM((nh, HG_HEAD_DIM, HG_HEAD_DIM), F32),
                        pltpu.VMEM((2, tb, w), F32),
                        pltpu.VMEM((2, tb, w), F32),
                        pltpu.VMEM((SCAN_CHUNK, HG_HEAD_DIM), F32),
                        pltpu.VMEM((SCAN_CHUNK, HG_HEAD_DIM), F32),
                        pltpu.VMEM((SCAN_CHUNK, HG_HEAD_DIM), F32),
                        pltpu.VMEM((SCAN_CHUNK, HG_HEAD_DIM), F32)],
        compiler_params=_cparams(("parallel", "arbitrary")),
        name="hgrn2_scan",
    )(z_hg, z_hg, z_hg, z_hg, z_hg, z_hg, lb, s0f, s0b)


def _dft_mats(n_rows, period):
    f = jnp.arange(n_rows, dtype=jnp.int32)
    m = (f[:, None] * f[None, :]) % period
    ang = m.astype(F32) * (2.0 * math.pi / period)
    return jnp.cos(ang).astype(BF16), jnp.sin(ang).astype(BF16)


def _split_bf16(x):
    hi = x.astype(BF16)
    lo = (x - hi.astype(F32)).astype(BF16)
    return hi, lo


def _filter_spectrum_kernel(hs_ref, hd_ref, c_ref, s_ref, kr_ref, ki_ref, kn_ref, *, scale):
    hs = hs_ref[...]
    hd = hd_ref[...]
    hs_hi, hs_lo = _split_bf16(hs)
    hd_hi, hd_lo = _split_bf16(hd)
    c = c_ref[...]
    s = s_ref[...]
    kr_ref[...] = ((_dot(c, hs_hi) + _dot(c, hs_lo)) * scale).astype(kr_ref.dtype)
    ki_ref[...] = ((_dot(s, hd_hi) + _dot(s, hd_lo)) * (-scale)).astype(ki_ref.dtype)
    row = lax.broadcasted_iota(jnp.int32, hs.shape, 0)
    sign = jnp.where(row % 2 == 0, 1.0, -1.0)
    kn = jnp.sum(hs * sign, axis=0, keepdims=True) * scale
    kn_ref[...] = jnp.broadcast_to(kn, kn_ref.shape)


def _filter_spectrum(hs, hd, cmat, smat):
    seq, ch = hs.shape
    scale = 2.0 / (2 * seq)
    return pl.pallas_call(
        functools.partial(_filter_spectrum_kernel, scale=scale),
        out_shape=[jax.ShapeDtypeStruct((seq, ch), BF16), jax.ShapeDtypeStruct((seq, ch), BF16),
                   jax.ShapeDtypeStruct((8, ch), F32)],
        compiler_params=pltpu.CompilerParams(vmem_limit_bytes=V7X_VMEM_LIMIT_BYTES),
        name="hyena_filter_spectrum",
    )(hs, hd, cmat, smat)


def _hyena_kernel(z_ref, cw_ref, cb_ref, c_ref, s_ref, kr_ref, ki_ref, kn_ref, g_ref, o_ref, *, seq, ch):
    row = lax.broadcasted_iota(jnp.int32, (seq, 1), 0)
    sign = jnp.where(row % 2 == 0, 1.0, -1.0)

    def short_conv(part):
        cols = slice(part * ch, (part + 1) * ch)
        z = z_ref[0, :, cols].astype(F32)
        z_prev = jnp.where(row >= 1, pltpu.roll(z, 1, 0), 0.0)
        z_next = jnp.where(row <= seq - 2, pltpu.roll(z, seq - 1, 0), 0.0)
        return z_prev * cw_ref[0:1, cols] + z * cw_ref[1:2, cols] + z_next * cw_ref[2:3, cols] + cb_ref[:, cols]

    def long_conv(x, order):
        cols = slice(order * ch, (order + 1) * ch)
        xb = x.astype(BF16)
        a = _dot(c_ref[...], xb)
        bm = _dot(s_ref[...], xb)
        kr = kr_ref[:, cols]
        ki = ki_ref[:, cols]
        yr = a * kr + bm * ki
        yi = (a * ki - bm * kr).astype(BF16)
        dc = 0.5 * yr[0:1, :]
        x_nyq = jnp.sum(x * sign, axis=0, keepdims=True)
        y = _dot(c_ref[...], yr.astype(BF16)) - _dot(s_ref[...], yi)
        return y - dc + (0.5 * x_nyq * kn_ref[0:1, cols]) * sign

    y = short_conv(1) * long_conv(short_conv(0), 0)
    y = short_conv(2) * long_conv(y, 1)
    ms = jnp.mean(y * y, axis=-1, keepdims=True)
    o_ref[0] = y * lax.rsqrt(ms + RMS_EPS) * g_ref[...]


def _hyena(z_hy, conv_w, conv_b, cmat, smat, kr, ki, kn, norm_g):
    bsz, seq, c3 = z_hy.shape
    ch = c3 // 3
    return pl.pallas_call(
        functools.partial(_hyena_kernel, seq=seq, ch=ch),
        out_shape=jax.ShapeDtypeStruct((bsz, seq, ch), F32),
        grid=(bsz,),
        in_specs=[pl.BlockSpec((1, seq, c3), lambda b: (b, 0, 0)),
                  _const_spec(conv_w.shape), _const_spec((1, c3)),
                  _const_spec(cmat.shape), _const_spec(smat.shape),
                  _const_spec(kr.shape), _const_spec(ki.shape), _const_spec(kn.shape),
                  _const_spec((1, ch))],
        out_specs=pl.BlockSpec((1, seq, ch), lambda b: (b, 0, 0)),
        compiler_params=_cparams(("parallel",)),
        name="hyena_mixer",
    )(z_hy, conv_w, conv_b.reshape(1, c3), cmat, smat, kr, ki, kn, norm_g.reshape(1, ch))


def _fnet_kernel(z_ref, c_ref, s_ref, bdc_ref, bds_ref, bdw_ref, b_ref, g_ref, o_ref, *, scale):
    zb = z_ref[0].astype(BF16)
    y1 = _dot(c_ref[...], zb)
    y2 = _dot(s_ref[...], zb)
    r = (_dot(y1.astype(BF16), bdc_ref[...]) - _dot(y2.astype(BF16), bds_ref[...])) * scale
    y = _dot(r.astype(BF16), bdw_ref[...]) + b_ref[...]
    ms = jnp.mean(y * y, axis=-1, keepdims=True)
    o_ref[0] = y * lax.rsqrt(ms + RMS_EPS) * g_ref[...]


def _block_diag(blocks):
    g, a, b = blocks.shape
    out = jnp.zeros((g * a, g * b), blocks.dtype)
    for i in range(g):
        out = out.at[i * a:(i + 1) * a, i * b:(i + 1) * b].set(blocks[i])
    return out


def _fnet(z_fn, cmat, smat, fn_w, fn_b, norm_g):
    bsz, seq, ch = z_fn.shape
    gd = ch // FN_GROUPS
    k = np.arange(gd)
    ang = 2.0 * np.pi * ((k[:, None] * k[None, :]) % gd) / gd
    eye = np.eye(FN_GROUPS)
    bdc = jnp.asarray(np.kron(eye, np.cos(ang)), BF16)
    bds = jnp.asarray(np.kron(eye, np.sin(ang)), BF16)
    bdw = _block_diag(fn_w).astype(BF16)
    scale = 1.0 / math.sqrt(seq * gd)
    return pl.pallas_call(
        functools.partial(_fnet_kernel, scale=scale),
        out_shape=jax.ShapeDtypeStruct((bsz, seq, ch), F32),
        grid=(bsz,),
        in_specs=[pl.BlockSpec((1, seq, ch), lambda b: (b, 0, 0)),
                  _const_spec(cmat.shape), _const_spec(smat.shape),
                  _const_spec((ch, ch)), _const_spec((ch, ch)), _const_spec((ch, ch)),
                  _const_spec((1, ch)), _const_spec((1, ch))],
        out_specs=pl.BlockSpec((1, seq, ch), lambda b: (b, 0, 0)),
        compiler_params=_cparams(("parallel",)),
        name="fnet_mixer",
    )(z_fn, cmat, smat, bdc, bds, bdw, fn_b.reshape(1, ch), norm_g.reshape(1, ch))


def _route(logits_t, rb_ref):
    rows = [logits_t[e:e + 1, :] for e in range(N_EXPERTS)]
    mx = functools.reduce(jnp.maximum, rows)
    ex = [jnp.exp(r - mx) for r in rows]
    inv = 1.0 / functools.reduce(lambda a, b: a + b, ex)
    probs = [e * inv for e in ex]
    sel = [probs[e] + rb_ref[e:e + 1, 0:1] for e in range(N_EXPERTS)]
    epg = EXPERTS_PER_GROUP
    gscore = []
    for g in range(N_GROUPS):
        s = sel[g * epg:(g + 1) * epg]
        pairs = [s[i] + s[j] for i in range(epg) for j in range(i + 1, epg)]
        gscore.append(functools.reduce(jnp.maximum, pairs))
    best = gscore[0]
    best_g = jnp.zeros_like(best, dtype=jnp.int32)
    for g in range(1, N_GROUPS):
        better = gscore[g] > best
        best_g = jnp.where(better, g, best_g)
        best = jnp.where(better, gscore[g], best)
    chosen = []
    for e in range(N_EXPERTS):
        g = e // epg
        beaten = jnp.zeros_like(best_g)
        for e2 in range(g * epg, (g + 1) * epg):
            if e2 == e:
                continue
            wins = (sel[e2] > sel[e]) | ((sel[e2] == sel[e]) & (e2 < e))
            beaten = beaten + wins.astype(jnp.int32)
        chosen.append(jnp.where((best_g == g) & (beaten < 2), probs[e], 0.0))
    tot = functools.reduce(lambda a, b: a + b, chosen)
    inv_tot = 1.0 / tot
    return jnp.concatenate([c * inv_tot for c in chosen], axis=0)


def _outproj_kernel(*refs, add_pos, n_heads):
    if add_pos:
        x_ref, pos_ref = refs[:2]
        rest = refs[2:]
        x = x_ref[0] + pos_ref[...]
    else:
        x_ref = refs[0]
        rest = refs[1:]
        x = x_ref[0]
    (yhy_ref, of_ref, ob_ref, g_ref, yfn_ref, wout_ref, hgg_ref, gate_ref, n2g_ref, sh_ref, sc_ref,
     rwt_ref, rb_ref, xo_ref, h_ref, comb_ref) = rest
    o = of_ref[0] + ob_ref[0]
    parts = []
    for h in range(n_heads):
        oh = o[:, h * HG_HEAD_DIM:(h + 1) * HG_HEAD_DIM]
        parts.append(oh * lax.rsqrt(jnp.mean(oh * oh, axis=-1, keepdims=True) + RMS_EPS))
    y_hg = jnp.concatenate(parts, axis=-1) * hgg_ref[...] * _silu(g_ref[0])
    c_hy = yhy_ref.shape[-1]
    c_hg = y_hg.shape[-1]
    mix = (_dot(yhy_ref[0].astype(BF16), wout_ref[0:c_hy, :]) +
           _dot(y_hg.astype(BF16), wout_ref[c_hy:c_hy + c_hg, :]) +
           _dot(yfn_ref[0].astype(BF16), wout_ref[c_hy + c_hg:, :]))
    xn = x + gate_ref[0] * mix
    xo_ref[0] = xn
    ms = jnp.mean(xn * xn, axis=-1, keepdims=True)
    h2 = xn * lax.rsqrt(ms + RMS_EPS) * n2g_ref[...]
    h2 = h2 * (1.0 + sc_ref[0]) + sh_ref[0]
    h_ref[0] = h2.astype(BF16)
    logits_t = _dot_nt(rwt_ref[...], h2, precision=HIGHEST)
    comb_ref[0] = _route(logits_t, rb_ref)


def _out_proj(x, pos, y_hy, o_f, o_b, z_hg, y_fn, w_out, hg_norm_g, mod, mod_row, norm2_g, rwt, rb, tm):
    bsz, seq, d = x.shape
    add_pos = pos is not None
    c_hy, c_hg, c_fn = y_hy.shape[-1], o_f.shape[-1], y_fn.shape[-1]
    tok = lambda c, j=0: pl.BlockSpec((1, tm, c), lambda b, i: (b, i, j))
    modk = lambda k: pl.BlockSpec((1, 1, d), lambda b, i: (mod_row(b), 0, k))
    in_specs = [tok(d)]
    args = [x]
    if add_pos:
        in_specs.append(pl.BlockSpec((tm, d), lambda b, i: (i, 0)))
        args.append(pos)
    in_specs += [tok(c_hy), tok(c_hg), tok(c_hg), tok(c_hg, 4), tok(c_fn),
                 _const_spec(w_out.shape), _const_spec((1, c_hg)), modk(2), _const_spec((1, d)),
                 modk(3), modk(4), _const_spec(rwt.shape), _const_spec(rb.shape)]
    args += [y_hy, o_f, o_b, z_hg, y_fn, w_out, hg_norm_g.reshape(1, c_hg), mod, norm2_g.reshape(1, d),
             mod, mod, rwt, rb]
    nt = seq // tm
    out_shape = [jax.ShapeDtypeStruct((bsz, seq, d), F32), jax.ShapeDtypeStruct((bsz, seq, d), BF16),
                 jax.ShapeDtypeStruct((bsz * nt, N_EXPERTS, tm), F32)]
    out_specs = [tok(d), tok(d), pl.BlockSpec((1, N_EXPERTS, tm), lambda b, i: (b * nt + i, 0, 0))]
    return pl.pallas_call(
        functools.partial(_outproj_kernel, add_pos=add_pos, n_heads=c_hg // HG_HEAD_DIM),
        out_shape=out_shape, grid=(bsz, nt), in_specs=in_specs, out_specs=out_specs,
        compiler_params=_cparams(("parallel", "parallel")),
        name="out_proj_router",
    )(*args)


def _moe_kernel(*refs, final, cap):
    if final:
        (h_ref, comb_ref, combt_ref, wg_ref, wu_ref, wd_ref, x_ref, gate_ref, fg_ref, o_ref,
         acc_ref, rcol_ref, rrow_ref) = refs
    else:
        (h_ref, comb_ref, combt_ref, wg_ref, wu_ref, wd_ref, x_ref, gate_ref, o_ref,
         acc_ref, rcol_ref, rrow_ref) = refs
    e = pl.program_id(1)
    tm = h_ref.shape[0]
    n_rt = combt_ref.shape[0]

    @pl.when(e == 0)
    def _():
        acc_ref[...] = jnp.zeros_like(acc_ref)
        ti = lax.broadcasted_iota(jnp.int32, (tm, tm), 0)
        tj = lax.broadcasted_iota(jnp.int32, (tm, tm), 1)
        before_col = jnp.where(tj < ti, 1.0, 0.0).astype(BF16)
        before_row = jnp.where(ti < tj, 1.0, 0.0).astype(BF16)
        sel_col = jnp.where(comb_ref[...] > 0.0, 1.0, 0.0).astype(BF16)
        combt = jnp.concatenate([combt_ref[j] for j in range(n_rt)], axis=1)
        sel_row = jnp.where(combt > 0.0, 1.0, 0.0).astype(BF16)
        rcol_ref[...] = _dot(before_col, sel_col)
        rrow_ref[...] = _dot(sel_row, before_row)

    h = h_ref[...]
    comb = comb_ref[...]
    lane = lax.broadcasted_iota(jnp.int32, comb.shape, 1)
    ce = jnp.sum(jnp.where(lane == e, comb, 0.0), axis=-1, keepdims=True)
    ce_row = jnp.concatenate([combt_ref[j, pl.ds(e, 1), :] for j in range(n_rt)], axis=1)
    count = jnp.sum(jnp.where(ce_row > 0.0, 1.0, 0.0))
    fits = count <= cap

    @pl.when(fits)
    def _():
        rank_col = jnp.sum(jnp.where(lane == e, rcol_ref[...], 0.0), axis=-1, keepdims=True)
        rank_row = rrow_ref[pl.ds(e, 1), :]
        slot_r = lax.broadcasted_iota(jnp.int32, (cap, tm), 0).astype(F32)
        pick = jnp.where((slot_r == rank_row) & (ce_row > 0.0), 1.0, 0.0).astype(BF16)
        hc = _dot(pick, h).astype(BF16)
        a = _silu(_dot(hc, wg_ref[0])) * _dot(hc, wu_ref[0])
        y = _dot(a.astype(BF16), wd_ref[0]).astype(BF16)
        slot_c = lax.broadcasted_iota(jnp.int32, (tm, cap), 1).astype(F32)
        spread = jnp.where((slot_c == rank_col) & (ce > 0.0), 1.0, 0.0).astype(BF16)
        acc_ref[...] += ce * _dot(spread, y)

    @pl.when(jnp.logical_not(fits))
    def _():
        a = _silu(_dot(h, wg_ref[0])) * _dot(h, wu_ref[0])
        acc_ref[...] += ce * _dot(a.astype(BF16), wd_ref[0])

    @pl.when(e == pl.num_programs(1) - 1)
    def _():
        y = x_ref[...] + gate_ref[0] * acc_ref[...]
        if final:
            ms = jnp.mean(y * y, axis=-1, keepdims=True)
            y = y * lax.rsqrt(ms + RMS_EPS) * fg_ref[...]
        o_ref[...] = y


def _moe(h, comb_t, wg, wu, wd, x, mod, mod_row_of_tile, final_g, tm):
    n, d = h.shape
    ne, _, de = wg.shape
    final = final_g is not None
    tr = comb_t.shape[-1]
    n_rt = tm // tr
    lanes = 128
    comb = jnp.transpose(comb_t, (0, 2, 1)).reshape(n, ne)
    comb = jnp.pad(comb, ((0, 0), (0, lanes - ne)))
    cap = min(MOE_CAP, tm)
    in_specs = [pl.BlockSpec((tm, d), lambda i, e: (i, 0)),
                pl.BlockSpec((tm, lanes), lambda i, e: (i, 0)),
                pl.BlockSpec((n_rt, ne, tr), lambda i, e: (i, 0, 0)),
                pl.BlockSpec((1, d, de), lambda i, e: (e, 0, 0)),
                pl.BlockSpec((1, d, de), lambda i, e: (e, 0, 0)),
                pl.BlockSpec((1, de, d), lambda i, e: (e, 0, 0)),
                pl.BlockSpec((tm, d), lambda i, e: (i, 0)),
                pl.BlockSpec((1, 1, d), lambda i, e: (mod_row_of_tile(i), 0, 5))]
    args = [h, comb, comb_t, wg, wu, wd, x, mod]
    if final:
        in_specs.append(pl.BlockSpec((1, d), lambda i, e: (0, 0)))
        args.append(final_g.reshape(1, d))
    return pl.pallas_call(
        functools.partial(_moe_kernel, final=final, cap=cap),
        out_shape=jax.ShapeDtypeStruct((n, d), F32),
        grid=(n // tm, ne), in_specs=in_specs,
        out_specs=pl.BlockSpec((tm, d), lambda i, e: (i, 0)),
        scratch_shapes=[pltpu.VMEM((tm, d), F32), pltpu.VMEM((tm, lanes), F32), pltpu.VMEM((ne, tm), F32)],
        compiler_params=_cparams(("parallel", "arbitrary")),
        name="moe_experts",
    )(*args)


def _grid_sincos(n_tok, d):
    rows = n_tok // GRID_W
    row = jnp.repeat(jnp.arange(rows, dtype=F32), GRID_W)
    col = jnp.tile(jnp.arange(GRID_W, dtype=F32), rows)
    quarter = d // 4
    omega = 1.0 / (POS_BASE ** (jnp.arange(quarter, dtype=F32) / quarter))
    ar = row[:, None] * omega
    ac = col[:, None] * omega
    return jnp.concatenate([jnp.sin(ar), jnp.cos(ar), jnp.sin(ac), jnp.cos(ac)], axis=-1)


def _hyena_filters(seq, ch, w1, b1, w2, b2, w3, freq):
    t = jnp.linspace(0.0, 1.0, seq, dtype=F32)[:, None]
    w = 2.0 * math.pi * jnp.arange(seq, dtype=F32)[:, None] / seq
    bands = jnp.linspace(1e-4, HY_BANDS - 1, HY_BANDS, dtype=F32)[None, :]
    z = jnp.concatenate([t, jnp.cos(bands * w), -jnp.sin(bands * w)], axis=-1)
    hp = functools.partial(jnp.dot, precision=HIGHEST)
    h = jnp.sin(freq * (hp(z, w1) + b1))
    h = jnp.sin(freq * (hp(h, w2) + b2))
    h = hp(h, w3).reshape(seq, 2 * HY_ORDER, ch)
    max_decay = math.log(HY_DECAY_TARGET) / HY_FAST_DECAY_PCT
    min_decay = math.log(HY_DECAY_TARGET) / HY_SLOW_DECAY_PCT
    deltas = jnp.linspace(min_decay, max_decay, ch, dtype=F32)
    return h * jnp.exp(-t[:, :, None] * jnp.abs(deltas))


def _filter_halves(h, bias):
    seq = h.shape[0]
    not0 = (jnp.arange(seq) > 0)[:, None].astype(F32)
    hs, hd = [], []
    for o in range(HY_ORDER):
        hf = h[:, 2 * o].at[0].add(bias[o])
        hb = h[:, 2 * o + 1] * not0
        hs.append(hf + hb)
        hd.append(hf - hb)
    return jnp.concatenate(hs, axis=-1), jnp.concatenate(hd, axis=-1)


def kernel(x, c, ctx, c_ctx, ada_w, ada_b, norm1_g, norm2_g, w_in, w_out, hy_conv_w, hy_conv_b, hy_filt_w1, hy_filt_b1, hy_filt_w2, hy_filt_b2, hy_filt_w3, hy_filt_freq, hy_bias, hy_norm_g, hg_lower_bounds, hg_norm_g, fn_w, fn_b, fn_norm_g, router_w, router_b, moe_w_gate, moe_w_up, moe_w_down, final_norm_g):
    bsz, seq, d = x.shape
    n_ctx = ctx.shape[1]
    depth = ada_w.shape[0]
    c_hy = hy_norm_g.shape[-1]
    c_hyp = hy_conv_w.shape[-1]
    w_hg = hg_norm_g.shape[-1]
    c_fn = fn_norm_g.shape[-1]
    hg_lo, hg_hi = c_hyp, c_hyp + 5 * w_hg
    nh = w_hg // HG_HEAD_DIM

    cs = jnp.cumsum(jax.nn.softmax(hg_lower_bounds.astype(F32), axis=0), axis=0)
    lower_bounds = cs - cs[0:1]
    pos = _grid_sincos(seq, d)

    n_rows = -(-(bsz + 1) // 8) * 8
    cond = jnp.zeros((n_rows, d), F32).at[:bsz].set(c).at[bsz].set(c_ctx)
    mods = _ada_modulation(cond, ada_w, ada_b)
    x_row = lambda b: b
    c_row = lambda b: bsz

    dft = {}
    for n in (seq, n_ctx):
        dft[("hy", n)] = _dft_mats(n, 2 * n)
        dft[("fn", n)] = _dft_mats(n, n)

    rwt = jnp.transpose(router_w).astype(F32)
    rb = jnp.broadcast_to(router_b.astype(F32)[:, None], (N_EXPERTS, 128))
    s_zero = jnp.zeros((bsz, nh, HG_HEAD_DIM, HG_HEAD_DIM), F32)

    tm_x = min(512, seq)
    tm_c = min(256, n_ctx)
    tb_x = min(256, seq)
    tb_c = min(256, n_ctx)
    segs = [(hg_lo, hg_hi, F32), (0, c_hyp, BF16), (hg_hi, hg_hi + c_fn, BF16)]

    def mixer(z_hy, z_fn, n, layer):
        cmat, smat = dft[("hy", n)]
        h = _hyena_filters(n, c_hy, hy_filt_w1[layer], hy_filt_b1[layer], hy_filt_w2[layer],
                           hy_filt_b2[layer], hy_filt_w3[layer], hy_filt_freq[layer])
        hs, hd = _filter_halves(h, hy_bias[layer])
        kr, ki, kn = _filter_spectrum(hs, hd, cmat, smat)
        y_hy = _hyena(z_hy, hy_conv_w[layer], hy_conv_b[layer], cmat, smat, kr, ki, kn, hy_norm_g[layer])
        fc, fs = dft[("fn", n)]
        y_fn = _fnet(z_fn, fc, fs, fn_w[layer], fn_b[layer], fn_norm_g[layer])
        return y_hy, y_fn

    for layer in range(depth):
        last = layer == depth - 1
        mod = mods[layer].reshape(n_rows, 1, N_MOD * d)
        w_in_b = w_in[layer].astype(BF16)
        w_out_b = w_out[layer].astype(BF16)
        wg = moe_w_gate[layer].astype(BF16)
        wu = moe_w_up[layer].astype(BF16)
        wd = moe_w_down[layer].astype(BF16)
        lb = lower_bounds[layer]
        x_pos = pos if layer == 0 else None

        if last:
            (zc_hg,) = _in_proj(ctx, None, norm1_g[layer], mod, c_row, w_in_b[:, hg_lo:hg_hi],
                                [(0, 5 * w_hg, F32)], tm_c)
        else:
            zc_hg, zc_hy, zc_fn = _in_proj(ctx, None, norm1_g[layer], mod, c_row, w_in_b, segs, tm_c)
        oc_f, oc_b, s_f, s_b = _hgrn2_scan(zc_hg, lb, s_zero, s_zero, tb_c)

        zx_hg, zx_hy, zx_fn = _in_proj(x, x_pos, norm1_g[layer], mod, x_row, w_in_b, segs, tm_x)
        ox_f, ox_b, _, _ = _hgrn2_scan(zx_hg, lb, s_f, s_b, tb_x)
        yx_hy, yx_fn = mixer(zx_hy, zx_fn, seq, layer)
        x_mid, hx2, comb_x = _out_proj(x, x_pos, yx_hy, ox_f, ox_b, zx_hg, yx_fn, w_out_b, hg_norm_g[layer],
                                       mod, x_row, norm2_g[layer], rwt, rb, tm_x)
        tm_moe = min(1024, seq)
        tiles_per_seq = seq // tm_moe
        x = _moe(hx2.reshape(bsz * seq, d), comb_x, wg, wu, wd, x_mid.reshape(bsz * seq, d), mod,
                 lambda i: i // tiles_per_seq, final_norm_g if last else None, tm_moe).reshape(bsz, seq, d)

        if not last:
            yc_hy, yc_fn = mixer(zc_hy, zc_fn, n_ctx, layer)
            c_mid, hc2, comb_c = _out_proj(ctx, None, yc_hy, oc_f, oc_b, zc_hg, yc_fn, w_out_b,
                                           hg_norm_g[layer], mod, c_row, norm2_g[layer], rwt, rb, tm_c)
            tm_mc = min(1024, bsz * n_ctx)
            ctx = _moe(hc2.reshape(bsz * n_ctx, d), comb_c, wg, wu, wd, c_mid.reshape(bsz * n_ctx, d), mod,
                       lambda i: bsz, None, tm_mc).reshape(bsz, n_ctx, d)
    return x
```

```python
import functools
import math

import numpy as np
import jax
import jax.numpy as jnp
from jax import lax
from jax.experimental import pallas as pl
from jax.experimental.pallas import tpu as pltpu

F32 = jnp.float32
BF16 = jnp.bfloat16
HIGHEST = lax.Precision.HIGHEST

GRID_W = 64
HY_ORDER = 2
HY_BANDS = 16
HY_DECAY_TARGET = 1e-2
HY_FAST_DECAY_PCT = 0.3
HY_SLOW_DECAY_PCT = 1.5
HG_HEAD_DIM = 128
FN_GROUPS = 4
N_EXPERTS = 16
N_GROUPS = 4
EXPERTS_PER_GROUP = N_EXPERTS // N_GROUPS
N_MOD = 6
RMS_EPS = 1e-6
POS_BASE = 10000.0

V7X_VMEM_LIMIT_BYTES = 56 * 1024 * 1024
SCAN_CHUNK = 64
SCAN_SUB = 16
SCAN_SAFE_LOG2_RANGE = 80.0
LOG2_E = 1.0 / math.log(2.0)
MOE_CAP = 192
NEG_BIG = -1e30


def _cparams(sem):
    return pltpu.CompilerParams(dimension_semantics=sem, vmem_limit_bytes=V7X_VMEM_LIMIT_BYTES)


def _const_spec(shape):
    nd = len(shape)
    return pl.BlockSpec(shape, lambda *_: (0,) * nd, pipeline_mode=pl.Buffered(1))


def _silu(x):
    return x * jax.nn.sigmoid(x)


def _dot(a, b):
    return jnp.dot(a, b, preferred_element_type=F32)


def _dot_nt(a, b, precision=None):
    return lax.dot_general(a, b, (((1,), (1,)), ((), ())), precision=precision,
                           preferred_element_type=F32)


def _dot_tn(a, b):
    return lax.dot_general(a, b, (((0,), (0,)), ((), ())), preferred_element_type=F32)


def _ada_kernel(c_ref, w_ref, b_ref, o_ref):
    s = _silu(c_ref[...])
    o_ref[0] = jnp.dot(s, w_ref[0], precision=HIGHEST, preferred_element_type=F32) + b_ref[0]


def _ada_modulation(cond, ada_w, ada_b):
    depth, d, nd = ada_w.shape
    rows = cond.shape[0]
    tn = 1536
    return pl.pallas_call(
        _ada_kernel,
        out_shape=jax.ShapeDtypeStruct((depth, rows, nd), F32),
        grid=(depth, nd // tn),
        in_specs=[pl.BlockSpec((rows, d), lambda l, j: (0, 0)),
                  pl.BlockSpec((1, d, tn), lambda l, j: (l, 0, j)),
                  pl.BlockSpec((1, 1, tn), lambda l, j: (l, 0, j))],
        out_specs=pl.BlockSpec((1, rows, tn), lambda l, j: (l, 0, j)),
        compiler_params=_cparams(("parallel", "parallel")),
        name="ada_modulation",
    )(cond, ada_w, ada_b.reshape(depth, 1, nd))


def _inproj_kernel(*refs, add_pos, segs):
    if add_pos:
        x_ref, pos_ref, g_ref, sh_ref, sc_ref, w_ref = refs[:6]
        outs = refs[6:]
        x = x_ref[0] + pos_ref[...]
    else:
        x_ref, g_ref, sh_ref, sc_ref, w_ref = refs[:5]
        outs = refs[5:]
        x = x_ref[0]
    ms = jnp.mean(x * x, axis=-1, keepdims=True)
    h = x * lax.rsqrt(ms + RMS_EPS) * g_ref[...]
    h = (h * (1.0 + sc_ref[0]) + sh_ref[0]).astype(BF16)
    for o_ref, (a, b, _) in zip(outs, segs):
        o_ref[0] = _dot(h, w_ref[:, a:b]).astype(o_ref.dtype)


def _in_proj(x, pos, norm_g, mod, mod_row, w, segs, tm):
    bsz, seq, d = x.shape
    add_pos = pos is not None
    in_specs = [pl.BlockSpec((1, tm, d), lambda b, i: (b, i, 0))]
    args = [x]
    if add_pos:
        in_specs.append(pl.BlockSpec((tm, d), lambda b, i: (i, 0)))
        args.append(pos)
    in_specs += [pl.BlockSpec((1, d), lambda b, i: (0, 0)),
                 pl.BlockSpec((1, 1, d), lambda b, i: (mod_row(b), 0, 0)),
                 pl.BlockSpec((1, 1, d), lambda b, i: (mod_row(b), 0, 1)),
                 _const_spec(w.shape)]
    args += [norm_g.reshape(1, d), mod, mod, w]
    out_shape = [jax.ShapeDtypeStruct((bsz, seq, b - a), dt) for a, b, dt in segs]
    out_specs = [pl.BlockSpec((1, tm, b - a), lambda bb, i: (bb, i, 0)) for a, b, _ in segs]
    return pl.pallas_call(
        functools.partial(_inproj_kernel, add_pos=add_pos, segs=tuple(segs)),
        out_shape=out_shape, grid=(bsz, seq // tm), in_specs=in_specs, out_specs=out_specs,
        compiler_params=_cparams(("parallel", "parallel")),
        name="in_proj",
    )(*args)


def _cumsum_rows(x, reverse):
    c = x.shape[0]
    row = lax.broadcasted_iota(jnp.int32, x.shape, 0)
    sh = 1
    while sh < c:
        if reverse:
            x = x + jnp.where(row < c - sh, pltpu.roll(x, c - sh, 0), 0.0)
        else:
            x = x + jnp.where(row >= sh, pltpu.roll(x, sh, 0), 0.0)
        sh *= 2
    return x


def _seg_cumsum_rows(x, seg, reverse):
    n = x.shape[0]
    pos = lax.broadcasted_iota(jnp.int32, x.shape, 0) % seg
    sh = 1
    while sh < seg:
        if reverse:
            x = x + jnp.where(pos < seg - sh, pltpu.roll(x, n - sh, 0), 0.0)
        else:
            x = x + jnp.where(pos >= sh, pltpu.roll(x, sh, 0), 0.0)
        sh *= 2
    return x


def _scan_chunk_fast(q, b, k, v, st_ref, h, reverse):
    c = SCAN_CHUNK
    s = SCAN_SUB
    nsub = c // s
    qs = _silu(q) * (HG_HEAD_DIM ** -0.5)
    st = st_ref[h]
    o_inter = _dot_nt((qs * jnp.exp2(b)).astype(BF16), st.astype(BF16))
    v_bf = v.astype(BF16)
    zero_row = jnp.zeros((1, HG_HEAD_DIM), F32)
    zero_blk = jnp.zeros((s, HG_HEAD_DIM), BF16)
    kt = {}
    prev_ref = None
    sc = [None] * nsub
    for i in (range(nsub - 1, -1, -1) if reverse else range(nsub)):
        r0 = i * s
        bi = b[r0:r0 + s]
        if reverse:
            ref = b[r0 + s:r0 + s + 1] if i < nsub - 1 else zero_row
        else:
            ref = b[r0 - 1:r0] if i > 0 else zero_row
        if prev_ref is not None:
            step = jnp.exp2(ref - prev_ref)
            kt = {j: blk * step for j, blk in kt.items()}
        kt[i] = k[r0:r0 + s] * jnp.exp2(ref - bi)
        prev_ref = ref
        qi = (qs[r0:r0 + s] * jnp.exp2(bi - ref)).astype(BF16)
        keys = jnp.concatenate([kt[j].astype(BF16) if j in kt else zero_blk for j in range(nsub)], axis=0)
        sc[i] = _dot_nt(qi, keys)
    scores = jnp.concatenate(sc, axis=0)
    rr = lax.broadcasted_iota(jnp.int32, (c, c), 0)
    cc = lax.broadcasted_iota(jnp.int32, (c, c), 1)
    scores = jnp.where((cc >= rr) if reverse else (cc <= rr), scores, 0.0)
    o = o_inter + _dot(scores.astype(BF16), v_bf)

    b_end = b[0:1] if reverse else b[c - 1:c]
    kd = (k * jnp.exp2(b_end - b)).astype(BF16)
    st_ref[h] = st * jnp.exp2(b_end) + _dot_tn(v_bf, kd)
    return o


def _scan_chunk_exact(q, lf2, k, v, st_ref, h, b_scr, q_scr, reverse):
    c = SCAN_CHUNK
    s = SCAN_SUB
    nsub = c // s
    b = _cumsum_rows(lf2, reverse)
    qs = _silu(q) * (HG_HEAD_DIM ** -0.5)
    st = st_ref[h]
    o_inter = _dot_nt((qs * jnp.exp2(b)).astype(BF16), st.astype(BF16))

    b_scr[...] = b
    q_scr[...] = qs
    v_bf = v.astype(BF16)
    ones = jnp.ones((HG_HEAD_DIM, HG_HEAD_DIM), BF16)
    sub_iota = lax.broadcasted_iota(jnp.int32, (s, HG_HEAD_DIM), 0)
    row_iota = lax.broadcasted_iota(jnp.int32, (c, HG_HEAD_DIM), 0)
    grp = (lax.broadcasted_iota(jnp.int32, (s, s * s), 1) // s ==
           lax.broadcasted_iota(jnp.int32, (s, s * s), 0)).astype(BF16)
    o_parts = []
    for i in range(nsub):
        r0 = i * s
        bi = b[r0:r0 + s]
        ki = k[r0:r0 + s]
        vi = v[r0:r0 + s]
        prods = []
        for t in range(s):
            bt = b_scr[pl.ds(r0 + t, 1), :]
            qt = q_scr[pl.ds(r0 + t, 1), :]
            keep = (sub_iota >= t) if reverse else (sub_iota <= t)
            e = jnp.exp2(jnp.where(keep, bt - bi, NEG_BIG))
            prods.append(((qt * ki) * e).astype(BF16))
        p = jnp.concatenate(prods, axis=0)
        rsum = _dot(p, ones)
        zt = (rsum * jnp.concatenate([vi] * s, axis=0)).astype(BF16)
        o_i = _dot(grp, zt)
        if reverse and i < nsub - 1:
            ref_row = b[r0 + s:r0 + s + 1]
            key_rows = row_iota >= r0 + s
        elif (not reverse) and i > 0:
            ref_row = b[r0 - 1:r0]
            key_rows = row_iota < r0
        else:
            ref_row = None
        if ref_row is not None:
            qi = qs[r0:r0 + s] * jnp.exp2(bi - ref_row)
            ks = k * jnp.exp2(jnp.where(key_rows, ref_row - b, NEG_BIG))
            sc = _dot_nt(qi.astype(BF16), ks.astype(BF16))
            o_i = o_i + _dot(sc.astype(BF16), v_bf)
        o_parts.append(o_i)
    o = o_inter + jnp.concatenate(o_parts, axis=0)

    b_end = b[0:1] if reverse else b[c - 1:c]
    kd = (k * jnp.exp2(b_end - b)).astype(BF16)
    st_ref[h] = st * jnp.exp2(b_end) + _dot_tn(v_bf, kd)
    return o


def _scan_kernel(qf_ref, qb_ref, ff_ref, fb_ref, vf_ref, vb_ref, lb_ref, s0f_ref, s0b_ref,
                 of_ref, ob_ref, sf_ref, sb_ref, stf, stb, lf_scr, k_scr, cum_scr, bf_scr, qf_scr, bb_scr, qb_scr,
                 *, n_heads, n_chunks):
    n = pl.program_id(1)

    @pl.when(n == 0)
    def _():
        stf[...] = s0f_ref[0]
        stb[...] = s0b_ref[0]

    tb = n_chunks * SCAN_CHUNK
    nblk = tb // SCAN_SUB
    sel = (lax.broadcasted_iota(jnp.int32, (nblk, tb), 1) // SCAN_SUB ==
           lax.broadcasted_iota(jnp.int32, (nblk, tb), 0)).astype(BF16)
    worst = jnp.zeros((nblk, lb_ref.shape[1]), F32)
    for d, f_ref in enumerate((ff_ref, fb_ref)):
        lb = lb_ref[d:d + 1, :]
        forget = lb + (1.0 - lb) * jax.nn.sigmoid(f_ref[0])
        lf2 = jnp.log(forget) * LOG2_E
        lf_scr[d] = lf2
        k_scr[d] = 1.0 - forget
        cum_scr[d] = _seg_cumsum_rows(lf2, SCAN_CHUNK, d == 1)
        worst = jnp.maximum(worst, -_dot(sel, lf2.astype(BF16)))
    safe = jnp.max(worst) <= SCAN_SAFE_LOG2_RANGE

    def rows_of(ci):
        rf = pl.ds(pl.multiple_of(ci * SCAN_CHUNK, SCAN_CHUNK), SCAN_CHUNK)
        rb = pl.ds(pl.multiple_of((n_chunks - 1 - ci) * SCAN_CHUNK, SCAN_CHUNK), SCAN_CHUNK)
        return rf, rb

    @pl.when(safe)
    def _():
        def chunk_body(ci, carry):
            rf, rb = rows_of(ci)
            for h in range(n_heads):
                cols = slice(h * HG_HEAD_DIM, (h + 1) * HG_HEAD_DIM)
                of_ref[0, rf, cols] = _scan_chunk_fast(qf_ref[0, rf, cols], cum_scr[0, rf, cols],
                                                       k_scr[0, rf, cols], vf_ref[0, rf, cols], stf, h, False)
                ob_ref[0, rb, cols] = _scan_chunk_fast(qb_ref[0, rb, cols], cum_scr[1, rb, cols],
                                                       k_scr[1, rb, cols], vb_ref[0, rb, cols], stb, h, True)
            return carry

        lax.fori_loop(0, n_chunks, chunk_body, 0)

    @pl.when(jnp.logical_not(safe))
    def _():
        def head_body(h, carry):
            cols = pl.ds(pl.multiple_of(h * HG_HEAD_DIM, HG_HEAD_DIM), HG_HEAD_DIM)

            def chunk_body(ci, carry2):
                rf, rb = rows_of(ci)
                of_ref[0, rf, cols] = _scan_chunk_exact(qf_ref[0, rf, cols], lf_scr[0, rf, cols],
                                                        k_scr[0, rf, cols], vf_ref[0, rf, cols], stf, h,
                                                        bf_scr, qf_scr, False)
                ob_ref[0, rb, cols] = _scan_chunk_exact(qb_ref[0, rb, cols], lf_scr[1, rb, cols],
                                                        k_scr[1, rb, cols], vb_ref[0, rb, cols], stb, h,
                                                        bb_scr, qb_scr, True)
                return carry2

            return lax.fori_loop(0, n_chunks, chunk_body, carry)

        lax.fori_loop(0, n_heads, head_body, 0)

    @pl.when(n == pl.num_programs(1) - 1)
    def _():
        sf_ref[0] = stf[...]
        sb_ref[0] = stb[...]


def _hgrn2_scan(z_hg, lb, s0f, s0b, tb):
    bsz, seq, w5 = z_hg.shape
    w = w5 // 5
    nh = w // HG_HEAD_DIM
    nb = seq // tb
    blk = (1, tb, w)
    fwd = lambda j: pl.BlockSpec(blk, lambda b, n: (b, n, j))
    bwd = lambda j: pl.BlockSpec(blk, lambda b, n: (b, nb - 1 - n, j))
    st_spec = pl.BlockSpec((1, nh, HG_HEAD_DIM, HG_HEAD_DIM), lambda b, n: (b, 0, 0, 0))
    st_shape = jax.ShapeDtypeStruct((bsz, nh, HG_HEAD_DIM, HG_HEAD_DIM), F32)
    o_shape = jax.ShapeDtypeStruct((bsz, seq, w), F32)
    return pl.pallas_call(
        functools.partial(_scan_kernel, n_heads=nh, n_chunks=tb // SCAN_CHUNK),
        out_shape=[o_shape, o_shape, st_shape, st_shape],
        grid=(bsz, nb),
        in_specs=[fwd(0), bwd(0), fwd(1), bwd(2), fwd(3), bwd(3),
                  pl.BlockSpec((2, w), lambda b, n: (0, 0)), st_spec, st_spec],
        out_specs=[fwd(0), bwd(0), st_spec, st_spec],
        scratch_shapes=[pltpu.VMEM((nh, HG_HEAD_DIM, HG_HEAD_DIM), F32),
                        pltpu.VMEM((nh, HG_HEAD_DIM, HG_HEAD_DIM), F32),
                        pltpu.VMEM((2, tb, w), F32),
                        pltpu.VMEM((2, tb, w), F32),
                        pltpu.VMEM((2, tb, w), F32),
                        pltpu.VMEM((SCAN_CHUNK, HG_HEAD_DIM), F32),
                        pltpu.VMEM((SCAN_CHUNK, HG_HEAD_DIM), F32),
                        pltpu.VMEM((SCAN_CHUNK, HG_HEAD_DIM), F32),
                        pltpu.VMEM((SCAN_CHUNK, HG_HEAD_DIM), F32)],
        compiler_params=_cparams(("parallel", "arbitrary")),
        name="hgrn2_scan",
    )(z_hg, z_hg, z_hg, z_hg, z_hg, z_hg, lb, s0f, s0b)


def _dft_mats(n_rows, period):
    f = jnp.arange(n_rows, dtype=jnp.int32)
    m = (f[:, None] * f[None, :]) % period
    ang = m.astype(F32) * (2.0 * math.pi / period)
    return jnp.cos(ang).astype(BF16), jnp.sin(ang).astype(BF16)


def _split_bf16(x):
    hi = x.astype(BF16)
    lo = (x - hi.astype(F32)).astype(BF16)
    return hi, lo


def _filter_spectrum_kernel(hs_ref, hd_ref, c_ref, s_ref, kr_ref, ki_ref, kn_ref, *, scale):
    hs = hs_ref[...]
    hd = hd_ref[...]
    hs_hi, hs_lo = _split_bf16(hs)
    hd_hi, hd_lo = _split_bf16(hd)
    c = c_ref[...]
    s = s_ref[...]
    kr_ref[...] = ((_dot(c, hs_hi) + _dot(c, hs_lo)) * scale).astype(kr_ref.dtype)
    ki_ref[...] = ((_dot(s, hd_hi) + _dot(s, hd_lo)) * (-scale)).astype(ki_ref.dtype)
    row = lax.broadcasted_iota(jnp.int32, hs.shape, 0)
    sign = jnp.where(row % 2 == 0, 1.0, -1.0)
    kn = jnp.sum(hs * sign, axis=0, keepdims=True) * scale
    kn_ref[...] = jnp.broadcast_to(kn, kn_ref.shape)


def _filter_spectrum(hs, hd, cmat, smat):
    seq, ch = hs.shape
    scale = 2.0 / (2 * seq)
    return pl.pallas_call(
        functools.partial(_filter_spectrum_kernel, scale=scale),
        out_shape=[jax.ShapeDtypeStruct((seq, ch), BF16), jax.ShapeDtypeStruct((seq, ch), BF16),
                   jax.ShapeDtypeStruct((8, ch), F32)],
        compiler_params=pltpu.CompilerParams(vmem_limit_bytes=V7X_VMEM_LIMIT_BYTES),
        name="hyena_filter_spectrum",
    )(hs, hd, cmat, smat)


def _hyena_kernel(z_ref, cw_ref, cb_ref, c_ref, s_ref, kr_ref, ki_ref, kn_ref, g_ref, o_ref, *, seq, ch):
    row = lax.broadcasted_iota(jnp.int32, (seq, 1), 0)
    sign = jnp.where(row % 2 == 0, 1.0, -1.0)

    def short_conv(part):
        cols = slice(part * ch, (part + 1) * ch)
        z = z_ref[0, :, cols].astype(F32)
        z_prev = jnp.where(row >= 1, pltpu.roll(z, 1, 0), 0.0)
        z_next = jnp.where(row <= seq - 2, pltpu.roll(z, seq - 1, 0), 0.0)
        return z_prev * cw_ref[0:1, cols] + z * cw_ref[1:2, cols] + z_next * cw_ref[2:3, cols] + cb_ref[:, cols]

    def long_conv(x, order):
        cols = slice(order * ch, (order + 1) * ch)
        xb = x.astype(BF16)
        a = _dot(c_ref[...], xb)
        bm = _dot(s_ref[...], xb)
        kr = kr_ref[:, cols]
        ki = ki_ref[:, cols]
        yr = a * kr + bm * ki
        yi = (a * ki - bm * kr).astype(BF16)
        dc = 0.5 * yr[0:1, :]
        x_nyq = jnp.sum(x * sign, axis=0, keepdims=True)
        y = _dot(c_ref[...], yr.astype(BF16)) - _dot(s_ref[...], yi)
        return y - dc + (0.5 * x_nyq * kn_ref[0:1, cols]) * sign

    y = short_conv(1) * long_conv(short_conv(0), 0)
    y = short_conv(2) * long_conv(y, 1)
    ms = jnp.mean(y * y, axis=-1, keepdims=True)
    o_ref[0] = y * lax.rsqrt(ms + RMS_EPS) * g_ref[...]


def _hyena(z_hy, conv_w, conv_b, cmat, smat, kr, ki, kn, norm_g):
    bsz, seq, c3 = z_hy.shape
    ch = c3 // 3
    return pl.pallas_call(
        functools.partial(_hyena_kernel, seq=seq, ch=ch),
        out_shape=jax.ShapeDtypeStruct((bsz, seq, ch), F32),
        grid=(bsz,),
        in_specs=[pl.BlockSpec((1, seq, c3), lambda b: (b, 0, 0)),
                  _const_spec(conv_w.shape), _const_spec((1, c3)),
                  _const_spec(cmat.shape), _const_spec(smat.shape),
                  _const_spec(kr.shape), _const_spec(ki.shape), _const_spec(kn.shape),
                  _const_spec((1, ch))],
        out_specs=pl.BlockSpec((1, seq, ch), lambda b: (b, 0, 0)),
        compiler_params=_cparams(("parallel",)),
        name="hyena_mixer",
    )(z_hy, conv_w, conv_b.reshape(1, c3), cmat, smat, kr, ki, kn, norm_g.reshape(1, ch))


def _fnet_kernel(z_ref, c_ref, s_ref, bdc_ref, bds_ref, bdw_ref, b_ref, g_ref, o_ref, *, scale):
    zb = z_ref[0].astype(BF16)
    y1 = _dot(c_ref[...], zb)
    y2 = _dot(s_ref[...], zb)
    r = (_dot(y1.astype(BF16), bdc_ref[...]) - _dot(y2.astype(BF16), bds_ref[...])) * scale
    y = _dot(r.astype(BF16), bdw_ref[...]) + b_ref[...]
    ms = jnp.mean(y * y, axis=-1, keepdims=True)
    o_ref[0] = y * lax.rsqrt(ms + RMS_EPS) * g_ref[...]


def _block_diag(blocks):
    g, a, b = blocks.shape
    out = jnp.zeros((g * a, g * b), blocks.dtype)
    for i in range(g):
        out = out.at[i * a:(i + 1) * a, i * b:(i + 1) * b].set(blocks[i])
    return out


def _fnet(z_fn, cmat, smat, fn_w, fn_b, norm_g):
    bsz, seq, ch = z_fn.shape
    gd = ch // FN_GROUPS
    k = np.arange(gd)
    ang = 2.0 * np.pi * ((k[:, None] * k[None, :]) % gd) / gd
    eye = np.eye(FN_GROUPS)
    bdc = jnp.asarray(np.kron(eye, np.cos(ang)), BF16)
    bds = jnp.asarray(np.kron(eye, np.sin(ang)), BF16)
    bdw = _block_diag(fn_w).astype(BF16)
    scale = 1.0 / math.sqrt(seq * gd)
    return pl.pallas_call(
        functools.partial(_fnet_kernel, scale=scale),
        out_shape=jax.ShapeDtypeStruct((bsz, seq, ch), F32),
        grid=(bsz,),
        in_specs=[pl.BlockSpec((1, seq, ch), lambda b: (b, 0, 0)),
                  _const_spec(cmat.shape), _const_spec(smat.shape),
                  _const_spec((ch, ch)), _const_spec((ch, ch)), _const_spec((ch, ch)),
                  _const_spec((1, ch)), _const_spec((1, ch))],
        out_specs=pl.BlockSpec((1, seq, ch), lambda b: (b, 0, 0)),
        compiler_params=_cparams(("parallel",)),
        name="fnet_mixer",
    )(z_fn, cmat, smat, bdc, bds, bdw, fn_b.reshape(1, ch), norm_g.reshape(1, ch))


def _route(logits_t, rb_ref):
    rows = [logits_t[e:e + 1, :] for e in range(N_EXPERTS)]
    mx = functools.reduce(jnp.maximum, rows)
    ex = [jnp.exp(r - mx) for r in rows]
    inv = 1.0 / functools.reduce(lambda a, b: a + b, ex)
    probs = [e * inv for e in ex]
    sel = [probs[e] + rb_ref[e:e + 1, 0:1] for e in range(N_EXPERTS)]
    epg = EXPERTS_PER_GROUP
    gscore = []
    for g in range(N_GROUPS):
        s = sel[g * epg:(g + 1) * epg]
        pairs = [s[i] + s[j] for i in range(epg) for j in range(i + 1, epg)]
        gscore.append(functools.reduce(jnp.maximum, pairs))
    best = gscore[0]
    best_g = jnp.zeros_like(best, dtype=jnp.int32)
    for g in range(1, N_GROUPS):
        better = gscore[g] > best
        best_g = jnp.where(better, g, best_g)
        best = jnp.where(better, gscore[g], best)
    chosen = []
    for e in range(N_EXPERTS):
        g = e // epg
        beaten = jnp.zeros_like(best_g)
        for e2 in range(g * epg, (g + 1) * epg):
            if e2 == e:
                continue
            wins = (sel[e2] > sel[e]) | ((sel[e2] == sel[e]) & (e2 < e))
            beaten = beaten + wins.astype(jnp.int32)
        chosen.append(jnp.where((best_g == g) & (beaten < 2), probs[e], 0.0))
    tot = functools.reduce(lambda a, b: a + b, chosen)
    inv_tot = 1.0 / tot
    return jnp.concatenate([c * inv_tot for c in chosen], axis=0)


def _outproj_kernel(*refs, add_pos, n_heads):
    if add_pos:
        x_ref, pos_ref = refs[:2]
        rest = refs[2:]
        x = x_ref[0] + pos_ref[...]
    else:
        x_ref = refs[0]
        rest = refs[1:]
        x = x_ref[0]
    (yhy_ref, of_ref, ob_ref, g_ref, yfn_ref, wout_ref, hgg_ref, gate_ref, n2g_ref, sh_ref, sc_ref,
     rwt_ref, rb_ref, xo_ref, h_ref, comb_ref) = rest
    o = of_ref[0] + ob_ref[0]
    parts = []
    for h in range(n_heads):
        oh = o[:, h * HG_HEAD_DIM:(h + 1) * HG_HEAD_DIM]
        parts.append(oh * lax.rsqrt(jnp.mean(oh * oh, axis=-1, keepdims=True) + RMS_EPS))
    y_hg = jnp.concatenate(parts, axis=-1) * hgg_ref[...] * _silu(g_ref[0])
    c_hy = yhy_ref.shape[-1]
    c_hg = y_hg.shape[-1]
    mix = (_dot(yhy_ref[0].astype(BF16), wout_ref[0:c_hy, :]) +
           _dot(y_hg.astype(BF16), wout_ref[c_hy:c_hy + c_hg, :]) +
           _dot(yfn_ref[0].astype(BF16), wout_ref[c_hy + c_hg:, :]))
    xn = x + gate_ref[0] * mix
    xo_ref[0] = xn
    ms = jnp.mean(xn * xn, axis=-1, keepdims=True)
    h2 = xn * lax.rsqrt(ms + RMS_EPS) * n2g_ref[...]
    h2 = h2 * (1.0 + sc_ref[0]) + sh_ref[0]
    h_ref[0] = h2.astype(BF16)
    logits_t = _dot_nt(rwt_ref[...], h2, precision=HIGHEST)
    comb_ref[0] = _route(logits_t, rb_ref)


def _out_proj(x, pos, y_hy, o_f, o_b, z_hg, y_fn, w_out, hg_norm_g, mod, mod_row, norm2_g, rwt, rb, tm):
    bsz, seq, d = x.shape
    add_pos = pos is not None
    c_hy, c_hg, c_fn = y_hy.shape[-1], o_f.shape[-1], y_fn.shape[-1]
    tok = lambda c, j=0: pl.BlockSpec((1, tm, c), lambda b, i: (b, i, j))
    modk = lambda k: pl.BlockSpec((1, 1, d), lambda b, i: (mod_row(b), 0, k))
    in_specs = [tok(d)]
    args = [x]
    if add_pos:
        in_specs.append(pl.BlockSpec((tm, d), lambda b, i: (i, 0)))
        args.append(pos)
    in_specs += [tok(c_hy), tok(c_hg), tok(c_hg), tok(c_hg, 4), tok(c_fn),
                 _const_spec(w_out.shape), _const_spec((1, c_hg)), modk(2), _const_spec((1, d)),
                 modk(3), modk(4), _const_spec(rwt.shape), _const_spec(rb.shape)]
    args += [y_hy, o_f, o_b, z_hg, y_fn, w_out, hg_norm_g.reshape(1, c_hg), mod, norm2_g.reshape(1, d),
             mod, mod, rwt, rb]
    nt = seq // tm
    out_shape = [jax.ShapeDtypeStruct((bsz, seq, d), F32), jax.ShapeDtypeStruct((bsz, seq, d), BF16),
                 jax.ShapeDtypeStruct((bsz * nt, N_EXPERTS, tm), F32)]
    out_specs = [tok(d), tok(d), pl.BlockSpec((1, N_EXPERTS, tm), lambda b, i: (b * nt + i, 0, 0))]
    return pl.pallas_call(
        functools.partial(_outproj_kernel, add_pos=add_pos, n_heads=c_hg // HG_HEAD_DIM),
        out_shape=out_shape, grid=(bsz, nt), in_specs=in_specs, out_specs=out_specs,
        compiler_params=_cparams(("parallel", "parallel")),
        name="out_proj_router",
    )(*args)


def _moe_kernel(*refs, final, cap):
    if final:
        (cnt_ref, h_ref, comb_ref, combt_ref, wg_ref, wu_ref, wd_ref, x_ref, gate_ref, fg_ref, o_ref,
         acc_ref, rcol_ref, rrow_ref) = refs
    else:
        (cnt_ref, h_ref, comb_ref, combt_ref, wg_ref, wu_ref, wd_ref, x_ref, gate_ref, o_ref,
         acc_ref, rcol_ref, rrow_ref) = refs
    e = pl.program_id(1)
    tm = h_ref.shape[0]
    n_rt = combt_ref.shape[0]

    @pl.when(e == 0)
    def _():
        acc_ref[...] = jnp.zeros_like(acc_ref)
        ti = lax.broadcasted_iota(jnp.int32, (tm, tm), 0)
        tj = lax.broadcasted_iota(jnp.int32, (tm, tm), 1)
        before_col = jnp.where(tj < ti, 1.0, 0.0).astype(BF16)
        before_row = jnp.where(ti < tj, 1.0, 0.0).astype(BF16)
        sel_col = jnp.where(comb_ref[...] > 0.0, 1.0, 0.0).astype(BF16)
        combt = jnp.concatenate([combt_ref[j] for j in range(n_rt)], axis=1)
        sel_row = jnp.where(combt > 0.0, 1.0, 0.0).astype(BF16)
        rcol_ref[...] = _dot(before_col, sel_col)
        rrow_ref[...] = _dot(sel_row, before_row)

    comb = comb_ref[...]
    lane = lax.broadcasted_iota(jnp.int32, comb.shape, 1)
    ce = jnp.sum(jnp.where(lane == e, comb, 0.0), axis=-1, keepdims=True)
    fits = cnt_ref[pl.program_id(0), e] <= cap

    @pl.when(fits)
    def _():
        h = h_ref[...]
        ce_row = jnp.concatenate([combt_ref[j, pl.ds(e, 1), :] for j in range(n_rt)], axis=1)
        rank_col = jnp.sum(jnp.where(lane == e, rcol_ref[...], 0.0), axis=-1, keepdims=True)
        rank_row = rrow_ref[pl.ds(e, 1), :]
        slot_r = lax.broadcasted_iota(jnp.int32, (cap, tm), 0).astype(F32)
        pick = jnp.where((slot_r == rank_row) & (ce_row > 0.0), 1.0, 0.0).astype(BF16)
        hc = _dot(pick, h).astype(BF16)
        a = _silu(_dot(hc, wg_ref[0])) * _dot(hc, wu_ref[0])
        y = _dot(a.astype(BF16), wd_ref[0]).astype(BF16)
        slot_c = lax.broadcasted_iota(jnp.int32, (tm, cap), 1).astype(F32)
        spread = jnp.where((slot_c == rank_col) & (ce > 0.0), 1.0, 0.0).astype(BF16)
        acc_ref[...] += ce * _dot(spread, y)

    @pl.when(jnp.logical_not(fits))
    def _():
        h = h_ref[...]
        a = _silu(_dot(h, wg_ref[0])) * _dot(h, wu_ref[0])
        acc_ref[...] += ce * _dot(a.astype(BF16), wd_ref[0])

    @pl.when(e == pl.num_programs(1) - 1)
    def _():
        y = x_ref[...] + gate_ref[0] * acc_ref[...]
        if final:
            ms = jnp.mean(y * y, axis=-1, keepdims=True)
            y = y * lax.rsqrt(ms + RMS_EPS) * fg_ref[...]
        o_ref[...] = y


def _moe(h, comb_t, wg, wu, wd, x, mod, mod_row_of_tile, final_g, tm):
    n, d = h.shape
    ne, _, de = wg.shape
    final = final_g is not None
    tr = comb_t.shape[-1]
    n_rt = tm // tr
    lanes = 128
    comb = jnp.transpose(comb_t, (0, 2, 1)).reshape(n, ne)
    comb = jnp.pad(comb, ((0, 0), (0, lanes - ne)))
    cap = min(MOE_CAP, tm)
    counts = jnp.sum((comb_t > 0.0).reshape(n // tm, n_rt, ne, tr), axis=(1, 3)).astype(jnp.int32)
    in_specs = [pl.BlockSpec((tm, d), lambda i, e, c: (i, 0)),
                pl.BlockSpec((tm, lanes), lambda i, e, c: (i, 0)),
                pl.BlockSpec((n_rt, ne, tr), lambda i, e, c: (i, 0, 0)),
                pl.BlockSpec((1, d, de), lambda i, e, c: (e, 0, 0)),
                pl.BlockSpec((1, d, de), lambda i, e, c: (e, 0, 0)),
                pl.BlockSpec((1, de, d), lambda i, e, c: (e, 0, 0)),
                pl.BlockSpec((tm, d), lambda i, e, c: (i, 0)),
                pl.BlockSpec((1, 1, d), lambda i, e, c: (mod_row_of_tile(i), 0, 5))]
    args = [counts, h, comb, comb_t, wg, wu, wd, x, mod]
    if final:
        in_specs.append(pl.BlockSpec((1, d), lambda i, e, c: (0, 0)))
        args.append(final_g.reshape(1, d))
    return pl.pallas_call(
        functools.partial(_moe_kernel, final=final, cap=cap),
        out_shape=jax.ShapeDtypeStruct((n, d), F32),
        grid_spec=pltpu.PrefetchScalarGridSpec(
            num_scalar_prefetch=1, grid=(n // tm, ne), in_specs=in_specs,
            out_specs=pl.BlockSpec((tm, d), lambda i, e, c: (i, 0)),
            scratch_shapes=[pltpu.VMEM((tm, d), F32), pltpu.VMEM((tm, lanes), F32),
                            pltpu.VMEM((ne, tm), F32)]),
        compiler_params=_cparams(("parallel", "arbitrary")),
        name="moe_experts",
    )(*args)


def _grid_sincos(n_tok, d):
    rows = n_tok // GRID_W
    row = jnp.repeat(jnp.arange(rows, dtype=F32), GRID_W)
    col = jnp.tile(jnp.arange(GRID_W, dtype=F32), rows)
    quarter = d // 4
    omega = 1.0 / (POS_BASE ** (jnp.arange(quarter, dtype=F32) / quarter))
    ar = row[:, None] * omega
    ac = col[:, None] * omega
    return jnp.concatenate([jnp.sin(ar), jnp.cos(ar), jnp.sin(ac), jnp.cos(ac)], axis=-1)


def _hyena_filters(seq, ch, w1, b1, w2, b2, w3, freq):
    t = jnp.linspace(0.0, 1.0, seq, dtype=F32)[:, None]
    w = 2.0 * math.pi * jnp.arange(seq, dtype=F32)[:, None] / seq
    bands = jnp.linspace(1e-4, HY_BANDS - 1, HY_BANDS, dtype=F32)[None, :]
    z = jnp.concatenate([t, jnp.cos(bands * w), -jnp.sin(bands * w)], axis=-1)
    hp = functools.partial(jnp.dot, precision=HIGHEST)
    h = jnp.sin(freq * (hp(z, w1) + b1))
    h = jnp.sin(freq * (hp(h, w2) + b2))
    h = hp(h, w3).reshape(seq, 2 * HY_ORDER, ch)
    max_decay = math.log(HY_DECAY_TARGET) / HY_FAST_DECAY_PCT
    min_decay = math.log(HY_DECAY_TARGET) / HY_SLOW_DECAY_PCT
    deltas = jnp.linspace(min_decay, max_decay, ch, dtype=F32)
    return h * jnp.exp(-t[:, :, None] * jnp.abs(deltas))


def _filter_halves(h, bias):
    seq = h.shape[0]
    not0 = (jnp.arange(seq) > 0)[:, None].astype(F32)
    hs, hd = [], []
    for o in range(HY_ORDER):
        hf = h[:, 2 * o].at[0].add(bias[o])
        hb = h[:, 2 * o + 1] * not0
        hs.append(hf + hb)
        hd.append(hf - hb)
    return jnp.concatenate(hs, axis=-1), jnp.concatenate(hd, axis=-1)


def kernel(x, c, ctx, c_ctx, ada_w, ada_b, norm1_g, norm2_g, w_in, w_out, hy_conv_w, hy_conv_b, hy_filt_w1, hy_filt_b1, hy_filt_w2, hy_filt_b2, hy_filt_w3, hy_filt_freq, hy_bias, hy_norm_g, hg_lower_bounds, hg_norm_g, fn_w, fn_b, fn_norm_g, router_w, router_b, moe_w_gate, moe_w_up, moe_w_down, final_norm_g):
    bsz, seq, d = x.shape
    n_ctx = ctx.shape[1]
    depth = ada_w.shape[0]
    c_hy = hy_norm_g.shape[-1]
    c_hyp = hy_conv_w.shape[-1]
    w_hg = hg_norm_g.shape[-1]
    c_fn = fn_norm_g.shape[-1]
    hg_lo, hg_hi = c_hyp, c_hyp + 5 * w_hg
    nh = w_hg // HG_HEAD_DIM

    cs = jnp.cumsum(jax.nn.softmax(hg_lower_bounds.astype(F32), axis=0), axis=0)
    lower_bounds = cs - cs[0:1]
    pos = _grid_sincos(seq, d)

    n_rows = -(-(bsz + 1) // 8) * 8
    cond = jnp.zeros((n_rows, d), F32).at[:bsz].set(c).at[bsz].set(c_ctx)
    mods = _ada_modulation(cond, ada_w, ada_b)
    x_row = lambda b: b
    c_row = lambda b: bsz

    dft = {}
    for n in (seq, n_ctx):
        dft[("hy", n)] = _dft_mats(n, 2 * n)
        dft[("fn", n)] = _dft_mats(n, n)

    rwt = jnp.transpose(router_w).astype(F32)
    rb = jnp.broadcast_to(router_b.astype(F32)[:, None], (N_EXPERTS, 128))
    s_zero = jnp.zeros((bsz, nh, HG_HEAD_DIM, HG_HEAD_DIM), F32)

    tm_x = min(512, seq)
    tm_c = min(256, n_ctx)
    tb_x = min(256, seq)
    tb_c = min(256, n_ctx)
    segs = [(hg_lo, hg_hi, F32), (0, c_hyp, BF16), (hg_hi, hg_hi + c_fn, BF16)]

    def mixer(z_hy, z_fn, n, layer):
        cmat, smat = dft[("hy", n)]
        h = _hyena_filters(n, c_hy, hy_filt_w1[layer], hy_filt_b1[layer], hy_filt_w2[layer],
                           hy_filt_b2[layer], hy_filt_w3[layer], hy_filt_freq[layer])
        hs, hd = _filter_halves(h, hy_bias[layer])
        kr, ki, kn = _filter_spectrum(hs, hd, cmat, smat)
        y_hy = _hyena(z_hy, hy_conv_w[layer], hy_conv_b[layer], cmat, smat, kr, ki, kn, hy_norm_g[layer])
        fc, fs = dft[("fn", n)]
        y_fn = _fnet(z_fn, fc, fs, fn_w[layer], fn_b[layer], fn_norm_g[layer])
        return y_hy, y_fn

    for layer in range(depth):
        last = layer == depth - 1
        mod = mods[layer].reshape(n_rows, 1, N_MOD * d)
        w_in_b = w_in[layer].astype(BF16)
        w_out_b = w_out[layer].astype(BF16)
        wg = moe_w_gate[layer].astype(BF16)
        wu = moe_w_up[layer].astype(BF16)
        wd = moe_w_down[layer].astype(BF16)
        lb = lower_bounds[layer]
        x_pos = pos if layer == 0 else None

        if last:
            (zc_hg,) = _in_proj(ctx, None, norm1_g[layer], mod, c_row, w_in_b[:, hg_lo:hg_hi],
                                [(0, 5 * w_hg, F32)], tm_c)
        else:
            zc_hg, zc_hy, zc_fn = _in_proj(ctx, None, norm1_g[layer], mod, c_row, w_in_b, segs, tm_c)
        oc_f, oc_b, s_f, s_b = _hgrn2_scan(zc_hg, lb, s_zero, s_zero, tb_c)

        zx_hg, zx_hy, zx_fn = _in_proj(x, x_pos, norm1_g[layer], mod, x_row, w_in_b, segs, tm_x)
        ox_f, ox_b, _, _ = _hgrn2_scan(zx_hg, lb, s_f, s_b, tb_x)
        yx_hy, yx_fn = mixer(zx_hy, zx_fn, seq, layer)
        x_mid, hx2, comb_x = _out_proj(x, x_pos, yx_hy, ox_f, ox_b, zx_hg, yx_fn, w_out_b, hg_norm_g[layer],
                                       mod, x_row, norm2_g[layer], rwt, rb, tm_x)
        tm_moe = min(1024, seq)
        tiles_per_seq = seq // tm_moe
        x = _moe(hx2.reshape(bsz * seq, d), comb_x, wg, wu, wd, x_mid.reshape(bsz * seq, d), mod,
                 lambda i: i // tiles_per_seq, final_norm_g if last else None, tm_moe).reshape(bsz, seq, d)

        if not last:
            yc_hy, yc_fn = mixer(zc_hy, zc_fn, n_ctx, layer)
            c_mid, hc2, comb_c = _out_proj(ctx, None, yc_hy, oc_f, oc_b, zc_hg, yc_fn, w_out_b,
                                           hg_norm_g[layer], mod, c_row, norm2_g[layer], rwt, rb, tm_c)
            tm_mc = min(1024, bsz * n_ctx)
            ctx = _moe(hc2.reshape(bsz * n_ctx, d), comb_c, wg, wu, wd, c_mid.reshape(bsz * n_ctx, d), mod,
                       lambda i: bsz, None, tm_mc).reshape(bsz, n_ctx, d)
    return x
```

```python
import functools
import math

import numpy as np
import jax
import jax.numpy as jnp
from jax import lax
from jax.experimental import pallas as pl
from jax.experimental.pallas import tpu as pltpu

F32 = jnp.float32
BF16 = jnp.bfloat16
HIGHEST = lax.Precision.HIGHEST

GRID_W = 64
HY_ORDER = 2
HY_BANDS = 16
HY_DECAY_TARGET = 1e-2
HY_FAST_DECAY_PCT = 0.3
HY_SLOW_DECAY_PCT = 1.5
HG_HEAD_DIM = 128
FN_GROUPS = 4
N_EXPERTS = 16
N_GROUPS = 4
EXPERTS_PER_GROUP = N_EXPERTS // N_GROUPS
N_MOD = 6
RMS_EPS = 1e-6
POS_BASE = 10000.0

V7X_VMEM_LIMIT_BYTES = 56 * 1024 * 1024
SCAN_CHUNK = 64
SCAN_SUB = 16
SCAN_SAFE_LOG2_RANGE = 80.0
LOG2_E = 1.0 / math.log(2.0)
MOE_CAP = 192
DFT_SPLIT = 64
NEG_BIG = -1e30


def _cparams(sem):
    return pltpu.CompilerParams(dimension_semantics=sem, vmem_limit_bytes=V7X_VMEM_LIMIT_BYTES)


def _const_spec(shape):
    nd = len(shape)
    return pl.BlockSpec(shape, lambda *_: (0,) * nd, pipeline_mode=pl.Buffered(1))


def _silu(x):
    return x * jax.nn.sigmoid(x)


def _dot(a, b):
    return jnp.dot(a, b, preferred_element_type=F32)


def _dot_nt(a, b, precision=None):
    return lax.dot_general(a, b, (((1,), (1,)), ((), ())), precision=precision,
                           preferred_element_type=F32)


def _dot_tn(a, b):
    return lax.dot_general(a, b, (((0,), (0,)), ((), ())), preferred_element_type=F32)


def _ada_kernel(c_ref, w_ref, b_ref, o_ref):
    s = _silu(c_ref[...])
    o_ref[0] = jnp.dot(s, w_ref[0], precision=HIGHEST, preferred_element_type=F32) + b_ref[0]


def _ada_modulation(cond, ada_w, ada_b):
    depth, d, nd = ada_w.shape
    rows = cond.shape[0]
    tn = 1536
    return pl.pallas_call(
        _ada_kernel,
        out_shape=jax.ShapeDtypeStruct((depth, rows, nd), F32),
        grid=(depth, nd // tn),
        in_specs=[pl.BlockSpec((rows, d), lambda l, j: (0, 0)),
                  pl.BlockSpec((1, d, tn), lambda l, j: (l, 0, j)),
                  pl.BlockSpec((1, 1, tn), lambda l, j: (l, 0, j))],
        out_specs=pl.BlockSpec((1, rows, tn), lambda l, j: (l, 0, j)),
        compiler_params=_cparams(("parallel", "parallel")),
        name="ada_modulation",
    )(cond, ada_w, ada_b.reshape(depth, 1, nd))


def _inproj_kernel(*refs, add_pos, segs):
    if add_pos:
        x_ref, pos_ref, g_ref, sh_ref, sc_ref, w_ref = refs[:6]
        outs = refs[6:]
        x = x_ref[0] + pos_ref[...]
    else:
        x_ref, g_ref, sh_ref, sc_ref, w_ref = refs[:5]
        outs = refs[5:]
        x = x_ref[0]
    ms = jnp.mean(x * x, axis=-1, keepdims=True)
    h = x * lax.rsqrt(ms + RMS_EPS) * g_ref[...]
    h = (h * (1.0 + sc_ref[0]) + sh_ref[0]).astype(BF16)
    for o_ref, (a, b, _) in zip(outs, segs):
        o_ref[0] = _dot(h, w_ref[:, a:b]).astype(o_ref.dtype)


def _in_proj(x, pos, norm_g, mod, mod_row, w, segs, tm):
    bsz, seq, d = x.shape
    add_pos = pos is not None
    in_specs = [pl.BlockSpec((1, tm, d), lambda b, i: (b, i, 0))]
    args = [x]
    if add_pos:
        in_specs.append(pl.BlockSpec((tm, d), lambda b, i: (i, 0)))
        args.append(pos)
    in_specs += [pl.BlockSpec((1, d), lambda b, i: (0, 0)),
                 pl.BlockSpec((1, 1, d), lambda b, i: (mod_row(b), 0, 0)),
                 pl.BlockSpec((1, 1, d), lambda b, i: (mod_row(b), 0, 1)),
                 _const_spec(w.shape)]
    args += [norm_g.reshape(1, d), mod, mod, w]
    out_shape = [jax.ShapeDtypeStruct((bsz, seq, b - a), dt) for a, b, dt in segs]
    out_specs = [pl.BlockSpec((1, tm, b - a), lambda bb, i: (bb, i, 0)) for a, b, _ in segs]
    return pl.pallas_call(
        functools.partial(_inproj_kernel, add_pos=add_pos, segs=tuple(segs)),
        out_shape=out_shape, grid=(bsz, seq // tm), in_specs=in_specs, out_specs=out_specs,
        compiler_params=_cparams(("parallel", "parallel")),
        name="in_proj",
    )(*args)


def _cumsum_rows(x, reverse):
    c = x.shape[0]
    row = lax.broadcasted_iota(jnp.int32, x.shape, 0)
    sh = 1
    while sh < c:
        if reverse:
            x = x + jnp.where(row < c - sh, pltpu.roll(x, c - sh, 0), 0.0)
        else:
            x = x + jnp.where(row >= sh, pltpu.roll(x, sh, 0), 0.0)
        sh *= 2
    return x


def _scan_chunk_fast(q, lf2, k, v, st_ref, h, reverse):
    c = SCAN_CHUNK
    s = SCAN_SUB
    nsub = c // s
    b = _cumsum_rows(lf2, reverse)
    qs = _silu(q) * (HG_HEAD_DIM ** -0.5)
    st = st_ref[h]
    o_inter = _dot_nt((qs * jnp.exp2(b)).astype(BF16), st.astype(BF16))
    v_bf = v.astype(BF16)
    zero_row = jnp.zeros((1, HG_HEAD_DIM), F32)
    zero_blk = jnp.zeros((s, HG_HEAD_DIM), BF16)
    kt = {}
    prev_ref = None
    sc = [None] * nsub
    for i in (range(nsub - 1, -1, -1) if reverse else range(nsub)):
        r0 = i * s
        bi = b[r0:r0 + s]
        if reverse:
            ref = b[r0 + s:r0 + s + 1] if i < nsub - 1 else zero_row
        else:
            ref = b[r0 - 1:r0] if i > 0 else zero_row
        if prev_ref is not None:
            step = jnp.exp2(ref - prev_ref)
            kt = {j: blk * step for j, blk in kt.items()}
        kt[i] = k[r0:r0 + s] * jnp.exp2(ref - bi)
        prev_ref = ref
        qi = (qs[r0:r0 + s] * jnp.exp2(bi - ref)).astype(BF16)
        keys = jnp.concatenate([kt[j].astype(BF16) if j in kt else zero_blk for j in range(nsub)], axis=0)
        sc[i] = _dot_nt(qi, keys)
    scores = jnp.concatenate(sc, axis=0)
    rr = lax.broadcasted_iota(jnp.int32, (c, c), 0)
    cc = lax.broadcasted_iota(jnp.int32, (c, c), 1)
    scores = jnp.where((cc >= rr) if reverse else (cc <= rr), scores, 0.0)
    o = o_inter + _dot(scores.astype(BF16), v_bf)

    b_end = b[0:1] if reverse else b[c - 1:c]
    kd = (k * jnp.exp2(b_end - b)).astype(BF16)
    st_ref[h] = st * jnp.exp2(b_end) + _dot_tn(v_bf, kd)
    return o


def _scan_chunk_exact(q, lf2, k, v, st_ref, h, b_scr, q_scr, reverse):
    c = SCAN_CHUNK
    s = SCAN_SUB
    nsub = c // s
    b = _cumsum_rows(lf2, reverse)
    qs = _silu(q) * (HG_HEAD_DIM ** -0.5)
    st = st_ref[h]
    o_inter = _dot_nt((qs * jnp.exp2(b)).astype(BF16), st.astype(BF16))

    b_scr[...] = b
    q_scr[...] = qs
    v_bf = v.astype(BF16)
    ones = jnp.ones((HG_HEAD_DIM, HG_HEAD_DIM), BF16)
    sub_iota = lax.broadcasted_iota(jnp.int32, (s, HG_HEAD_DIM), 0)
    row_iota = lax.broadcasted_iota(jnp.int32, (c, HG_HEAD_DIM), 0)
    grp = (lax.broadcasted_iota(jnp.int32, (s, s * s), 1) // s ==
           lax.broadcasted_iota(jnp.int32, (s, s * s), 0)).astype(BF16)
    o_parts = []
    for i in range(nsub):
        r0 = i * s
        bi = b[r0:r0 + s]
        ki = k[r0:r0 + s]
        vi = v[r0:r0 + s]
        prods = []
        for t in range(s):
            bt = b_scr[pl.ds(r0 + t, 1), :]
            qt = q_scr[pl.ds(r0 + t, 1), :]
            keep = (sub_iota >= t) if reverse else (sub_iota <= t)
            e = jnp.exp2(jnp.where(keep, bt - bi, NEG_BIG))
            prods.append(((qt * ki) * e).astype(BF16))
        p = jnp.concatenate(prods, axis=0)
        rsum = _dot(p, ones)
        zt = (rsum * jnp.concatenate([vi] * s, axis=0)).astype(BF16)
        o_i = _dot(grp, zt)
        if reverse and i < nsub - 1:
            ref_row = b[r0 + s:r0 + s + 1]
            key_rows = row_iota >= r0 + s
        elif (not reverse) and i > 0:
            ref_row = b[r0 - 1:r0]
            key_rows = row_iota < r0
        else:
            ref_row = None
        if ref_row is not None:
            qi = qs[r0:r0 + s] * jnp.exp2(bi - ref_row)
            ks = k * jnp.exp2(jnp.where(key_rows, ref_row - b, NEG_BIG))
            sc = _dot_nt(qi.astype(BF16), ks.astype(BF16))
            o_i = o_i + _dot(sc.astype(BF16), v_bf)
        o_parts.append(o_i)
    o = o_inter + jnp.concatenate(o_parts, axis=0)

    b_end = b[0:1] if reverse else b[c - 1:c]
    kd = (k * jnp.exp2(b_end - b)).astype(BF16)
    st_ref[h] = st * jnp.exp2(b_end) + _dot_tn(v_bf, kd)
    return o


def _scan_kernel(qf_ref, qb_ref, ff_ref, fb_ref, vf_ref, vb_ref, lb_ref, s0f_ref, s0b_ref,
                 of_ref, ob_ref, sf_ref, sb_ref, stf, stb, lf_scr, k_scr, bf_scr, qf_scr, bb_scr, qb_scr,
                 *, n_heads, n_chunks):
    n = pl.program_id(1)

    @pl.when(n == 0)
    def _():
        stf[...] = s0f_ref[0]
        stb[...] = s0b_ref[0]

    tb = n_chunks * SCAN_CHUNK
    nblk = tb // SCAN_SUB
    sel = (lax.broadcasted_iota(jnp.int32, (nblk, tb), 1) // SCAN_SUB ==
           lax.broadcasted_iota(jnp.int32, (nblk, tb), 0)).astype(BF16)
    worst = jnp.zeros((nblk, lb_ref.shape[1]), F32)
    for d, f_ref in enumerate((ff_ref, fb_ref)):
        lb = lb_ref[d:d + 1, :]
        forget = lb + (1.0 - lb) * jax.nn.sigmoid(f_ref[0])
        lf2 = jnp.log(forget) * LOG2_E
        lf_scr[d] = lf2
        k_scr[d] = 1.0 - forget
        worst = jnp.maximum(worst, -_dot(sel, lf2.astype(BF16)))
    safe = jnp.max(worst) <= SCAN_SAFE_LOG2_RANGE

    def rows_of(ci):
        rf = pl.ds(pl.multiple_of(ci * SCAN_CHUNK, SCAN_CHUNK), SCAN_CHUNK)
        rb = pl.ds(pl.multiple_of((n_chunks - 1 - ci) * SCAN_CHUNK, SCAN_CHUNK), SCAN_CHUNK)
        return rf, rb

    @pl.when(safe)
    def _():
        def chunk_body(ci, carry):
            rf, rb = rows_of(ci)
            for h in range(n_heads):
                cols = slice(h * HG_HEAD_DIM, (h + 1) * HG_HEAD_DIM)
                of_ref[0, rf, cols] = _scan_chunk_fast(qf_ref[0, rf, cols], lf_scr[0, rf, cols],
                                                       k_scr[0, rf, cols], vf_ref[0, rf, cols], stf, h, False)
                ob_ref[0, rb, cols] = _scan_chunk_fast(qb_ref[0, rb, cols], lf_scr[1, rb, cols],
                                                       k_scr[1, rb, cols], vb_ref[0, rb, cols], stb, h, True)
            return carry

        lax.fori_loop(0, n_chunks, chunk_body, 0)

    @pl.when(jnp.logical_not(safe))
    def _():
        def head_body(h, carry):
            cols = pl.ds(pl.multiple_of(h * HG_HEAD_DIM, HG_HEAD_DIM), HG_HEAD_DIM)

            def chunk_body(ci, carry2):
                rf, rb = rows_of(ci)
                of_ref[0, rf, cols] = _scan_chunk_exact(qf_ref[0, rf, cols], lf_scr[0, rf, cols],
                                                        k_scr[0, rf, cols], vf_ref[0, rf, cols], stf, h,
                                                        bf_scr, qf_scr, False)
                ob_ref[0, rb, cols] = _scan_chunk_exact(qb_ref[0, rb, cols], lf_scr[1, rb, cols],
                                                        k_scr[1, rb, cols], vb_ref[0, rb, cols], stb, h,
                                                        bb_scr, qb_scr, True)
                return carry2

            return lax.fori_loop(0, n_chunks, chunk_body, carry)

        lax.fori_loop(0, n_heads, head_body, 0)

    @pl.when(n == pl.num_programs(1) - 1)
    def _():
        sf_ref[0] = stf[...]
        sb_ref[0] = stb[...]


def _hgrn2_scan(z_hg, lb, s0f, s0b, tb):
    bsz, seq, w5 = z_hg.shape
    w = w5 // 5
    nh = w // HG_HEAD_DIM
    nb = seq // tb
    blk = (1, tb, w)
    fwd = lambda j: pl.BlockSpec(blk, lambda b, n: (b, n, j))
    bwd = lambda j: pl.BlockSpec(blk, lambda b, n: (b, nb - 1 - n, j))
    st_spec = pl.BlockSpec((1, nh, HG_HEAD_DIM, HG_HEAD_DIM), lambda b, n: (b, 0, 0, 0))
    st_shape = jax.ShapeDtypeStruct((bsz, nh, HG_HEAD_DIM, HG_HEAD_DIM), F32)
    o_shape = jax.ShapeDtypeStruct((bsz, seq, w), F32)
    return pl.pallas_call(
        functools.partial(_scan_kernel, n_heads=nh, n_chunks=tb // SCAN_CHUNK),
        out_shape=[o_shape, o_shape, st_shape, st_shape],
        grid=(bsz, nb),
        in_specs=[fwd(0), bwd(0), fwd(1), bwd(2), fwd(3), bwd(3),
                  pl.BlockSpec((2, w), lambda b, n: (0, 0)), st_spec, st_spec],
        out_specs=[fwd(0), bwd(0), st_spec, st_spec],
        scratch_shapes=[pltpu.VMEM((nh, HG_HEAD_DIM, HG_HEAD_DIM), F32),
                        pltpu.VMEM((nh, HG_HEAD_DIM, HG_HEAD_DIM), F32),
                        pltpu.VMEM((2, tb, w), F32),
                        pltpu.VMEM((2, tb, w), F32),
                        pltpu.VMEM((SCAN_CHUNK, HG_HEAD_DIM), F32),
                        pltpu.VMEM((SCAN_CHUNK, HG_HEAD_DIM), F32),
                        pltpu.VMEM((SCAN_CHUNK, HG_HEAD_DIM), F32),
                        pltpu.VMEM((SCAN_CHUNK, HG_HEAD_DIM), F32)],
        compiler_params=_cparams(("parallel", "arbitrary")),
        name="hgrn2_scan",
    )(z_hg, z_hg, z_hg, z_hg, z_hg, z_hg, lb, s0f, s0b)


def _dft_mats(n_rows, period):
    t = jnp.arange(n_rows, dtype=jnp.int32)

    def table(mult, count):
        m = (jnp.arange(count, dtype=jnp.int32)[:, None] * mult * t[None, :]) % period
        ang = m.astype(F32) * (2.0 * math.pi / period)
        return jnp.cos(ang), jnp.sin(ang)

    n_hi = n_rows // DFT_SPLIT
    ca, sa = table(DFT_SPLIT, n_hi)
    cb, sb = table(1, DFT_SPLIT)
    cmat = ca[:, None, :] * cb[None, :, :] - sa[:, None, :] * sb[None, :, :]
    smat = sa[:, None, :] * cb[None, :, :] + ca[:, None, :] * sb[None, :, :]
    return cmat.reshape(n_rows, n_rows).astype(BF16), smat.reshape(n_rows, n_rows).astype(BF16)


def _split_bf16(x):
    hi = x.astype(BF16)
    lo = (x - hi.astype(F32)).astype(BF16)
    return hi, lo


def _filter_spectrum_kernel(hs_ref, hd_ref, c_ref, s_ref, kr_ref, ki_ref, kn_ref, *, scale):
    hs = hs_ref[...]
    hd = hd_ref[...]
    hs_hi, hs_lo = _split_bf16(hs)
    hd_hi, hd_lo = _split_bf16(hd)
    c = c_ref[...]
    s = s_ref[...]
    kr_ref[...] = ((_dot(c, hs_hi) + _dot(c, hs_lo)) * scale).astype(kr_ref.dtype)
    ki_ref[...] = ((_dot(s, hd_hi) + _dot(s, hd_lo)) * (-scale)).astype(ki_ref.dtype)
    row = lax.broadcasted_iota(jnp.int32, hs.shape, 0)
    sign = jnp.where(row % 2 == 0, 1.0, -1.0)
    kn = jnp.sum(hs * sign, axis=0, keepdims=True) * scale
    kn_ref[...] = jnp.broadcast_to(kn, kn_ref.shape)


def _filter_spectrum(hs, hd, cmat, smat):
    seq, ch = hs.shape
    scale = 2.0 / (2 * seq)
    return pl.pallas_call(
        functools.partial(_filter_spectrum_kernel, scale=scale),
        out_shape=[jax.ShapeDtypeStruct((seq, ch), BF16), jax.ShapeDtypeStruct((seq, ch), BF16),
                   jax.ShapeDtypeStruct((8, ch), F32)],
        compiler_params=pltpu.CompilerParams(vmem_limit_bytes=V7X_VMEM_LIMIT_BYTES),
        name="hyena_filter_spectrum",
    )(hs, hd, cmat, smat)


def _hyena_kernel(z_ref, cw_ref, cb_ref, c_ref, s_ref, kr_ref, ki_ref, kn_ref, g_ref, o_ref, *, seq, ch):
    row = lax.broadcasted_iota(jnp.int32, (seq, 1), 0)
    sign = jnp.where(row % 2 == 0, 1.0, -1.0)

    def short_conv(part):
        cols = slice(part * ch, (part + 1) * ch)
        z = z_ref[0, :, cols].astype(F32)
        z_prev = jnp.where(row >= 1, pltpu.roll(z, 1, 0), 0.0)
        z_next = jnp.where(row <= seq - 2, pltpu.roll(z, seq - 1, 0), 0.0)
        return z_prev * cw_ref[0:1, cols] + z * cw_ref[1:2, cols] + z_next * cw_ref[2:3, cols] + cb_ref[:, cols]

    def long_conv(x, order):
        cols = slice(order * ch, (order + 1) * ch)
        xb = x.astype(BF16)
        a = _dot(c_ref[...], xb)
        bm = _dot(s_ref[...], xb)
        kr = kr_ref[:, cols]
        ki = ki_ref[:, cols]
        yr = a * kr + bm * ki
        yi = (a * ki - bm * kr).astype(BF16)
        dc = 0.5 * yr[0:1, :]
        x_nyq = jnp.sum(x * sign, axis=0, keepdims=True)
        y = _dot(c_ref[...], yr.astype(BF16)) - _dot(s_ref[...], yi)
        return y - dc + (0.5 * x_nyq * kn_ref[0:1, cols]) * sign

    y = short_conv(1) * long_conv(short_conv(0), 0)
    y = short_conv(2) * long_conv(y, 1)
    ms = jnp.mean(y * y, axis=-1, keepdims=True)
    o_ref[0] = y * lax.rsqrt(ms + RMS_EPS) * g_ref[...]


def _hyena(z_hy, conv_w, conv_b, cmat, smat, kr, ki, kn, norm_g):
    bsz, seq, c3 = z_hy.shape
    ch = c3 // 3
    return pl.pallas_call(
        functools.partial(_hyena_kernel, seq=seq, ch=ch),
        out_shape=jax.ShapeDtypeStruct((bsz, seq, ch), F32),
        grid=(bsz,),
        in_specs=[pl.BlockSpec((1, seq, c3), lambda b: (b, 0, 0)),
                  _const_spec(conv_w.shape), _const_spec((1, c3)),
                  _const_spec(cmat.shape), _const_spec(smat.shape),
                  _const_spec(kr.shape), _const_spec(ki.shape), _const_spec(kn.shape),
                  _const_spec((1, ch))],
        out_specs=pl.BlockSpec((1, seq, ch), lambda b: (b, 0, 0)),
        compiler_params=_cparams(("parallel",)),
        name="hyena_mixer",
    )(z_hy, conv_w, conv_b.reshape(1, c3), cmat, smat, kr, ki, kn, norm_g.reshape(1, ch))


def _fnet_kernel(z_ref, c_ref, s_ref, bdc_ref, bds_ref, bdw_ref, b_ref, g_ref, o_ref, *, scale):
    zb = z_ref[0].astype(BF16)
    y1 = _dot(c_ref[...], zb)
    y2 = _dot(s_ref[...], zb)
    r = (_dot(y1.astype(BF16), bdc_ref[...]) - _dot(y2.astype(BF16), bds_ref[...])) * scale
    y = _dot(r.astype(BF16), bdw_ref[...]) + b_ref[...]
    ms = jnp.mean(y * y, axis=-1, keepdims=True)
    o_ref[0] = y * lax.rsqrt(ms + RMS_EPS) * g_ref[...]


def _block_diag(blocks):
    g, a, b = blocks.shape
    out = jnp.zeros((g * a, g * b), blocks.dtype)
    for i in range(g):
        out = out.at[i * a:(i + 1) * a, i * b:(i + 1) * b].set(blocks[i])
    return out


def _fnet(z_fn, cmat, smat, fn_w, fn_b, norm_g):
    bsz, seq, ch = z_fn.shape
    gd = ch // FN_GROUPS
    k = np.arange(gd)
    ang = 2.0 * np.pi * ((k[:, None] * k[None, :]) % gd) / gd
    eye = np.eye(FN_GROUPS)
    bdc = jnp.asarray(np.kron(eye, np.cos(ang)), BF16)
    bds = jnp.asarray(np.kron(eye, np.sin(ang)), BF16)
    bdw = _block_diag(fn_w).astype(BF16)
    scale = 1.0 / math.sqrt(seq * gd)
    return pl.pallas_call(
        functools.partial(_fnet_kernel, scale=scale),
        out_shape=jax.ShapeDtypeStruct((bsz, seq, ch), F32),
        grid=(bsz,),
        in_specs=[pl.BlockSpec((1, seq, ch), lambda b: (b, 0, 0)),
                  _const_spec(cmat.shape), _const_spec(smat.shape),
                  _const_spec((ch, ch)), _const_spec((ch, ch)), _const_spec((ch, ch)),
                  _const_spec((1, ch)), _const_spec((1, ch))],
        out_specs=pl.BlockSpec((1, seq, ch), lambda b: (b, 0, 0)),
        compiler_params=_cparams(("parallel",)),
        name="fnet_mixer",
    )(z_fn, cmat, smat, bdc, bds, bdw, fn_b.reshape(1, ch), norm_g.reshape(1, ch))


def _route(logits_t, rb_ref):
    rows = [logits_t[e:e + 1, :] for e in range(N_EXPERTS)]
    mx = functools.reduce(jnp.maximum, rows)
    ex = [jnp.exp(r - mx) for r in rows]
    inv = 1.0 / functools.reduce(lambda a, b: a + b, ex)
    probs = [e * inv for e in ex]
    sel = [probs[e] + rb_ref[e:e + 1, 0:1] for e in range(N_EXPERTS)]
    epg = EXPERTS_PER_GROUP
    gscore = []
    for g in range(N_GROUPS):
        s = sel[g * epg:(g + 1) * epg]
        pairs = [s[i] + s[j] for i in range(epg) for j in range(i + 1, epg)]
        gscore.append(functools.reduce(jnp.maximum, pairs))
    best = gscore[0]
    best_g = jnp.zeros_like(best, dtype=jnp.int32)
    for g in range(1, N_GROUPS):
        better = gscore[g] > best
        best_g = jnp.where(better, g, best_g)
        best = jnp.where(better, gscore[g], best)
    chosen = []
    for e in range(N_EXPERTS):
        g = e // epg
        beaten = jnp.zeros_like(best_g)
        for e2 in range(g * epg, (g + 1) * epg):
            if e2 == e:
                continue
            wins = (sel[e2] > sel[e]) | ((sel[e2] == sel[e]) & (e2 < e))
            beaten = beaten + wins.astype(jnp.int32)
        chosen.append(jnp.where((best_g == g) & (beaten < 2), probs[e], 0.0))
    tot = functools.reduce(lambda a, b: a + b, chosen)
    inv_tot = 1.0 / tot
    return jnp.concatenate([c * inv_tot for c in chosen], axis=0)


def _outproj_kernel(*refs, add_pos, n_heads):
    if add_pos:
        x_ref, pos_ref = refs[:2]
        rest = refs[2:]
        x = x_ref[0] + pos_ref[...]
    else:
        x_ref = refs[0]
        rest = refs[1:]
        x = x_ref[0]
    (yhy_ref, of_ref, ob_ref, g_ref, yfn_ref, wout_ref, hgg_ref, gate_ref, n2g_ref, sh_ref, sc_ref,
     rwt_ref, rb_ref, xo_ref, h_ref, comb_ref) = rest
    o = of_ref[0] + ob_ref[0]
    parts = []
    for h in range(n_heads):
        oh = o[:, h * HG_HEAD_DIM:(h + 1) * HG_HEAD_DIM]
        parts.append(oh * lax.rsqrt(jnp.mean(oh * oh, axis=-1, keepdims=True) + RMS_EPS))
    y_hg = jnp.concatenate(parts, axis=-1) * hgg_ref[...] * _silu(g_ref[0])
    c_hy = yhy_ref.shape[-1]
    c_hg = y_hg.shape[-1]
    mix = (_dot(yhy_ref[0].astype(BF16), wout_ref[0:c_hy, :]) +
           _dot(y_hg.astype(BF16), wout_ref[c_hy:c_hy + c_hg, :]) +
           _dot(yfn_ref[0].astype(BF16), wout_ref[c_hy + c_hg:, :]))
    xn = x + gate_ref[0] * mix
    xo_ref[0] = xn
    ms = jnp.mean(xn * xn, axis=-1, keepdims=True)
    h2 = xn * lax.rsqrt(ms + RMS_EPS) * n2g_ref[...]
    h2 = h2 * (1.0 + sc_ref[0]) + sh_ref[0]
    h_ref[0] = h2.astype(BF16)
    logits_t = _dot_nt(rwt_ref[...], h2, precision=HIGHEST)
    comb_ref[0] = _route(logits_t, rb_ref)


def _out_proj(x, pos, y_hy, o_f, o_b, z_hg, y_fn, w_out, hg_norm_g, mod, mod_row, norm2_g, rwt, rb, tm):
    bsz, seq, d = x.shape
    add_pos = pos is not None
    c_hy, c_hg, c_fn = y_hy.shape[-1], o_f.shape[-1], y_fn.shape[-1]
    tok = lambda c, j=0: pl.BlockSpec((1, tm, c), lambda b, i: (b, i, j))
    modk = lambda k: pl.BlockSpec((1, 1, d), lambda b, i: (mod_row(b), 0, k))
    in_specs = [tok(d)]
    args = [x]
    if add_pos:
        in_specs.append(pl.BlockSpec((tm, d), lambda b, i: (i, 0)))
        args.append(pos)
    in_specs += [tok(c_hy), tok(c_hg), tok(c_hg), tok(c_hg, 4), tok(c_fn),
                 _const_spec(w_out.shape), _const_spec((1, c_hg)), modk(2), _const_spec((1, d)),
                 modk(3), modk(4), _const_spec(rwt.shape), _const_spec(rb.shape)]
    args += [y_hy, o_f, o_b, z_hg, y_fn, w_out, hg_norm_g.reshape(1, c_hg), mod, norm2_g.reshape(1, d),
             mod, mod, rwt, rb]
    nt = seq // tm
    out_shape = [jax.ShapeDtypeStruct((bsz, seq, d), F32), jax.ShapeDtypeStruct((bsz, seq, d), BF16),
                 jax.ShapeDtypeStruct((bsz * nt, N_EXPERTS, tm), F32)]
    out_specs = [tok(d), tok(d), pl.BlockSpec((1, N_EXPERTS, tm), lambda b, i: (b * nt + i, 0, 0))]
    return pl.pallas_call(
        functools.partial(_outproj_kernel, add_pos=add_pos, n_heads=c_hg // HG_HEAD_DIM),
        out_shape=out_shape, grid=(bsz, nt), in_specs=in_specs, out_specs=out_specs,
        compiler_params=_cparams(("parallel", "parallel")),
        name="out_proj_router",
    )(*args)


def _moe_kernel(*refs, final, cap):
    if final:
        (cnt_ref, h_ref, comb_ref, combt_ref, wg_ref, wu_ref, wd_ref, x_ref, gate_ref, fg_ref, o_ref,
         acc_ref, rcol_ref, rrow_ref) = refs
    else:
        (cnt_ref, h_ref, comb_ref, combt_ref, wg_ref, wu_ref, wd_ref, x_ref, gate_ref, o_ref,
         acc_ref, rcol_ref, rrow_ref) = refs
    e = pl.program_id(1)
    tm = h_ref.shape[0]
    n_rt = combt_ref.shape[0]

    @pl.when(e == 0)
    def _():
        acc_ref[...] = jnp.zeros_like(acc_ref)
        ti = lax.broadcasted_iota(jnp.int32, (tm, tm), 0)
        tj = lax.broadcasted_iota(jnp.int32, (tm, tm), 1)
        before_col = jnp.where(tj < ti, 1.0, 0.0).astype(BF16)
        before_row = jnp.where(ti < tj, 1.0, 0.0).astype(BF16)
        sel_col = jnp.where(comb_ref[...] > 0.0, 1.0, 0.0).astype(BF16)
        combt = jnp.concatenate([combt_ref[j] for j in range(n_rt)], axis=1)
        sel_row = jnp.where(combt > 0.0, 1.0, 0.0).astype(BF16)
        rcol_ref[...] = _dot(before_col, sel_col)
        rrow_ref[...] = _dot(sel_row, before_row)

    comb = comb_ref[...]
    lane = lax.broadcasted_iota(jnp.int32, comb.shape, 1)
    ce = jnp.sum(jnp.where(lane == e, comb, 0.0), axis=-1, keepdims=True)
    fits = cnt_ref[pl.program_id(0), e] <= cap

    @pl.when(fits)
    def _():
        h = h_ref[...]
        ce_row = jnp.concatenate([combt_ref[j, pl.ds(e, 1), :] for j in range(n_rt)], axis=1)
        rank_col = jnp.sum(jnp.where(lane == e, rcol_ref[...], 0.0), axis=-1, keepdims=True)
        rank_row = rrow_ref[pl.ds(e, 1), :]
        slot_r = lax.broadcasted_iota(jnp.int32, (cap, tm), 0).astype(F32)
        pick = jnp.where((slot_r == rank_row) & (ce_row > 0.0), 1.0, 0.0).astype(BF16)
        hc = _dot(pick, h).astype(BF16)
        a = _silu(_dot(hc, wg_ref[0])) * _dot(hc, wu_ref[0])
        y = _dot(a.astype(BF16), wd_ref[0]).astype(BF16)
        slot_c = lax.broadcasted_iota(jnp.int32, (tm, cap), 1).astype(F32)
        spread = jnp.where((slot_c == rank_col) & (ce > 0.0), 1.0, 0.0).astype(BF16)
        acc_ref[...] += ce * _dot(spread, y)

    @pl.when(jnp.logical_not(fits))
    def _():
        h = h_ref[...]
        a = _silu(_dot(h, wg_ref[0])) * _dot(h, wu_ref[0])
        acc_ref[...] += ce * _dot(a.astype(BF16), wd_ref[0])

    @pl.when(e == pl.num_programs(1) - 1)
    def _():
        y = x_ref[...] + gate_ref[0] * acc_ref[...]
        if final:
            ms = jnp.mean(y * y, axis=-1, keepdims=True)
            y = y * lax.rsqrt(ms + RMS_EPS) * fg_ref[...]
        o_ref[...] = y


def _moe(h, comb_t, wg, wu, wd, x, mod, mod_row_of_tile, final_g, tm):
    n, d = h.shape
    ne, _, de = wg.shape
    final = final_g is not None
    tr = comb_t.shape[-1]
    n_rt = tm // tr
    lanes = 128
    comb = jnp.transpose(comb_t, (0, 2, 1)).reshape(n, ne)
    comb = jnp.pad(comb, ((0, 0), (0, lanes - ne)))
    cap = min(MOE_CAP, tm)
    counts = jnp.sum((comb_t > 0.0).reshape(n // tm, n_rt, ne, tr), axis=(1, 3)).astype(jnp.int32)
    in_specs = [pl.BlockSpec((tm, d), lambda i, e, c: (i, 0)),
                pl.BlockSpec((tm, lanes), lambda i, e, c: (i, 0)),
                pl.BlockSpec((n_rt, ne, tr), lambda i, e, c: (i, 0, 0)),
                pl.BlockSpec((1, d, de), lambda i, e, c: (e, 0, 0)),
                pl.BlockSpec((1, d, de), lambda i, e, c: (e, 0, 0)),
                pl.BlockSpec((1, de, d), lambda i, e, c: (e, 0, 0)),
                pl.BlockSpec((tm, d), lambda i, e, c: (i, 0)),
                pl.BlockSpec((1, 1, d), lambda i, e, c: (mod_row_of_tile(i), 0, 5))]
    args = [counts, h, comb, comb_t, wg, wu, wd, x, mod]
    if final:
        in_specs.append(pl.BlockSpec((1, d), lambda i, e, c: (0, 0)))
        args.append(final_g.reshape(1, d))
    return pl.pallas_call(
        functools.partial(_moe_kernel, final=final, cap=cap),
        out_shape=jax.ShapeDtypeStruct((n, d), F32),
        grid_spec=pltpu.PrefetchScalarGridSpec(
            num_scalar_prefetch=1, grid=(n // tm, ne), in_specs=in_specs,
            out_specs=pl.BlockSpec((tm, d), lambda i, e, c: (i, 0)),
            scratch_shapes=[pltpu.VMEM((tm, d), F32), pltpu.VMEM((tm, lanes), F32),
                            pltpu.VMEM((ne, tm), F32)]),
        compiler_params=_cparams(("parallel", "arbitrary")),
        name="moe_experts",
    )(*args)


def _grid_sincos(n_tok, d):
    rows = n_tok // GRID_W
    row = jnp.repeat(jnp.arange(rows, dtype=F32), GRID_W)
    col = jnp.tile(jnp.arange(GRID_W, dtype=F32), rows)
    quarter = d // 4
    omega = 1.0 / (POS_BASE ** (jnp.arange(quarter, dtype=F32) / quarter))
    ar = row[:, None] * omega
    ac = col[:, None] * omega
    return jnp.concatenate([jnp.sin(ar), jnp.cos(ar), jnp.sin(ac), jnp.cos(ac)], axis=-1)


def _hyena_filters(seq, ch, w1, b1, w2, b2, w3, freq):
    t = jnp.linspace(0.0, 1.0, seq, dtype=F32)[:, None]
    w = 2.0 * math.pi * jnp.arange(seq, dtype=F32)[:, None] / seq
    bands = jnp.linspace(1e-4, HY_BANDS - 1, HY_BANDS, dtype=F32)[None, :]
    z = jnp.concatenate([t, jnp.cos(bands * w), -jnp.sin(bands * w)], axis=-1)
    hp = functools.partial(jnp.dot, precision=HIGHEST)
    h = jnp.sin(freq * (hp(z, w1) + b1))
    h = jnp.sin(freq * (hp(h, w2) + b2))
    h = hp(h, w3).reshape(seq, 2 * HY_ORDER, ch)
    max_decay = math.log(HY_DECAY_TARGET) / HY_FAST_DECAY_PCT
    min_decay = math.log(HY_DECAY_TARGET) / HY_SLOW_DECAY_PCT
    deltas = jnp.linspace(min_decay, max_decay, ch, dtype=F32)
    return h * jnp.exp(-t[:, :, None] * jnp.abs(deltas))


def _filter_halves(h, bias):
    seq = h.shape[0]
    not0 = (jnp.arange(seq) > 0)[:, None].astype(F32)
    hs, hd = [], []
    for o in range(HY_ORDER):
        hf = h[:, 2 * o].at[0].add(bias[o])
        hb = h[:, 2 * o + 1] * not0
        hs.append(hf + hb)
        hd.append(hf - hb)
    return jnp.concatenate(hs, axis=-1), jnp.concatenate(hd, axis=-1)


def kernel(x, c, ctx, c_ctx, ada_w, ada_b, norm1_g, norm2_g, w_in, w_out, hy_conv_w, hy_conv_b, hy_filt_w1, hy_filt_b1, hy_filt_w2, hy_filt_b2, hy_filt_w3, hy_filt_freq, hy_bias, hy_norm_g, hg_lower_bounds, hg_norm_g, fn_w, fn_b, fn_norm_g, router_w, router_b, moe_w_gate, moe_w_up, moe_w_down, final_norm_g):
    bsz, seq, d = x.shape
    n_ctx = ctx.shape[1]
    depth = ada_w.shape[0]
    c_hy = hy_norm_g.shape[-1]
    c_hyp = hy_conv_w.shape[-1]
    w_hg = hg_norm_g.shape[-1]
    c_fn = fn_norm_g.shape[-1]
    hg_lo, hg_hi = c_hyp, c_hyp + 5 * w_hg
    nh = w_hg // HG_HEAD_DIM

    cs = jnp.cumsum(jax.nn.softmax(hg_lower_bounds.astype(F32), axis=0), axis=0)
    lower_bounds = cs - cs[0:1]
    pos = _grid_sincos(seq, d)

    n_rows = -(-(bsz + 1) // 8) * 8
    cond = jnp.zeros((n_rows, d), F32).at[:bsz].set(c).at[bsz].set(c_ctx)
    mods = _ada_modulation(cond, ada_w, ada_b)
    x_row = lambda b: b
    c_row = lambda b: bsz

    dft = {}
    for n in (seq, n_ctx):
        dft[("hy", n)] = _dft_mats(n, 2 * n)
        dft[("fn", n)] = _dft_mats(n, n)

    rwt = jnp.transpose(router_w).astype(F32)
    rb = jnp.broadcast_to(router_b.astype(F32)[:, None], (N_EXPERTS, 128))
    s_zero = jnp.zeros((bsz, nh, HG_HEAD_DIM, HG_HEAD_DIM), F32)

    tm_x = min(512, seq)
    tm_c = min(256, n_ctx)
    tb_x = min(256, seq)
    tb_c = min(256, n_ctx)
    segs = [(hg_lo, hg_hi, F32), (0, c_hyp, BF16), (hg_hi, hg_hi + c_fn, BF16)]

    def mixer(z_hy, z_fn, n, layer):
        cmat, smat = dft[("hy", n)]
        h = _hyena_filters(n, c_hy, hy_filt_w1[layer], hy_filt_b1[layer], hy_filt_w2[layer],
                           hy_filt_b2[layer], hy_filt_w3[layer], hy_filt_freq[layer])
        hs, hd = _filter_halves(h, hy_bias[layer])
        kr, ki, kn = _filter_spectrum(hs, hd, cmat, smat)
        y_hy = _hyena(z_hy, hy_conv_w[layer], hy_conv_b[layer], cmat, smat, kr, ki, kn, hy_norm_g[layer])
        fc, fs = dft[("fn", n)]
        y_fn = _fnet(z_fn, fc, fs, fn_w[layer], fn_b[layer], fn_norm_g[layer])
        return y_hy, y_fn

    for layer in range(depth):
        last = layer == depth - 1
        mod = mods[layer].reshape(n_rows, 1, N_MOD * d)
        w_in_b = w_in[layer].astype(BF16)
        w_out_b = w_out[layer].astype(BF16)
        wg = moe_w_gate[layer].astype(BF16)
        wu = moe_w_up[layer].astype(BF16)
        wd = moe_w_down[layer].astype(BF16)
        lb = lower_bounds[layer]
        x_pos = pos if layer == 0 else None

        if last:
            (zc_hg,) = _in_proj(ctx, None, norm1_g[layer], mod, c_row, w_in_b[:, hg_lo:hg_hi],
                                [(0, 5 * w_hg, F32)], tm_c)
        else:
            zc_hg, zc_hy, zc_fn = _in_proj(ctx, None, norm1_g[layer], mod, c_row, w_in_b, segs, tm_c)
        oc_f, oc_b, s_f, s_b = _hgrn2_scan(zc_hg, lb, s_zero, s_zero, tb_c)

        zx_hg, zx_hy, zx_fn = _in_proj(x, x_pos, norm1_g[layer], mod, x_row, w_in_b, segs, tm_x)
        ox_f, ox_b, _, _ = _hgrn2_scan(zx_hg, lb, s_f, s_b, tb_x)
        yx_hy, yx_fn = mixer(zx_hy, zx_fn, seq, layer)
        x_mid, hx2, comb_x = _out_proj(x, x_pos, yx_hy, ox_f, ox_b, zx_hg, yx_fn, w_out_b, hg_norm_g[layer],
                                       mod, x_row, norm2_g[layer], rwt, rb, tm_x)
        tm_moe = min(1024, seq)
        tiles_per_seq = seq // tm_moe
        x = _moe(hx2.reshape(bsz * seq, d), comb_x, wg, wu, wd, x_mid.reshape(bsz * seq, d), mod,
                 lambda i: i // tiles_per_seq, final_norm_g if last else None, tm_moe).reshape(bsz, seq, d)

        if not last:
            yc_hy, yc_fn = mixer(zc_hy, zc_fn, n_ctx, layer)
            c_mid, hc2, comb_c = _out_proj(ctx, None, yc_hy, oc_f, oc_b, zc_hg, yc_fn, w_out_b,
                                           hg_norm_g[layer], mod, c_row, norm2_g[layer], rwt, rb, tm_c)
            tm_mc = min(1024, bsz * n_ctx)
            ctx = _moe(hc2.reshape(bsz * n_ctx, d), comb_c, wg, wu, wd, c_mid.reshape(bsz * n_ctx, d), mod,
                       lambda i: bsz, None, tm_mc).reshape(bsz, n_ctx, d)
    return x
```

```python
import functools
import math

import numpy as np
import jax
import jax.numpy as jnp
from jax import lax
from jax.experimental import pallas as pl
from jax.experimental.pallas import tpu as pltpu

F32 = jnp.float32
BF16 = jnp.bfloat16
HIGHEST = lax.Precision.HIGHEST

GRID_W = 64
HY_ORDER = 2
HY_BANDS = 16
HY_DECAY_TARGET = 1e-2
HY_FAST_DECAY_PCT = 0.3
HY_SLOW_DECAY_PCT = 1.5
HG_HEAD_DIM = 128
FN_GROUPS = 4
N_EXPERTS = 16
N_GROUPS = 4
EXPERTS_PER_GROUP = N_EXPERTS // N_GROUPS
N_MOD = 6
RMS_EPS = 1e-6
POS_BASE = 10000.0

V7X_VMEM_LIMIT_BYTES = 56 * 1024 * 1024
SCAN_CHUNK = 64
SCAN_SUB = 16
SCAN_SAFE_LOG2_RANGE = 80.0
LOG2_E = 1.0 / math.log(2.0)
MOE_CAP = 160
DFT_SPLIT = 64
NEG_BIG = -1e30


def _cparams(sem):
    return pltpu.CompilerParams(dimension_semantics=sem, vmem_limit_bytes=V7X_VMEM_LIMIT_BYTES)


def _const_spec(shape):
    nd = len(shape)
    return pl.BlockSpec(shape, lambda *_: (0,) * nd, pipeline_mode=pl.Buffered(1))


def _silu(x):
    return x * jax.nn.sigmoid(x)


def _dot(a, b):
    return jnp.dot(a, b, preferred_element_type=F32)


def _dot_nt(a, b, precision=None):
    return lax.dot_general(a, b, (((1,), (1,)), ((), ())), precision=precision,
                           preferred_element_type=F32)


def _dot_tn(a, b):
    return lax.dot_general(a, b, (((0,), (0,)), ((), ())), preferred_element_type=F32)


def _ada_kernel(c_ref, w_ref, b_ref, o_ref):
    s = _silu(c_ref[...])
    o_ref[0] = jnp.dot(s, w_ref[0], precision=HIGHEST, preferred_element_type=F32) + b_ref[0]


def _ada_modulation(cond, ada_w, ada_b):
    depth, d, nd = ada_w.shape
    rows = cond.shape[0]
    tn = 1536
    return pl.pallas_call(
        _ada_kernel,
        out_shape=jax.ShapeDtypeStruct((depth, rows, nd), F32),
        grid=(depth, nd // tn),
        in_specs=[pl.BlockSpec((rows, d), lambda l, j: (0, 0)),
                  pl.BlockSpec((1, d, tn), lambda l, j: (l, 0, j)),
                  pl.BlockSpec((1, 1, tn), lambda l, j: (l, 0, j))],
        out_specs=pl.BlockSpec((1, rows, tn), lambda l, j: (l, 0, j)),
        compiler_params=_cparams(("parallel", "parallel")),
        name="ada_modulation",
    )(cond, ada_w, ada_b.reshape(depth, 1, nd))


def _inproj_kernel(*refs, add_pos, segs):
    if add_pos:
        x_ref, pos_ref, g_ref, sh_ref, sc_ref, w_ref = refs[:6]
        outs = refs[6:]
        x = x_ref[0] + pos_ref[...]
    else:
        x_ref, g_ref, sh_ref, sc_ref, w_ref = refs[:5]
        outs = refs[5:]
        x = x_ref[0]
    ms = jnp.mean(x * x, axis=-1, keepdims=True)
    h = x * lax.rsqrt(ms + RMS_EPS) * g_ref[...]
    h = (h * (1.0 + sc_ref[0]) + sh_ref[0]).astype(BF16)
    for o_ref, (a, b, _) in zip(outs, segs):
        o_ref[0] = _dot(h, w_ref[:, a:b]).astype(o_ref.dtype)


def _in_proj(x, pos, norm_g, mod, mod_row, w, segs, tm):
    bsz, seq, d = x.shape
    add_pos = pos is not None
    in_specs = [pl.BlockSpec((1, tm, d), lambda b, i: (b, i, 0))]
    args = [x]
    if add_pos:
        in_specs.append(pl.BlockSpec((tm, d), lambda b, i: (i, 0)))
        args.append(pos)
    in_specs += [pl.BlockSpec((1, d), lambda b, i: (0, 0)),
                 pl.BlockSpec((1, 1, d), lambda b, i: (mod_row(b), 0, 0)),
                 pl.BlockSpec((1, 1, d), lambda b, i: (mod_row(b), 0, 1)),
                 _const_spec(w.shape)]
    args += [norm_g.reshape(1, d), mod, mod, w]
    out_shape = [jax.ShapeDtypeStruct((bsz, seq, b - a), dt) for a, b, dt in segs]
    out_specs = [pl.BlockSpec((1, tm, b - a), lambda bb, i: (bb, i, 0)) for a, b, _ in segs]
    return pl.pallas_call(
        functools.partial(_inproj_kernel, add_pos=add_pos, segs=tuple(segs)),
        out_shape=out_shape, grid=(bsz, seq // tm), in_specs=in_specs, out_specs=out_specs,
        compiler_params=_cparams(("parallel", "parallel")),
        name="in_proj",
    )(*args)


def _cumsum_rows(x, reverse):
    c = x.shape[0]
    row = lax.broadcasted_iota(jnp.int32, x.shape, 0)
    sh = 1
    while sh < c:
        if reverse:
            x = x + jnp.where(row < c - sh, pltpu.roll(x, c - sh, 0), 0.0)
        else:
            x = x + jnp.where(row >= sh, pltpu.roll(x, sh, 0), 0.0)
        sh *= 2
    return x


def _scan_chunk_fast(q, lf2, k, v, st_ref, h, reverse):
    c = SCAN_CHUNK
    s = SCAN_SUB
    nsub = c // s
    b = _cumsum_rows(lf2, reverse)
    qs = _silu(q) * (HG_HEAD_DIM ** -0.5)
    st = st_ref[h]
    o_inter = _dot_nt((qs * jnp.exp2(b)).astype(BF16), st.astype(BF16))
    v_bf = v.astype(BF16)
    zero_row = jnp.zeros((1, HG_HEAD_DIM), F32)
    zero_blk = jnp.zeros((s, HG_HEAD_DIM), BF16)
    kt = {}
    prev_ref = None
    sc = [None] * nsub
    for i in (range(nsub - 1, -1, -1) if reverse else range(nsub)):
        r0 = i * s
        bi = b[r0:r0 + s]
        if reverse:
            ref = b[r0 + s:r0 + s + 1] if i < nsub - 1 else zero_row
        else:
            ref = b[r0 - 1:r0] if i > 0 else zero_row
        if prev_ref is not None:
            step = jnp.exp2(ref - prev_ref)
            kt = {j: blk * step for j, blk in kt.items()}
        kt[i] = k[r0:r0 + s] * jnp.exp2(ref - bi)
        prev_ref = ref
        qi = (qs[r0:r0 + s] * jnp.exp2(bi - ref)).astype(BF16)
        keys = jnp.concatenate([kt[j].astype(BF16) if j in kt else zero_blk for j in range(nsub)], axis=0)
        sc[i] = _dot_nt(qi, keys)
    scores = jnp.concatenate(sc, axis=0)
    rr = lax.broadcasted_iota(jnp.int32, (c, c), 0)
    cc = lax.broadcasted_iota(jnp.int32, (c, c), 1)
    scores = jnp.where((cc >= rr) if reverse else (cc <= rr), scores, 0.0)
    o = o_inter + _dot(scores.astype(BF16), v_bf)

    b_end = b[0:1] if reverse else b[c - 1:c]
    kd = (k * jnp.exp2(b_end - b)).astype(BF16)
    st_ref[h] = st * jnp.exp2(b_end) + _dot_tn(v_bf, kd)
    return o


def _scan_chunk_exact(q, lf2, k, v, st_ref, h, b_scr, q_scr, reverse):
    c = SCAN_CHUNK
    s = SCAN_SUB
    nsub = c // s
    b = _cumsum_rows(lf2, reverse)
    qs = _silu(q) * (HG_HEAD_DIM ** -0.5)
    st = st_ref[h]
    o_inter = _dot_nt((qs * jnp.exp2(b)).astype(BF16), st.astype(BF16))

    b_scr[...] = b
    q_scr[...] = qs
    v_bf = v.astype(BF16)
    ones = jnp.ones((HG_HEAD_DIM, HG_HEAD_DIM), BF16)
    sub_iota = lax.broadcasted_iota(jnp.int32, (s, HG_HEAD_DIM), 0)
    row_iota = lax.broadcasted_iota(jnp.int32, (c, HG_HEAD_DIM), 0)
    grp = (lax.broadcasted_iota(jnp.int32, (s, s * s), 1) // s ==
           lax.broadcasted_iota(jnp.int32, (s, s * s), 0)).astype(BF16)
    o_parts = []
    for i in range(nsub):
        r0 = i * s
        bi = b[r0:r0 + s]
        ki = k[r0:r0 + s]
        vi = v[r0:r0 + s]
        prods = []
        for t in range(s):
            bt = b_scr[pl.ds(r0 + t, 1), :]
            qt = q_scr[pl.ds(r0 + t, 1), :]
            keep = (sub_iota >= t) if reverse else (sub_iota <= t)
            e = jnp.exp2(jnp.where(keep, bt - bi, NEG_BIG))
            prods.append(((qt * ki) * e).astype(BF16))
        p = jnp.concatenate(prods, axis=0)
        rsum = _dot(p, ones)
        zt = (rsum * jnp.concatenate([vi] * s, axis=0)).astype(BF16)
        o_i = _dot(grp, zt)
        if reverse and i < nsub - 1:
            ref_row = b[r0 + s:r0 + s + 1]
            key_rows = row_iota >= r0 + s
        elif (not reverse) and i > 0:
            ref_row = b[r0 - 1:r0]
            key_rows = row_iota < r0
        else:
            ref_row = None
        if ref_row is not None:
            qi = qs[r0:r0 + s] * jnp.exp2(bi - ref_row)
            ks = k * jnp.exp2(jnp.where(key_rows, ref_row - b, NEG_BIG))
            sc = _dot_nt(qi.astype(BF16), ks.astype(BF16))
            o_i = o_i + _dot(sc.astype(BF16), v_bf)
        o_parts.append(o_i)
    o = o_inter + jnp.concatenate(o_parts, axis=0)

    b_end = b[0:1] if reverse else b[c - 1:c]
    kd = (k * jnp.exp2(b_end - b)).astype(BF16)
    st_ref[h] = st * jnp.exp2(b_end) + _dot_tn(v_bf, kd)
    return o


def _scan_kernel(qf_ref, qb_ref, ff_ref, fb_ref, vf_ref, vb_ref, lb_ref, s0f_ref, s0b_ref,
                 of_ref, ob_ref, sf_ref, sb_ref, stf, stb, lf_scr, k_scr, bf_scr, qf_scr, bb_scr, qb_scr,
                 *, n_heads, n_chunks):
    n = pl.program_id(1)

    @pl.when(n == 0)
    def _():
        stf[...] = s0f_ref[0]
        stb[...] = s0b_ref[0]

    tb = n_chunks * SCAN_CHUNK
    nblk = tb // SCAN_SUB
    sel = (lax.broadcasted_iota(jnp.int32, (nblk, tb), 1) // SCAN_SUB ==
           lax.broadcasted_iota(jnp.int32, (nblk, tb), 0)).astype(BF16)
    worst = jnp.zeros((nblk, lb_ref.shape[1]), F32)
    for d, f_ref in enumerate((ff_ref, fb_ref)):
        lb = lb_ref[d:d + 1, :]
        forget = lb + (1.0 - lb) * jax.nn.sigmoid(f_ref[0])
        lf2 = jnp.log(forget) * LOG2_E
        lf_scr[d] = lf2
        k_scr[d] = 1.0 - forget
        worst = jnp.maximum(worst, -_dot(sel, lf2.astype(BF16)))
    safe = jnp.max(worst) <= SCAN_SAFE_LOG2_RANGE

    def rows_of(ci):
        rf = pl.ds(pl.multiple_of(ci * SCAN_CHUNK, SCAN_CHUNK), SCAN_CHUNK)
        rb = pl.ds(pl.multiple_of((n_chunks - 1 - ci) * SCAN_CHUNK, SCAN_CHUNK), SCAN_CHUNK)
        return rf, rb

    @pl.when(safe)
    def _():
        def chunk_body(ci, carry):
            rf, rb = rows_of(ci)
            for h in range(n_heads):
                cols = slice(h * HG_HEAD_DIM, (h + 1) * HG_HEAD_DIM)
                of_ref[0, rf, cols] = _scan_chunk_fast(qf_ref[0, rf, cols], lf_scr[0, rf, cols],
                                                       k_scr[0, rf, cols], vf_ref[0, rf, cols], stf, h, False)
                ob_ref[0, rb, cols] = _scan_chunk_fast(qb_ref[0, rb, cols], lf_scr[1, rb, cols],
                                                       k_scr[1, rb, cols], vb_ref[0, rb, cols], stb, h, True)
            return carry

        lax.fori_loop(0, n_chunks, chunk_body, 0)

    @pl.when(jnp.logical_not(safe))
    def _():
        def head_body(h, carry):
            cols = pl.ds(pl.multiple_of(h * HG_HEAD_DIM, HG_HEAD_DIM), HG_HEAD_DIM)

            def chunk_body(ci, carry2):
                rf, rb = rows_of(ci)
                of_ref[0, rf, cols] = _scan_chunk_exact(qf_ref[0, rf, cols], lf_scr[0, rf, cols],
                                                        k_scr[0, rf, cols], vf_ref[0, rf, cols], stf, h,
                                                        bf_scr, qf_scr, False)
                ob_ref[0, rb, cols] = _scan_chunk_exact(qb_ref[0, rb, cols], lf_scr[1, rb, cols],
                                                        k_scr[1, rb, cols], vb_ref[0, rb, cols], stb, h,
                                                        bb_scr, qb_scr, True)
                return carry2

            return lax.fori_loop(0, n_chunks, chunk_body, carry)

        lax.fori_loop(0, n_heads, head_body, 0)

    @pl.when(n == pl.num_programs(1) - 1)
    def _():
        sf_ref[0] = stf[...]
        sb_ref[0] = stb[...]


def _hgrn2_scan(z_hg, lb, s0f, s0b, tb):
    bsz, seq, w5 = z_hg.shape
    w = w5 // 5
    nh = w // HG_HEAD_DIM
    nb = seq // tb
    blk = (1, tb, w)
    fwd = lambda j: pl.BlockSpec(blk, lambda b, n: (b, n, j))
    bwd = lambda j: pl.BlockSpec(blk, lambda b, n: (b, nb - 1 - n, j))
    st_spec = pl.BlockSpec((1, nh, HG_HEAD_DIM, HG_HEAD_DIM), lambda b, n: (b, 0, 0, 0))
    st_shape = jax.ShapeDtypeStruct((bsz, nh, HG_HEAD_DIM, HG_HEAD_DIM), F32)
    o_shape = jax.ShapeDtypeStruct((bsz, seq, w), F32)
    return pl.pallas_call(
        functools.partial(_scan_kernel, n_heads=nh, n_chunks=tb // SCAN_CHUNK),
        out_shape=[o_shape, o_shape, st_shape, st_shape],
        grid=(bsz, nb),
        in_specs=[fwd(0), bwd(0), fwd(1), bwd(2), fwd(3), bwd(3),
                  pl.BlockSpec((2, w), lambda b, n: (0, 0)), st_spec, st_spec],
        out_specs=[fwd(0), bwd(0), st_spec, st_spec],
        scratch_shapes=[pltpu.VMEM((nh, HG_HEAD_DIM, HG_HEAD_DIM), F32),
                        pltpu.VMEM((nh, HG_HEAD_DIM, HG_HEAD_DIM), F32),
                        pltpu.VMEM((2, tb, w), F32),
                        pltpu.VMEM((2, tb, w), F32),
                        pltpu.VMEM((SCAN_CHUNK, HG_HEAD_DIM), F32),
                        pltpu.VMEM((SCAN_CHUNK, HG_HEAD_DIM), F32),
                        pltpu.VMEM((SCAN_CHUNK, HG_HEAD_DIM), F32),
                        pltpu.VMEM((SCAN_CHUNK, HG_HEAD_DIM), F32)],
        compiler_params=_cparams(("parallel", "arbitrary")),
        name="hgrn2_scan",
    )(z_hg, z_hg, z_hg, z_hg, z_hg, z_hg, lb, s0f, s0b)


def _dft_mats(n_rows, period):
    t = jnp.arange(n_rows, dtype=jnp.int32)

    def table(mult, count):
        m = (jnp.arange(count, dtype=jnp.int32)[:, None] * mult * t[None, :]) % period
        ang = m.astype(F32) * (2.0 * math.pi / period)
        return jnp.cos(ang), jnp.sin(ang)

    n_hi = n_rows // DFT_SPLIT
    ca, sa = table(DFT_SPLIT, n_hi)
    cb, sb = table(1, DFT_SPLIT)
    cmat = ca[:, None, :] * cb[None, :, :] - sa[:, None, :] * sb[None, :, :]
    smat = sa[:, None, :] * cb[None, :, :] + ca[:, None, :] * sb[None, :, :]
    return cmat.reshape(n_rows, n_rows).astype(BF16), smat.reshape(n_rows, n_rows).astype(BF16)


def _split_bf16(x):
    hi = x.astype(BF16)
    lo = (x - hi.astype(F32)).astype(BF16)
    return hi, lo


def _filter_spectrum_kernel(hs_ref, hd_ref, c_ref, s_ref, kr_ref, ki_ref, kn_ref, *, scale):
    hs = hs_ref[...]
    hd = hd_ref[...]
    hs_hi, hs_lo = _split_bf16(hs)
    hd_hi, hd_lo = _split_bf16(hd)
    c = c_ref[...]
    s = s_ref[...]
    kr_ref[...] = ((_dot(c, hs_hi) + _dot(c, hs_lo)) * scale).astype(kr_ref.dtype)
    ki_ref[...] = ((_dot(s, hd_hi) + _dot(s, hd_lo)) * (-scale)).astype(ki_ref.dtype)
    row = lax.broadcasted_iota(jnp.int32, hs.shape, 0)
    sign = jnp.where(row % 2 == 0, 1.0, -1.0)
    kn = jnp.sum(hs * sign, axis=0, keepdims=True) * scale
    kn_ref[...] = jnp.broadcast_to(kn, kn_ref.shape)


def _filter_spectrum(hs, hd, cmat, smat):
    seq, ch = hs.shape
    scale = 2.0 / (2 * seq)
    return pl.pallas_call(
        functools.partial(_filter_spectrum_kernel, scale=scale),
        out_shape=[jax.ShapeDtypeStruct((seq, ch), BF16), jax.ShapeDtypeStruct((seq, ch), BF16),
                   jax.ShapeDtypeStruct((8, ch), F32)],
        compiler_params=pltpu.CompilerParams(vmem_limit_bytes=V7X_VMEM_LIMIT_BYTES),
        name="hyena_filter_spectrum",
    )(hs, hd, cmat, smat)


def _hyena_kernel(z_ref, cw_ref, cb_ref, c_ref, s_ref, kr_ref, ki_ref, kn_ref, g_ref, o_ref, *, seq, ch):
    row = lax.broadcasted_iota(jnp.int32, (seq, 1), 0)
    sign = jnp.where(row % 2 == 0, 1.0, -1.0)

    def short_conv(part):
        cols = slice(part * ch, (part + 1) * ch)
        z = z_ref[0, :, cols].astype(F32)
        z_prev = jnp.where(row >= 1, pltpu.roll(z, 1, 0), 0.0)
        z_next = jnp.where(row <= seq - 2, pltpu.roll(z, seq - 1, 0), 0.0)
        return z_prev * cw_ref[0:1, cols] + z * cw_ref[1:2, cols] + z_next * cw_ref[2:3, cols] + cb_ref[:, cols]

    def long_conv(x, order):
        cols = slice(order * ch, (order + 1) * ch)
        xb = x.astype(BF16)
        a = _dot(c_ref[...], xb)
        bm = _dot(s_ref[...], xb)
        kr = kr_ref[:, cols]
        ki = ki_ref[:, cols]
        yr = a * kr + bm * ki
        yi = (a * ki - bm * kr).astype(BF16)
        dc = 0.5 * yr[0:1, :]
        x_nyq = jnp.sum(x * sign, axis=0, keepdims=True)
        y = _dot(c_ref[...], yr.astype(BF16)) - _dot(s_ref[...], yi)
        return y - dc + (0.5 * x_nyq * kn_ref[0:1, cols]) * sign

    y = short_conv(1) * long_conv(short_conv(0), 0)
    y = short_conv(2) * long_conv(y, 1)
    ms = jnp.mean(y * y, axis=-1, keepdims=True)
    o_ref[0] = y * lax.rsqrt(ms + RMS_EPS) * g_ref[...]


def _hyena(z_hy, conv_w, conv_b, cmat, smat, kr, ki, kn, norm_g):
    bsz, seq, c3 = z_hy.shape
    ch = c3 // 3
    return pl.pallas_call(
        functools.partial(_hyena_kernel, seq=seq, ch=ch),
        out_shape=jax.ShapeDtypeStruct((bsz, seq, ch), F32),
        grid=(bsz,),
        in_specs=[pl.BlockSpec((1, seq, c3), lambda b: (b, 0, 0)),
                  _const_spec(conv_w.shape), _const_spec((1, c3)),
                  _const_spec(cmat.shape), _const_spec(smat.shape),
                  _const_spec(kr.shape), _const_spec(ki.shape), _const_spec(kn.shape),
                  _const_spec((1, ch))],
        out_specs=pl.BlockSpec((1, seq, ch), lambda b: (b, 0, 0)),
        compiler_params=_cparams(("parallel",)),
        name="hyena_mixer",
    )(z_hy, conv_w, conv_b.reshape(1, c3), cmat, smat, kr, ki, kn, norm_g.reshape(1, ch))


def _fnet_kernel(z_ref, c_ref, s_ref, bdc_ref, bds_ref, bdw_ref, b_ref, g_ref, o_ref, *, scale):
    zb = z_ref[0].astype(BF16)
    y1 = _dot(c_ref[...], zb)
    y2 = _dot(s_ref[...], zb)
    r = (_dot(y1.astype(BF16), bdc_ref[...]) - _dot(y2.astype(BF16), bds_ref[...])) * scale
    y = _dot(r.astype(BF16), bdw_ref[...]) + b_ref[...]
    ms = jnp.mean(y * y, axis=-1, keepdims=True)
    o_ref[0] = y * lax.rsqrt(ms + RMS_EPS) * g_ref[...]


def _block_diag(blocks):
    g, a, b = blocks.shape
    out = jnp.zeros((g * a, g * b), blocks.dtype)
    for i in range(g):
        out = out.at[i * a:(i + 1) * a, i * b:(i + 1) * b].set(blocks[i])
    return out


def _fnet(z_fn, cmat, smat, fn_w, fn_b, norm_g):
    bsz, seq, ch = z_fn.shape
    gd = ch // FN_GROUPS
    k = np.arange(gd)
    ang = 2.0 * np.pi * ((k[:, None] * k[None, :]) % gd) / gd
    eye = np.eye(FN_GROUPS)
    bdc = jnp.asarray(np.kron(eye, np.cos(ang)), BF16)
    bds = jnp.asarray(np.kron(eye, np.sin(ang)), BF16)
    bdw = _block_diag(fn_w).astype(BF16)
    scale = 1.0 / math.sqrt(seq * gd)
    return pl.pallas_call(
        functools.partial(_fnet_kernel, scale=scale),
        out_shape=jax.ShapeDtypeStruct((bsz, seq, ch), F32),
        grid=(bsz,),
        in_specs=[pl.BlockSpec((1, seq, ch), lambda b: (b, 0, 0)),
                  _const_spec(cmat.shape), _const_spec(smat.shape),
                  _const_spec((ch, ch)), _const_spec((ch, ch)), _const_spec((ch, ch)),
                  _const_spec((1, ch)), _const_spec((1, ch))],
        out_specs=pl.BlockSpec((1, seq, ch), lambda b: (b, 0, 0)),
        compiler_params=_cparams(("parallel",)),
        name="fnet_mixer",
    )(z_fn, cmat, smat, bdc, bds, bdw, fn_b.reshape(1, ch), norm_g.reshape(1, ch))


def _route(logits_t, rb_ref):
    rows = [logits_t[e:e + 1, :] for e in range(N_EXPERTS)]
    mx = functools.reduce(jnp.maximum, rows)
    ex = [jnp.exp(r - mx) for r in rows]
    inv = 1.0 / functools.reduce(lambda a, b: a + b, ex)
    probs = [e * inv for e in ex]
    sel = [probs[e] + rb_ref[e:e + 1, 0:1] for e in range(N_EXPERTS)]
    epg = EXPERTS_PER_GROUP
    gscore = []
    for g in range(N_GROUPS):
        s = sel[g * epg:(g + 1) * epg]
        pairs = [s[i] + s[j] for i in range(epg) for j in range(i + 1, epg)]
        gscore.append(functools.reduce(jnp.maximum, pairs))
    best = gscore[0]
    best_g = jnp.zeros_like(best, dtype=jnp.int32)
    for g in range(1, N_GROUPS):
        better = gscore[g] > best
        best_g = jnp.where(better, g, best_g)
        best = jnp.where(better, gscore[g], best)
    chosen = []
    for e in range(N_EXPERTS):
        g = e // epg
        beaten = jnp.zeros_like(best_g)
        for e2 in range(g * epg, (g + 1) * epg):
            if e2 == e:
                continue
            wins = (sel[e2] > sel[e]) | ((sel[e2] == sel[e]) & (e2 < e))
            beaten = beaten + wins.astype(jnp.int32)
        chosen.append(jnp.where((best_g == g) & (beaten < 2), probs[e], 0.0))
    tot = functools.reduce(lambda a, b: a + b, chosen)
    inv_tot = 1.0 / tot
    return jnp.concatenate([c * inv_tot for c in chosen], axis=0)


def _outproj_kernel(*refs, add_pos, n_heads):
    if add_pos:
        x_ref, pos_ref = refs[:2]
        rest = refs[2:]
        x = x_ref[0] + pos_ref[...]
    else:
        x_ref = refs[0]
        rest = refs[1:]
        x = x_ref[0]
    (yhy_ref, of_ref, ob_ref, g_ref, yfn_ref, wout_ref, hgg_ref, gate_ref, n2g_ref, sh_ref, sc_ref,
     rwt_ref, rb_ref, xo_ref, h_ref, comb_ref) = rest
    o = of_ref[0] + ob_ref[0]
    parts = []
    for h in range(n_heads):
        oh = o[:, h * HG_HEAD_DIM:(h + 1) * HG_HEAD_DIM]
        parts.append(oh * lax.rsqrt(jnp.mean(oh * oh, axis=-1, keepdims=True) + RMS_EPS))
    y_hg = jnp.concatenate(parts, axis=-1) * hgg_ref[...] * _silu(g_ref[0])
    c_hy = yhy_ref.shape[-1]
    c_hg = y_hg.shape[-1]
    mix = (_dot(yhy_ref[0].astype(BF16), wout_ref[0:c_hy, :]) +
           _dot(y_hg.astype(BF16), wout_ref[c_hy:c_hy + c_hg, :]) +
           _dot(yfn_ref[0].astype(BF16), wout_ref[c_hy + c_hg:, :]))
    xn = x + gate_ref[0] * mix
    xo_ref[0] = xn
    ms = jnp.mean(xn * xn, axis=-1, keepdims=True)
    h2 = xn * lax.rsqrt(ms + RMS_EPS) * n2g_ref[...]
    h2 = h2 * (1.0 + sc_ref[0]) + sh_ref[0]
    h_ref[0] = h2.astype(BF16)
    logits_t = _dot_nt(rwt_ref[...], h2, precision=HIGHEST)
    comb_ref[0] = _route(logits_t, rb_ref)


def _out_proj(x, pos, y_hy, o_f, o_b, z_hg, y_fn, w_out, hg_norm_g, mod, mod_row, norm2_g, rwt, rb, tm):
    bsz, seq, d = x.shape
    add_pos = pos is not None
    c_hy, c_hg, c_fn = y_hy.shape[-1], o_f.shape[-1], y_fn.shape[-1]
    tok = lambda c, j=0: pl.BlockSpec((1, tm, c), lambda b, i: (b, i, j))
    modk = lambda k: pl.BlockSpec((1, 1, d), lambda b, i: (mod_row(b), 0, k))
    in_specs = [tok(d)]
    args = [x]
    if add_pos:
        in_specs.append(pl.BlockSpec((tm, d), lambda b, i: (i, 0)))
        args.append(pos)
    in_specs += [tok(c_hy), tok(c_hg), tok(c_hg), tok(c_hg, 4), tok(c_fn),
                 _const_spec(w_out.shape), _const_spec((1, c_hg)), modk(2), _const_spec((1, d)),
                 modk(3), modk(4), _const_spec(rwt.shape), _const_spec(rb.shape)]
    args += [y_hy, o_f, o_b, z_hg, y_fn, w_out, hg_norm_g.reshape(1, c_hg), mod, norm2_g.reshape(1, d),
             mod, mod, rwt, rb]
    nt = seq // tm
    out_shape = [jax.ShapeDtypeStruct((bsz, seq, d), F32), jax.ShapeDtypeStruct((bsz, seq, d), BF16),
                 jax.ShapeDtypeStruct((bsz * nt, N_EXPERTS, tm), F32)]
    out_specs = [tok(d), tok(d), pl.BlockSpec((1, N_EXPERTS, tm), lambda b, i: (b * nt + i, 0, 0))]
    return pl.pallas_call(
        functools.partial(_outproj_kernel, add_pos=add_pos, n_heads=c_hg // HG_HEAD_DIM),
        out_shape=out_shape, grid=(bsz, nt), in_specs=in_specs, out_specs=out_specs,
        compiler_params=_cparams(("parallel", "parallel")),
        name="out_proj_router",
    )(*args)


def _moe_kernel(*refs, final, cap):
    if final:
        (cnt_ref, h_ref, comb_ref, combt_ref, wg_ref, wu_ref, wd_ref, x_ref, gate_ref, fg_ref, o_ref,
         acc_ref, rcol_ref, rrow_ref) = refs
    else:
        (cnt_ref, h_ref, comb_ref, combt_ref, wg_ref, wu_ref, wd_ref, x_ref, gate_ref, o_ref,
         acc_ref, rcol_ref, rrow_ref) = refs
    e = pl.program_id(1)
    tm = h_ref.shape[0]
    n_rt = combt_ref.shape[0]

    @pl.when(e == 0)
    def _():
        acc_ref[...] = jnp.zeros_like(acc_ref)
        ti = lax.broadcasted_iota(jnp.int32, (tm, tm), 0)
        tj = lax.broadcasted_iota(jnp.int32, (tm, tm), 1)
        before_col = jnp.where(tj < ti, 1.0, 0.0).astype(BF16)
        before_row = jnp.where(ti < tj, 1.0, 0.0).astype(BF16)
        sel_col = jnp.where(comb_ref[...] > 0.0, 1.0, 0.0).astype(BF16)
        combt = jnp.concatenate([combt_ref[j] for j in range(n_rt)], axis=1)
        sel_row = jnp.where(combt > 0.0, 1.0, 0.0).astype(BF16)
        rcol_ref[...] = _dot(before_col, sel_col)
        rrow_ref[...] = _dot(sel_row, before_row)

    comb = comb_ref[...]
    lane = lax.broadcasted_iota(jnp.int32, comb.shape, 1)
    ce = jnp.sum(jnp.where(lane == e, comb, 0.0), axis=-1, keepdims=True)
    fits = cnt_ref[pl.program_id(0), e] <= cap

    @pl.when(fits)
    def _():
        h = h_ref[...]
        ce_row = jnp.concatenate([combt_ref[j, pl.ds(e, 1), :] for j in range(n_rt)], axis=1)
        rank_col = jnp.sum(jnp.where(lane == e, rcol_ref[...], 0.0), axis=-1, keepdims=True)
        rank_row = rrow_ref[pl.ds(e, 1), :]
        slot_r = lax.broadcasted_iota(jnp.int32, (cap, tm), 0).astype(F32)
        pick = jnp.where((slot_r == rank_row) & (ce_row > 0.0), 1.0, 0.0).astype(BF16)
        hc = _dot(pick, h).astype(BF16)
        a = _silu(_dot(hc, wg_ref[0])) * _dot(hc, wu_ref[0])
        y = _dot(a.astype(BF16), wd_ref[0]).astype(BF16)
        slot_c = lax.broadcasted_iota(jnp.int32, (tm, cap), 1).astype(F32)
        spread = jnp.where((slot_c == rank_col) & (ce > 0.0), 1.0, 0.0).astype(BF16)
        acc_ref[...] += ce * _dot(spread, y)

    @pl.when(jnp.logical_not(fits))
    def _():
        h = h_ref[...]
        a = _silu(_dot(h, wg_ref[0])) * _dot(h, wu_ref[0])
        acc_ref[...] += ce * _dot(a.astype(BF16), wd_ref[0])

    @pl.when(e == pl.num_programs(1) - 1)
    def _():
        y = x_ref[...] + gate_ref[0] * acc_ref[...]
        if final:
            ms = jnp.mean(y * y, axis=-1, keepdims=True)
            y = y * lax.rsqrt(ms + RMS_EPS) * fg_ref[...]
        o_ref[...] = y


def _moe(h, comb_t, wg, wu, wd, x, mod, mod_row_of_tile, final_g, tm):
    n, d = h.shape
    ne, _, de = wg.shape
    final = final_g is not None
    tr = comb_t.shape[-1]
    n_rt = tm // tr
    lanes = 128
    comb = jnp.transpose(comb_t, (0, 2, 1)).reshape(n, ne)
    comb = jnp.pad(comb, ((0, 0), (0, lanes - ne)))
    cap = min(MOE_CAP, tm)
    counts = jnp.sum((comb_t > 0.0).reshape(n // tm, n_rt, ne, tr), axis=(1, 3)).astype(jnp.int32)
    in_specs = [pl.BlockSpec((tm, d), lambda i, e, c: (i, 0)),
                pl.BlockSpec((tm, lanes), lambda i, e, c: (i, 0)),
                pl.BlockSpec((n_rt, ne, tr), lambda i, e, c: (i, 0, 0)),
                pl.BlockSpec((1, d, de), lambda i, e, c: (e, 0, 0)),
                pl.BlockSpec((1, d, de), lambda i, e, c: (e, 0, 0)),
                pl.BlockSpec((1, de, d), lambda i, e, c: (e, 0, 0)),
                pl.BlockSpec((tm, d), lambda i, e, c: (i, 0)),
                pl.BlockSpec((1, 1, d), lambda i, e, c: (mod_row_of_tile(i), 0, 5))]
    args = [counts, h, comb, comb_t, wg, wu, wd, x, mod]
    if final:
        in_specs.append(pl.BlockSpec((1, d), lambda i, e, c: (0, 0)))
        args.append(final_g.reshape(1, d))
    return pl.pallas_call(
        functools.partial(_moe_kernel, final=final, cap=cap),
        out_shape=jax.ShapeDtypeStruct((n, d), F32),
        grid_spec=pltpu.PrefetchScalarGridSpec(
            num_scalar_prefetch=1, grid=(n // tm, ne), in_specs=in_specs,
            out_specs=pl.BlockSpec((tm, d), lambda i, e, c: (i, 0)),
            scratch_shapes=[pltpu.VMEM((tm, d), F32), pltpu.VMEM((tm, lanes), F32),
                            pltpu.VMEM((ne, tm), F32)]),
        compiler_params=_cparams(("parallel", "arbitrary")),
        name="moe_experts",
    )(*args)


def _grid_sincos(n_tok, d):
    rows = n_tok // GRID_W
    row = jnp.repeat(jnp.arange(rows, dtype=F32), GRID_W)
    col = jnp.tile(jnp.arange(GRID_W, dtype=F32), rows)
    quarter = d // 4
    omega = 1.0 / (POS_BASE ** (jnp.arange(quarter, dtype=F32) / quarter))
    ar = row[:, None] * omega
    ac = col[:, None] * omega
    return jnp.concatenate([jnp.sin(ar), jnp.cos(ar), jnp.sin(ac), jnp.cos(ac)], axis=-1)


def _hyena_filters(seq, ch, w1, b1, w2, b2, w3, freq):
    t = jnp.linspace(0.0, 1.0, seq, dtype=F32)[:, None]
    w = 2.0 * math.pi * jnp.arange(seq, dtype=F32)[:, None] / seq
    bands = jnp.linspace(1e-4, HY_BANDS - 1, HY_BANDS, dtype=F32)[None, :]
    z = jnp.concatenate([t, jnp.cos(bands * w), -jnp.sin(bands * w)], axis=-1)
    hp = functools.partial(jnp.dot, precision=HIGHEST)
    h = jnp.sin(freq * (hp(z, w1) + b1))
    h = jnp.sin(freq * (hp(h, w2) + b2))
    h = hp(h, w3).reshape(seq, 2 * HY_ORDER, ch)
    max_decay = math.log(HY_DECAY_TARGET) / HY_FAST_DECAY_PCT
    min_decay = math.log(HY_DECAY_TARGET) / HY_SLOW_DECAY_PCT
    deltas = jnp.linspace(min_decay, max_decay, ch, dtype=F32)
    return h * jnp.exp(-t[:, :, None] * jnp.abs(deltas))


def _filter_halves(h, bias):
    seq = h.shape[0]
    not0 = (jnp.arange(seq) > 0)[:, None].astype(F32)
    hs, hd = [], []
    for o in range(HY_ORDER):
        hf = h[:, 2 * o].at[0].add(bias[o])
        hb = h[:, 2 * o + 1] * not0
        hs.append(hf + hb)
        hd.append(hf - hb)
    return jnp.concatenate(hs, axis=-1), jnp.concatenate(hd, axis=-1)


def kernel(x, c, ctx, c_ctx, ada_w, ada_b, norm1_g, norm2_g, w_in, w_out, hy_conv_w, hy_conv_b, hy_filt_w1, hy_filt_b1, hy_filt_w2, hy_filt_b2, hy_filt_w3, hy_filt_freq, hy_bias, hy_norm_g, hg_lower_bounds, hg_norm_g, fn_w, fn_b, fn_norm_g, router_w, router_b, moe_w_gate, moe_w_up, moe_w_down, final_norm_g):
    bsz, seq, d = x.shape
    n_ctx = ctx.shape[1]
    depth = ada_w.shape[0]
    c_hy = hy_norm_g.shape[-1]
    c_hyp = hy_conv_w.shape[-1]
    w_hg = hg_norm_g.shape[-1]
    c_fn = fn_norm_g.shape[-1]
    hg_lo, hg_hi = c_hyp, c_hyp + 5 * w_hg
    nh = w_hg // HG_HEAD_DIM

    cs = jnp.cumsum(jax.nn.softmax(hg_lower_bounds.astype(F32), axis=0), axis=0)
    lower_bounds = cs - cs[0:1]
    pos = _grid_sincos(seq, d)

    n_rows = -(-(bsz + 1) // 8) * 8
    cond = jnp.zeros((n_rows, d), F32).at[:bsz].set(c).at[bsz].set(c_ctx)
    mods = _ada_modulation(cond, ada_w, ada_b)
    x_row = lambda b: b
    c_row = lambda b: bsz

    dft = {}
    for n in (seq, n_ctx):
        dft[("hy", n)] = _dft_mats(n, 2 * n)
        dft[("fn", n)] = _dft_mats(n, n)

    rwt = jnp.transpose(router_w).astype(F32)
    rb = jnp.broadcast_to(router_b.astype(F32)[:, None], (N_EXPERTS, 128))
    s_zero = jnp.zeros((bsz, nh, HG_HEAD_DIM, HG_HEAD_DIM), F32)

    tm_x = min(512, seq)
    tm_c = min(256, n_ctx)
    tb_x = min(256, seq)
    tb_c = min(256, n_ctx)
    segs = [(hg_lo, hg_hi, F32), (0, c_hyp, BF16), (hg_hi, hg_hi + c_fn, BF16)]

    def mixer(z_hy, z_fn, n, layer):
        cmat, smat = dft[("hy", n)]
        h = _hyena_filters(n, c_hy, hy_filt_w1[layer], hy_filt_b1[layer], hy_filt_w2[layer],
                           hy_filt_b2[layer], hy_filt_w3[layer], hy_filt_freq[layer])
        hs, hd = _filter_halves(h, hy_bias[layer])
        kr, ki, kn = _filter_spectrum(hs, hd, cmat, smat)
        y_hy = _hyena(z_hy, hy_conv_w[layer], hy_conv_b[layer], cmat, smat, kr, ki, kn, hy_norm_g[layer])
        fc, fs = dft[("fn", n)]
        y_fn = _fnet(z_fn, fc, fs, fn_w[layer], fn_b[layer], fn_norm_g[layer])
        return y_hy, y_fn

    for layer in range(depth):
        last = layer == depth - 1
        mod = mods[layer].reshape(n_rows, 1, N_MOD * d)
        w_in_b = w_in[layer].astype(BF16)
        w_out_b = w_out[layer].astype(BF16)
        wg = moe_w_gate[layer].astype(BF16)
        wu = moe_w_up[layer].astype(BF16)
        wd = moe_w_down[layer].astype(BF16)
        lb = lower_bounds[layer]
        x_pos = pos if layer == 0 else None

        if last:
            (zc_hg,) = _in_proj(ctx, None, norm1_g[layer], mod, c_row, w_in_b[:, hg_lo:hg_hi],
                                [(0, 5 * w_hg, F32)], tm_c)
        else:
            zc_hg, zc_hy, zc_fn = _in_proj(ctx, None, norm1_g[layer], mod, c_row, w_in_b, segs, tm_c)
        oc_f, oc_b, s_f, s_b = _hgrn2_scan(zc_hg, lb, s_zero, s_zero, tb_c)

        zx_hg, zx_hy, zx_fn = _in_proj(x, x_pos, norm1_g[layer], mod, x_row, w_in_b, segs, tm_x)
        ox_f, ox_b, _, _ = _hgrn2_scan(zx_hg, lb, s_f, s_b, tb_x)
        yx_hy, yx_fn = mixer(zx_hy, zx_fn, seq, layer)
        x_mid, hx2, comb_x = _out_proj(x, x_pos, yx_hy, ox_f, ox_b, zx_hg, yx_fn, w_out_b, hg_norm_g[layer],
                                       mod, x_row, norm2_g[layer], rwt, rb, tm_x)
        tm_moe = min(1024, seq)
        tiles_per_seq = seq // tm_moe
        x = _moe(hx2.reshape(bsz * seq, d), comb_x, wg, wu, wd, x_mid.reshape(bsz * seq, d), mod,
                 lambda i: i // tiles_per_seq, final_norm_g if last else None, tm_moe).reshape(bsz, seq, d)

        if not last:
            yc_hy, yc_fn = mixer(zc_hy, zc_fn, n_ctx, layer)
            c_mid, hc2, comb_c = _out_proj(ctx, None, yc_hy, oc_f, oc_b, zc_hg, yc_fn, w_out_b,
                                           hg_norm_g[layer], mod, c_row, norm2_g[layer], rwt, rb, tm_c)
            tm_mc = min(1024, bsz * n_ctx)
            ctx = _moe(hc2.reshape(bsz * n_ctx, d), comb_c, wg, wu, wd, c_mid.reshape(bsz * n_ctx, d), mod,
                       lambda i: bsz, None, tm_mc).reshape(bsz, n_ctx, d)
    return x
```

```python
import functools
import math

import numpy as np
import jax
import jax.numpy as jnp
from jax import lax
from jax.experimental import pallas as pl
from jax.experimental.pallas import tpu as pltpu

F32 = jnp.float32
BF16 = jnp.bfloat16
HIGHEST = lax.Precision.HIGHEST

GRID_W = 64
HY_ORDER = 2
HY_BANDS = 16
HY_DECAY_TARGET = 1e-2
HY_FAST_DECAY_PCT = 0.3
HY_SLOW_DECAY_PCT = 1.5
HG_HEAD_DIM = 128
FN_GROUPS = 4
N_EXPERTS = 16
N_GROUPS = 4
EXPERTS_PER_GROUP = N_EXPERTS // N_GROUPS
N_MOD = 6
RMS_EPS = 1e-6
POS_BASE = 10000.0

V7X_VMEM_LIMIT_BYTES = 56 * 1024 * 1024
SCAN_CHUNK = 64
SCAN_SUB = 16
SCAN_SAFE_LOG2_RANGE = 80.0
LOG2_E = 1.0 / math.log(2.0)
MOE_CAP = 192
DFT_SPLIT = 64
NEG_BIG = -1e30


def _cparams(sem):
    return pltpu.CompilerParams(dimension_semantics=sem, vmem_limit_bytes=V7X_VMEM_LIMIT_BYTES)


def _const_spec(shape):
    nd = len(shape)
    return pl.BlockSpec(shape, lambda *_: (0,) * nd, pipeline_mode=pl.Buffered(1))


def _silu(x):
    return x * jax.nn.sigmoid(x)


def _dot(a, b):
    return jnp.dot(a, b, preferred_element_type=F32)


def _dot_nt(a, b, precision=None):
    return lax.dot_general(a, b, (((1,), (1,)), ((), ())), precision=precision,
                           preferred_element_type=F32)


def _dot_tn(a, b):
    return lax.dot_general(a, b, (((0,), (0,)), ((), ())), preferred_element_type=F32)


def _ada_kernel(c_ref, w_ref, b_ref, o_ref):
    s = _silu(c_ref[...])
    o_ref[0] = jnp.dot(s, w_ref[0], precision=HIGHEST, preferred_element_type=F32) + b_ref[0]


def _ada_modulation(cond, ada_w, ada_b):
    depth, d, nd = ada_w.shape
    rows = cond.shape[0]
    tn = 1536
    return pl.pallas_call(
        _ada_kernel,
        out_shape=jax.ShapeDtypeStruct((depth, rows, nd), F32),
        grid=(depth, nd // tn),
        in_specs=[pl.BlockSpec((rows, d), lambda l, j: (0, 0)),
                  pl.BlockSpec((1, d, tn), lambda l, j: (l, 0, j)),
                  pl.BlockSpec((1, 1, tn), lambda l, j: (l, 0, j))],
        out_specs=pl.BlockSpec((1, rows, tn), lambda l, j: (l, 0, j)),
        compiler_params=_cparams(("parallel", "parallel")),
        name="ada_modulation",
    )(cond, ada_w, ada_b.reshape(depth, 1, nd))


def _inproj_kernel(*refs, add_pos, segs):
    if add_pos:
        x_ref, pos_ref, g_ref, sh_ref, sc_ref, w_ref = refs[:6]
        outs = refs[6:]
        x = x_ref[0] + pos_ref[...]
    else:
        x_ref, g_ref, sh_ref, sc_ref, w_ref = refs[:5]
        outs = refs[5:]
        x = x_ref[0]
    ms = jnp.mean(x * x, axis=-1, keepdims=True)
    h = x * lax.rsqrt(ms + RMS_EPS) * g_ref[...]
    h = (h * (1.0 + sc_ref[0]) + sh_ref[0]).astype(BF16)
    for o_ref, (a, b, _) in zip(outs, segs):
        o_ref[0] = _dot(h, w_ref[:, a:b]).astype(o_ref.dtype)


def _in_proj(x, pos, norm_g, mod, mod_row, w, segs, tm):
    bsz, seq, d = x.shape
    add_pos = pos is not None
    in_specs = [pl.BlockSpec((1, tm, d), lambda b, i: (b, i, 0))]
    args = [x]
    if add_pos:
        in_specs.append(pl.BlockSpec((tm, d), lambda b, i: (i, 0)))
        args.append(pos)
    in_specs += [pl.BlockSpec((1, d), lambda b, i: (0, 0)),
                 pl.BlockSpec((1, 1, d), lambda b, i: (mod_row(b), 0, 0)),
                 pl.BlockSpec((1, 1, d), lambda b, i: (mod_row(b), 0, 1)),
                 _const_spec(w.shape)]
    args += [norm_g.reshape(1, d), mod, mod, w]
    out_shape = [jax.ShapeDtypeStruct((bsz, seq, b - a), dt) for a, b, dt in segs]
    out_specs = [pl.BlockSpec((1, tm, b - a), lambda bb, i: (bb, i, 0)) for a, b, _ in segs]
    return pl.pallas_call(
        functools.partial(_inproj_kernel, add_pos=add_pos, segs=tuple(segs)),
        out_shape=out_shape, grid=(bsz, seq // tm), in_specs=in_specs, out_specs=out_specs,
        compiler_params=_cparams(("parallel", "parallel")),
        name="in_proj",
    )(*args)


def _cumsum_rows(x, reverse):
    c = x.shape[0]
    row = lax.broadcasted_iota(jnp.int32, x.shape, 0)
    sh = 1
    while sh < c:
        if reverse:
            x = x + jnp.where(row < c - sh, pltpu.roll(x, c - sh, 0), 0.0)
        else:
            x = x + jnp.where(row >= sh, pltpu.roll(x, sh, 0), 0.0)
        sh *= 2
    return x


def _scan_chunk_fast(q, lf2, k, v, st_ref, h, reverse):
    c = SCAN_CHUNK
    s = SCAN_SUB
    nsub = c // s
    b = _cumsum_rows(lf2, reverse)
    qs = _silu(q) * (HG_HEAD_DIM ** -0.5)
    st = st_ref[h]
    o_inter = _dot_nt((qs * jnp.exp2(b)).astype(BF16), st.astype(BF16))
    v_bf = v.astype(BF16)
    zero_row = jnp.zeros((1, HG_HEAD_DIM), F32)
    zero_blk = jnp.zeros((s, HG_HEAD_DIM), BF16)
    kt = {}
    prev_ref = None
    sc = [None] * nsub
    for i in (range(nsub - 1, -1, -1) if reverse else range(nsub)):
        r0 = i * s
        bi = b[r0:r0 + s]
        if reverse:
            ref = b[r0 + s:r0 + s + 1] if i < nsub - 1 else zero_row
        else:
            ref = b[r0 - 1:r0] if i > 0 else zero_row
        if prev_ref is not None:
            step = jnp.exp2(ref - prev_ref)
            kt = {j: blk * step for j, blk in kt.items()}
        kt[i] = k[r0:r0 + s] * jnp.exp2(ref - bi)
        prev_ref = ref
        qi = (qs[r0:r0 + s] * jnp.exp2(bi - ref)).astype(BF16)
        keys = jnp.concatenate([kt[j].astype(BF16) if j in kt else zero_blk for j in range(nsub)], axis=0)
        sc[i] = _dot_nt(qi, keys)
    scores = jnp.concatenate(sc, axis=0)
    rr = lax.broadcasted_iota(jnp.int32, (c, c), 0)
    cc = lax.broadcasted_iota(jnp.int32, (c, c), 1)
    scores = jnp.where((cc >= rr) if reverse else (cc <= rr), scores, 0.0)
    o = o_inter + _dot(scores.astype(BF16), v_bf)

    b_end = b[0:1] if reverse else b[c - 1:c]
    kd = (k * jnp.exp2(b_end - b)).astype(BF16)
    st_ref[h] = st * jnp.exp2(b_end) + _dot_tn(v_bf, kd)
    return o


def _scan_chunk_exact(q, lf2, k, v, st_ref, h, b_scr, q_scr, reverse):
    c = SCAN_CHUNK
    s = SCAN_SUB
    nsub = c // s
    b = _cumsum_rows(lf2, reverse)
    qs = _silu(q) * (HG_HEAD_DIM ** -0.5)
    st = st_ref[h]
    o_inter = _dot_nt((qs * jnp.exp2(b)).astype(BF16), st.astype(BF16))

    b_scr[...] = b
    q_scr[...] = qs
    v_bf = v.astype(BF16)
    ones = jnp.ones((HG_HEAD_DIM, HG_HEAD_DIM), BF16)
    sub_iota = lax.broadcasted_iota(jnp.int32, (s, HG_HEAD_DIM), 0)
    row_iota = lax.broadcasted_iota(jnp.int32, (c, HG_HEAD_DIM), 0)
    grp = (lax.broadcasted_iota(jnp.int32, (s, s * s), 1) // s ==
           lax.broadcasted_iota(jnp.int32, (s, s * s), 0)).astype(BF16)
    o_parts = []
    for i in range(nsub):
        r0 = i * s
        bi = b[r0:r0 + s]
        ki = k[r0:r0 + s]
        vi = v[r0:r0 + s]
        prods = []
        for t in range(s):
            bt = b_scr[pl.ds(r0 + t, 1), :]
            qt = q_scr[pl.ds(r0 + t, 1), :]
            keep = (sub_iota >= t) if reverse else (sub_iota <= t)
            e = jnp.exp2(jnp.where(keep, bt - bi, NEG_BIG))
            prods.append(((qt * ki) * e).astype(BF16))
        p = jnp.concatenate(prods, axis=0)
        rsum = _dot(p, ones)
        zt = (rsum * jnp.concatenate([vi] * s, axis=0)).astype(BF16)
        o_i = _dot(grp, zt)
        if reverse and i < nsub - 1:
            ref_row = b[r0 + s:r0 + s + 1]
            key_rows = row_iota >= r0 + s
        elif (not reverse) and i > 0:
            ref_row = b[r0 - 1:r0]
            key_rows = row_iota < r0
        else:
            ref_row = None
        if ref_row is not None:
            qi = qs[r0:r0 + s] * jnp.exp2(bi - ref_row)
            ks = k * jnp.exp2(jnp.where(key_rows, ref_row - b, NEG_BIG))
            sc = _dot_nt(qi.astype(BF16), ks.astype(BF16))
            o_i = o_i + _dot(sc.astype(BF16), v_bf)
        o_parts.append(o_i)
    o = o_inter + jnp.concatenate(o_parts, axis=0)

    b_end = b[0:1] if reverse else b[c - 1:c]
    kd = (k * jnp.exp2(b_end - b)).astype(BF16)
    st_ref[h] = st * jnp.exp2(b_end) + _dot_tn(v_bf, kd)
    return o


def _scan_kernel(qf_ref, qb_ref, ff_ref, fb_ref, vf_ref, vb_ref, lb_ref, s0f_ref, s0b_ref,
                 of_ref, ob_ref, sf_ref, sb_ref, stf, stb, lf_scr, k_scr, bf_scr, qf_scr, bb_scr, qb_scr,
                 *, n_heads, n_chunks):
    n = pl.program_id(1)

    @pl.when(n == 0)
    def _():
        stf[...] = s0f_ref[0]
        stb[...] = s0b_ref[0]

    tb = n_chunks * SCAN_CHUNK
    nblk = tb // SCAN_SUB
    sel = (lax.broadcasted_iota(jnp.int32, (nblk, tb), 1) // SCAN_SUB ==
           lax.broadcasted_iota(jnp.int32, (nblk, tb), 0)).astype(BF16)
    worst = jnp.zeros((nblk, lb_ref.shape[1]), F32)
    for d, f_ref in enumerate((ff_ref, fb_ref)):
        lb = lb_ref[d:d + 1, :]
        forget = lb + (1.0 - lb) * jax.nn.sigmoid(f_ref[0])
        lf2 = jnp.log(forget) * LOG2_E
        lf_scr[d] = lf2
        k_scr[d] = 1.0 - forget
        worst = jnp.maximum(worst, -_dot(sel, lf2.astype(BF16)))
    safe = jnp.max(worst) <= SCAN_SAFE_LOG2_RANGE

    def rows_of(ci):
        rf = pl.ds(pl.multiple_of(ci * SCAN_CHUNK, SCAN_CHUNK), SCAN_CHUNK)
        rb = pl.ds(pl.multiple_of((n_chunks - 1 - ci) * SCAN_CHUNK, SCAN_CHUNK), SCAN_CHUNK)
        return rf, rb

    @pl.when(safe)
    def _():
        def chunk_body(ci, carry):
            rf, rb = rows_of(ci)
            for h in range(n_heads):
                cols = slice(h * HG_HEAD_DIM, (h + 1) * HG_HEAD_DIM)
                of_ref[0, rf, cols] = _scan_chunk_fast(qf_ref[0, rf, cols], lf_scr[0, rf, cols],
                                                       k_scr[0, rf, cols], vf_ref[0, rf, cols], stf, h, False)
                ob_ref[0, rb, cols] = _scan_chunk_fast(qb_ref[0, rb, cols], lf_scr[1, rb, cols],
                                                       k_scr[1, rb, cols], vb_ref[0, rb, cols], stb, h, True)
            return carry

        lax.fori_loop(0, n_chunks, chunk_body, 0, unroll=2)

    @pl.when(jnp.logical_not(safe))
    def _():
        def head_body(h, carry):
            cols = pl.ds(pl.multiple_of(h * HG_HEAD_DIM, HG_HEAD_DIM), HG_HEAD_DIM)

            def chunk_body(ci, carry2):
                rf, rb = rows_of(ci)
                of_ref[0, rf, cols] = _scan_chunk_exact(qf_ref[0, rf, cols], lf_scr[0, rf, cols],
                                                        k_scr[0, rf, cols], vf_ref[0, rf, cols], stf, h,
                                                        bf_scr, qf_scr, False)
                ob_ref[0, rb, cols] = _scan_chunk_exact(qb_ref[0, rb, cols], lf_scr[1, rb, cols],
                                                        k_scr[1, rb, cols], vb_ref[0, rb, cols], stb, h,
                                                        bb_scr, qb_scr, True)
                return carry2

            return lax.fori_loop(0, n_chunks, chunk_body, carry)

        lax.fori_loop(0, n_heads, head_body, 0)

    @pl.when(n == pl.num_programs(1) - 1)
    def _():
        sf_ref[0] = stf[...]
        sb_ref[0] = stb[...]


def _hgrn2_scan(z_hg, lb, s0f, s0b, tb):
    bsz, seq, w5 = z_hg.shape
    w = w5 // 5
    nh = w // HG_HEAD_DIM
    nb = seq // tb
    blk = (1, tb, w)
    fwd = lambda j: pl.BlockSpec(blk, lambda b, n: (b, n, j))
    bwd = lambda j: pl.BlockSpec(blk, lambda b, n: (b, nb - 1 - n, j))
    st_spec = pl.BlockSpec((1, nh, HG_HEAD_DIM, HG_HEAD_DIM), lambda b, n: (b, 0, 0, 0))
    st_shape = jax.ShapeDtypeStruct((bsz, nh, HG_HEAD_DIM, HG_HEAD_DIM), F32)
    o_shape = jax.ShapeDtypeStruct((bsz, seq, w), F32)
    return pl.pallas_call(
        functools.partial(_scan_kernel, n_heads=nh, n_chunks=tb // SCAN_CHUNK),
        out_shape=[o_shape, o_shape, st_shape, st_shape],
        grid=(bsz, nb),
        in_specs=[fwd(0), bwd(0), fwd(1), bwd(2), fwd(3), bwd(3),
                  pl.BlockSpec((2, w), lambda b, n: (0, 0)), st_spec, st_spec],
        out_specs=[fwd(0), bwd(0), st_spec, st_spec],
        scratch_shapes=[pltpu.VMEM((nh, HG_HEAD_DIM, HG_HEAD_DIM), F32),
                        pltpu.VMEM((nh, HG_HEAD_DIM, HG_HEAD_DIM), F32),
                        pltpu.VMEM((2, tb, w), F32),
                        pltpu.VMEM((2, tb, w), F32),
                        pltpu.VMEM((SCAN_CHUNK, HG_HEAD_DIM), F32),
                        pltpu.VMEM((SCAN_CHUNK, HG_HEAD_DIM), F32),
                        pltpu.VMEM((SCAN_CHUNK, HG_HEAD_DIM), F32),
                        pltpu.VMEM((SCAN_CHUNK, HG_HEAD_DIM), F32)],
        compiler_params=_cparams(("parallel", "arbitrary")),
        name="hgrn2_scan",
    )(z_hg, z_hg, z_hg, z_hg, z_hg, z_hg, lb, s0f, s0b)


def _dft_mats(n_rows, period):
    t = jnp.arange(n_rows, dtype=jnp.int32)

    def table(mult, count):
        m = (jnp.arange(count, dtype=jnp.int32)[:, None] * mult * t[None, :]) % period
        ang = m.astype(F32) * (2.0 * math.pi / period)
        return jnp.cos(ang), jnp.sin(ang)

    n_hi = n_rows // DFT_SPLIT
    ca, sa = table(DFT_SPLIT, n_hi)
    cb, sb = table(1, DFT_SPLIT)
    cmat = ca[:, None, :] * cb[None, :, :] - sa[:, None, :] * sb[None, :, :]
    smat = sa[:, None, :] * cb[None, :, :] + ca[:, None, :] * sb[None, :, :]
    return cmat.reshape(n_rows, n_rows).astype(BF16), smat.reshape(n_rows, n_rows).astype(BF16)


def _split_bf16(x):
    hi = x.astype(BF16)
    lo = (x - hi.astype(F32)).astype(BF16)
    return hi, lo


def _filter_spectrum_kernel(hs_ref, hd_ref, c_ref, s_ref, kr_ref, ki_ref, kn_ref, *, scale):
    hs = hs_ref[...]
    hd = hd_ref[...]
    hs_hi, hs_lo = _split_bf16(hs)
    hd_hi, hd_lo = _split_bf16(hd)
    c = c_ref[...]
    s = s_ref[...]
    kr_ref[...] = ((_dot(c, hs_hi) + _dot(c, hs_lo)) * scale).astype(kr_ref.dtype)
    ki_ref[...] = ((_dot(s, hd_hi) + _dot(s, hd_lo)) * (-scale)).astype(ki_ref.dtype)
    row = lax.broadcasted_iota(jnp.int32, hs.shape, 0)
    sign = jnp.where(row % 2 == 0, 1.0, -1.0)
    kn = jnp.sum(hs * sign, axis=0, keepdims=True) * scale
    kn_ref[...] = jnp.broadcast_to(kn, kn_ref.shape)


def _filter_spectrum(hs, hd, cmat, smat):
    seq, ch = hs.shape
    scale = 2.0 / (2 * seq)
    return pl.pallas_call(
        functools.partial(_filter_spectrum_kernel, scale=scale),
        out_shape=[jax.ShapeDtypeStruct((seq, ch), BF16), jax.ShapeDtypeStruct((seq, ch), BF16),
                   jax.ShapeDtypeStruct((8, ch), F32)],
        compiler_params=pltpu.CompilerParams(vmem_limit_bytes=V7X_VMEM_LIMIT_BYTES),
        name="hyena_filter_spectrum",
    )(hs, hd, cmat, smat)


def _hyena_kernel(z_ref, cw_ref, cb_ref, c_ref, s_ref, kr_ref, ki_ref, kn_ref, g_ref, o_ref, *, seq, ch):
    row = lax.broadcasted_iota(jnp.int32, (seq, 1), 0)
    sign = jnp.where(row % 2 == 0, 1.0, -1.0)

    def short_conv(part):
        cols = slice(part * ch, (part + 1) * ch)
        z = z_ref[0, :, cols].astype(F32)
        z_prev = jnp.where(row >= 1, pltpu.roll(z, 1, 0), 0.0)
        z_next = jnp.where(row <= seq - 2, pltpu.roll(z, seq - 1, 0), 0.0)
        return z_prev * cw_ref[0:1, cols] + z * cw_ref[1:2, cols] + z_next * cw_ref[2:3, cols] + cb_ref[:, cols]

    def long_conv(x, order):
        cols = slice(order * ch, (order + 1) * ch)
        xb = x.astype(BF16)
        a = _dot(c_ref[...], xb)
        bm = _dot(s_ref[...], xb)
        kr = kr_ref[:, cols]
        ki = ki_ref[:, cols]
        yr = a * kr + bm * ki
        yi = (a * ki - bm * kr).astype(BF16)
        dc = 0.5 * yr[0:1, :]
        x_nyq = jnp.sum(x * sign, axis=0, keepdims=True)
        y = _dot(c_ref[...], yr.astype(BF16)) - _dot(s_ref[...], yi)
        return y - dc + (0.5 * x_nyq * kn_ref[0:1, cols]) * sign

    y = short_conv(1) * long_conv(short_conv(0), 0)
    y = short_conv(2) * long_conv(y, 1)
    ms = jnp.mean(y * y, axis=-1, keepdims=True)
    o_ref[0] = y * lax.rsqrt(ms + RMS_EPS) * g_ref[...]


def _hyena(z_hy, conv_w, conv_b, cmat, smat, kr, ki, kn, norm_g):
    bsz, seq, c3 = z_hy.shape
    ch = c3 // 3
    return pl.pallas_call(
        functools.partial(_hyena_kernel, seq=seq, ch=ch),
        out_shape=jax.ShapeDtypeStruct((bsz, seq, ch), F32),
        grid=(bsz,),
        in_specs=[pl.BlockSpec((1, seq, c3), lambda b: (b, 0, 0)),
                  _const_spec(conv_w.shape), _const_spec((1, c3)),
                  _const_spec(cmat.shape), _const_spec(smat.shape),
                  _const_spec(kr.shape), _const_spec(ki.shape), _const_spec(kn.shape),
                  _const_spec((1, ch))],
        out_specs=pl.BlockSpec((1, seq, ch), lambda b: (b, 0, 0)),
        compiler_params=_cparams(("parallel",)),
        name="hyena_mixer",
    )(z_hy, conv_w, conv_b.reshape(1, c3), cmat, smat, kr, ki, kn, norm_g.reshape(1, ch))


def _fnet_kernel(z_ref, c_ref, s_ref, bdc_ref, bds_ref, bdw_ref, b_ref, g_ref, o_ref, *, scale):
    zb = z_ref[0].astype(BF16)
    y1 = _dot(c_ref[...], zb)
    y2 = _dot(s_ref[...], zb)
    r = (_dot(y1.astype(BF16), bdc_ref[...]) - _dot(y2.astype(BF16), bds_ref[...])) * scale
    y = _dot(r.astype(BF16), bdw_ref[...]) + b_ref[...]
    ms = jnp.mean(y * y, axis=-1, keepdims=True)
    o_ref[0] = y * lax.rsqrt(ms + RMS_EPS) * g_ref[...]


def _block_diag(blocks):
    g, a, b = blocks.shape
    out = jnp.zeros((g * a, g * b), blocks.dtype)
    for i in range(g):
        out = out.at[i * a:(i + 1) * a, i * b:(i + 1) * b].set(blocks[i])
    return out


def _fnet(z_fn, cmat, smat, fn_w, fn_b, norm_g):
    bsz, seq, ch = z_fn.shape
    gd = ch // FN_GROUPS
    k = np.arange(gd)
    ang = 2.0 * np.pi * ((k[:, None] * k[None, :]) % gd) / gd
    eye = np.eye(FN_GROUPS)
    bdc = jnp.asarray(np.kron(eye, np.cos(ang)), BF16)
    bds = jnp.asarray(np.kron(eye, np.sin(ang)), BF16)
    bdw = _block_diag(fn_w).astype(BF16)
    scale = 1.0 / math.sqrt(seq * gd)
    return pl.pallas_call(
        functools.partial(_fnet_kernel, scale=scale),
        out_shape=jax.ShapeDtypeStruct((bsz, seq, ch), F32),
        grid=(bsz,),
        in_specs=[pl.BlockSpec((1, seq, ch), lambda b: (b, 0, 0)),
                  _const_spec(cmat.shape), _const_spec(smat.shape),
                  _const_spec((ch, ch)), _const_spec((ch, ch)), _const_spec((ch, ch)),
                  _const_spec((1, ch)), _const_spec((1, ch))],
        out_specs=pl.BlockSpec((1, seq, ch), lambda b: (b, 0, 0)),
        compiler_params=_cparams(("parallel",)),
        name="fnet_mixer",
    )(z_fn, cmat, smat, bdc, bds, bdw, fn_b.reshape(1, ch), norm_g.reshape(1, ch))


def _route(logits_t, rb_ref):
    rows = [logits_t[e:e + 1, :] for e in range(N_EXPERTS)]
    mx = functools.reduce(jnp.maximum, rows)
    ex = [jnp.exp(r - mx) for r in rows]
    inv = 1.0 / functools.reduce(lambda a, b: a + b, ex)
    probs = [e * inv for e in ex]
    sel = [probs[e] + rb_ref[e:e + 1, 0:1] for e in range(N_EXPERTS)]
    epg = EXPERTS_PER_GROUP
    gscore = []
    for g in range(N_GROUPS):
        s = sel[g * epg:(g + 1) * epg]
        pairs = [s[i] + s[j] for i in range(epg) for j in range(i + 1, epg)]
        gscore.append(functools.reduce(jnp.maximum, pairs))
    best = gscore[0]
    best_g = jnp.zeros_like(best, dtype=jnp.int32)
    for g in range(1, N_GROUPS):
        better = gscore[g] > best
        best_g = jnp.where(better, g, best_g)
        best = jnp.where(better, gscore[g], best)
    chosen = []
    for e in range(N_EXPERTS):
        g = e // epg
        beaten = jnp.zeros_like(best_g)
        for e2 in range(g * epg, (g + 1) * epg):
            if e2 == e:
                continue
            wins = (sel[e2] > sel[e]) | ((sel[e2] == sel[e]) & (e2 < e))
            beaten = beaten + wins.astype(jnp.int32)
        chosen.append(jnp.where((best_g == g) & (beaten < 2), probs[e], 0.0))
    tot = functools.reduce(lambda a, b: a + b, chosen)
    inv_tot = 1.0 / tot
    return jnp.concatenate([c * inv_tot for c in chosen], axis=0)


def _outproj_kernel(*refs, add_pos, n_heads):
    if add_pos:
        x_ref, pos_ref = refs[:2]
        rest = refs[2:]
        x = x_ref[0] + pos_ref[...]
    else:
        x_ref = refs[0]
        rest = refs[1:]
        x = x_ref[0]
    (yhy_ref, of_ref, ob_ref, g_ref, yfn_ref, wout_ref, hgg_ref, gate_ref, n2g_ref, sh_ref, sc_ref,
     rwt_ref, rb_ref, xo_ref, h_ref, comb_ref) = rest
    o = of_ref[0] + ob_ref[0]
    parts = []
    for h in range(n_heads):
        oh = o[:, h * HG_HEAD_DIM:(h + 1) * HG_HEAD_DIM]
        parts.append(oh * lax.rsqrt(jnp.mean(oh * oh, axis=-1, keepdims=True) + RMS_EPS))
    y_hg = jnp.concatenate(parts, axis=-1) * hgg_ref[...] * _silu(g_ref[0])
    c_hy = yhy_ref.shape[-1]
    c_hg = y_hg.shape[-1]
    mix = (_dot(yhy_ref[0].astype(BF16), wout_ref[0:c_hy, :]) +
           _dot(y_hg.astype(BF16), wout_ref[c_hy:c_hy + c_hg, :]) +
           _dot(yfn_ref[0].astype(BF16), wout_ref[c_hy + c_hg:, :]))
    xn = x + gate_ref[0] * mix
    xo_ref[0] = xn
    ms = jnp.mean(xn * xn, axis=-1, keepdims=True)
    h2 = xn * lax.rsqrt(ms + RMS_EPS) * n2g_ref[...]
    h2 = h2 * (1.0 + sc_ref[0]) + sh_ref[0]
    h_ref[0] = h2.astype(BF16)
    logits_t = _dot_nt(rwt_ref[...], h2, precision=HIGHEST)
    comb_ref[0] = _route(logits_t, rb_ref)


def _out_proj(x, pos, y_hy, o_f, o_b, z_hg, y_fn, w_out, hg_norm_g, mod, mod_row, norm2_g, rwt, rb, tm):
    bsz, seq, d = x.shape
    add_pos = pos is not None
    c_hy, c_hg, c_fn = y_hy.shape[-1], o_f.shape[-1], y_fn.shape[-1]
    tok = lambda c, j=0: pl.BlockSpec((1, tm, c), lambda b, i: (b, i, j))
    modk = lambda k: pl.BlockSpec((1, 1, d), lambda b, i: (mod_row(b), 0, k))
    in_specs = [tok(d)]
    args = [x]
    if add_pos:
        in_specs.append(pl.BlockSpec((tm, d), lambda b, i: (i, 0)))
        args.append(pos)
    in_specs += [tok(c_hy), tok(c_hg), tok(c_hg), tok(c_hg, 4), tok(c_fn),
                 _const_spec(w_out.shape), _const_spec((1, c_hg)), modk(2), _const_spec((1, d)),
                 modk(3), modk(4), _const_spec(rwt.shape), _const_spec(rb.shape)]
    args += [y_hy, o_f, o_b, z_hg, y_fn, w_out, hg_norm_g.reshape(1, c_hg), mod, norm2_g.reshape(1, d),
             mod, mod, rwt, rb]
    nt = seq // tm
    out_shape = [jax.ShapeDtypeStruct((bsz, seq, d), F32), jax.ShapeDtypeStruct((bsz, seq, d), BF16),
                 jax.ShapeDtypeStruct((bsz * nt, N_EXPERTS, tm), F32)]
    out_specs = [tok(d), tok(d), pl.BlockSpec((1, N_EXPERTS, tm), lambda b, i: (b * nt + i, 0, 0))]
    return pl.pallas_call(
        functools.partial(_outproj_kernel, add_pos=add_pos, n_heads=c_hg // HG_HEAD_DIM),
        out_shape=out_shape, grid=(bsz, nt), in_specs=in_specs, out_specs=out_specs,
        compiler_params=_cparams(("parallel", "parallel")),
        name="out_proj_router",
    )(*args)


def _moe_kernel(*refs, final, cap):
    if final:
        (cnt_ref, h_ref, comb_ref, combt_ref, wg_ref, wu_ref, wd_ref, x_ref, gate_ref, fg_ref, o_ref,
         acc_ref, rcol_ref, rrow_ref) = refs
    else:
        (cnt_ref, h_ref, comb_ref, combt_ref, wg_ref, wu_ref, wd_ref, x_ref, gate_ref, o_ref,
         acc_ref, rcol_ref, rrow_ref) = refs
    e = pl.program_id(1)
    tm = h_ref.shape[0]
    n_rt = combt_ref.shape[0]

    @pl.when(e == 0)
    def _():
        acc_ref[...] = jnp.zeros_like(acc_ref)
        ti = lax.broadcasted_iota(jnp.int32, (tm, tm), 0)
        tj = lax.broadcasted_iota(jnp.int32, (tm, tm), 1)
        before_col = jnp.where(tj < ti, 1.0, 0.0).astype(BF16)
        before_row = jnp.where(ti < tj, 1.0, 0.0).astype(BF16)
        sel_col = jnp.where(comb_ref[...] > 0.0, 1.0, 0.0).astype(BF16)
        combt = jnp.concatenate([combt_ref[j] for j in range(n_rt)], axis=1)
        sel_row = jnp.where(combt > 0.0, 1.0, 0.0).astype(BF16)
        rcol_ref[...] = _dot(before_col, sel_col)
        rrow_ref[...] = _dot(sel_row, before_row)

    comb = comb_ref[...]
    lane = lax.broadcasted_iota(jnp.int32, comb.shape, 1)
    ce = jnp.sum(jnp.where(lane == e, comb, 0.0), axis=-1, keepdims=True)
    fits = cnt_ref[pl.program_id(0), e] <= cap

    @pl.when(fits)
    def _():
        h = h_ref[...]
        ce_row = jnp.concatenate([combt_ref[j, pl.ds(e, 1), :] for j in range(n_rt)], axis=1)
        rank_col = jnp.sum(jnp.where(lane == e, rcol_ref[...], 0.0), axis=-1, keepdims=True)
        rank_row = rrow_ref[pl.ds(e, 1), :]
        slot_r = lax.broadcasted_iota(jnp.int32, (cap, tm), 0).astype(F32)
        pick = jnp.where((slot_r == rank_row) & (ce_row > 0.0), 1.0, 0.0).astype(BF16)
        hc = _dot(pick, h).astype(BF16)
        a = _silu(_dot(hc, wg_ref[0])) * _dot(hc, wu_ref[0])
        y = _dot(a.astype(BF16), wd_ref[0]).astype(BF16)
        slot_c = lax.broadcasted_iota(jnp.int32, (tm, cap), 1).astype(F32)
        spread = jnp.where((slot_c == rank_col) & (ce > 0.0), 1.0, 0.0).astype(BF16)
        acc_ref[...] += ce * _dot(spread, y)

    @pl.when(jnp.logical_not(fits))
    def _():
        h = h_ref[...]
        a = _silu(_dot(h, wg_ref[0])) * _dot(h, wu_ref[0])
        acc_ref[...] += ce * _dot(a.astype(BF16), wd_ref[0])

    @pl.when(e == pl.num_programs(1) - 1)
    def _():
        y = x_ref[...] + gate_ref[0] * acc_ref[...]
        if final:
            ms = jnp.mean(y * y, axis=-1, keepdims=True)
            y = y * lax.rsqrt(ms + RMS_EPS) * fg_ref[...]
        o_ref[...] = y


def _moe(h, comb_t, wg, wu, wd, x, mod, mod_row_of_tile, final_g, tm):
    n, d = h.shape
    ne, _, de = wg.shape
    final = final_g is not None
    tr = comb_t.shape[-1]
    n_rt = tm // tr
    lanes = 128
    comb = jnp.transpose(comb_t, (0, 2, 1)).reshape(n, ne)
    comb = jnp.pad(comb, ((0, 0), (0, lanes - ne)))
    cap = min(MOE_CAP, tm)
    counts = jnp.sum((comb_t > 0.0).reshape(n // tm, n_rt, ne, tr), axis=(1, 3)).astype(jnp.int32)
    in_specs = [pl.BlockSpec((tm, d), lambda i, e, c: (i, 0)),
                pl.BlockSpec((tm, lanes), lambda i, e, c: (i, 0)),
                pl.BlockSpec((n_rt, ne, tr), lambda i, e, c: (i, 0, 0)),
                pl.BlockSpec((1, d, de), lambda i, e, c: (e, 0, 0)),
                pl.BlockSpec((1, d, de), lambda i, e, c: (e, 0, 0)),
                pl.BlockSpec((1, de, d), lambda i, e, c: (e, 0, 0)),
                pl.BlockSpec((tm, d), lambda i, e, c: (i, 0)),
                pl.BlockSpec((1, 1, d), lambda i, e, c: (mod_row_of_tile(i), 0, 5))]
    args = [counts, h, comb, comb_t, wg, wu, wd, x, mod]
    if final:
        in_specs.append(pl.BlockSpec((1, d), lambda i, e, c: (0, 0)))
        args.append(final_g.reshape(1, d))
    return pl.pallas_call(
        functools.partial(_moe_kernel, final=final, cap=cap),
        out_shape=jax.ShapeDtypeStruct((n, d), F32),
        grid_spec=pltpu.PrefetchScalarGridSpec(
            num_scalar_prefetch=1, grid=(n // tm, ne), in_specs=in_specs,
            out_specs=pl.BlockSpec((tm, d), lambda i, e, c: (i, 0)),
            scratch_shapes=[pltpu.VMEM((tm, d), F32), pltpu.VMEM((tm, lanes), F32),
                            pltpu.VMEM((ne, tm), F32)]),
        compiler_params=_cparams(("parallel", "arbitrary")),
        name="moe_experts",
    )(*args)


def _grid_sincos(n_tok, d):
    rows = n_tok // GRID_W
    row = jnp.repeat(jnp.arange(rows, dtype=F32), GRID_W)
    col = jnp.tile(jnp.arange(GRID_W, dtype=F32), rows)
    quarter = d // 4
    omega = 1.0 / (POS_BASE ** (jnp.arange(quarter, dtype=F32) / quarter))
    ar = row[:, None] * omega
    ac = col[:, None] * omega
    return jnp.concatenate([jnp.sin(ar), jnp.cos(ar), jnp.sin(ac), jnp.cos(ac)], axis=-1)


def _hyena_filters(seq, ch, w1, b1, w2, b2, w3, freq):
    t = jnp.linspace(0.0, 1.0, seq, dtype=F32)[:, None]
    w = 2.0 * math.pi * jnp.arange(seq, dtype=F32)[:, None] / seq
    bands = jnp.linspace(1e-4, HY_BANDS - 1, HY_BANDS, dtype=F32)[None, :]
    z = jnp.concatenate([t, jnp.cos(bands * w), -jnp.sin(bands * w)], axis=-1)
    hp = functools.partial(jnp.dot, precision=HIGHEST)
    h = jnp.sin(freq * (hp(z, w1) + b1))
    h = jnp.sin(freq * (hp(h, w2) + b2))
    h = hp(h, w3).reshape(seq, 2 * HY_ORDER, ch)
    max_decay = math.log(HY_DECAY_TARGET) / HY_FAST_DECAY_PCT
    min_decay = math.log(HY_DECAY_TARGET) / HY_SLOW_DECAY_PCT
    deltas = jnp.linspace(min_decay, max_decay, ch, dtype=F32)
    return h * jnp.exp(-t[:, :, None] * jnp.abs(deltas))


def _filter_halves(h, bias):
    seq = h.shape[0]
    not0 = (jnp.arange(seq) > 0)[:, None].astype(F32)
    hs, hd = [], []
    for o in range(HY_ORDER):
        hf = h[:, 2 * o].at[0].add(bias[o])
        hb = h[:, 2 * o + 1] * not0
        hs.append(hf + hb)
        hd.append(hf - hb)
    return jnp.concatenate(hs, axis=-1), jnp.concatenate(hd, axis=-1)


def kernel(x, c, ctx, c_ctx, ada_w, ada_b, norm1_g, norm2_g, w_in, w_out, hy_conv_w, hy_conv_b, hy_filt_w1, hy_filt_b1, hy_filt_w2, hy_filt_b2, hy_filt_w3, hy_filt_freq, hy_bias, hy_norm_g, hg_lower_bounds, hg_norm_g, fn_w, fn_b, fn_norm_g, router_w, router_b, moe_w_gate, moe_w_up, moe_w_down, final_norm_g):
    bsz, seq, d = x.shape
    n_ctx = ctx.shape[1]
    depth = ada_w.shape[0]
    c_hy = hy_norm_g.shape[-1]
    c_hyp = hy_conv_w.shape[-1]
    w_hg = hg_norm_g.shape[-1]
    c_fn = fn_norm_g.shape[-1]
    hg_lo, hg_hi = c_hyp, c_hyp + 5 * w_hg
    nh = w_hg // HG_HEAD_DIM

    cs = jnp.cumsum(jax.nn.softmax(hg_lower_bounds.astype(F32), axis=0), axis=0)
    lower_bounds = cs - cs[0:1]
    pos = _grid_sincos(seq, d)

    n_rows = -(-(bsz + 1) // 8) * 8
    cond = jnp.zeros((n_rows, d), F32).at[:bsz].set(c).at[bsz].set(c_ctx)
    mods = _ada_modulation(cond, ada_w, ada_b)
    x_row = lambda b: b
    c_row = lambda b: bsz

    dft = {}
    for n in (seq, n_ctx):
        dft[("hy", n)] = _dft_mats(n, 2 * n)
        dft[("fn", n)] = _dft_mats(n, n)

    rwt = jnp.transpose(router_w).astype(F32)
    rb = jnp.broadcast_to(router_b.astype(F32)[:, None], (N_EXPERTS, 128))
    s_zero = jnp.zeros((bsz, nh, HG_HEAD_DIM, HG_HEAD_DIM), F32)

    tm_x = min(512, seq)
    tm_c = min(256, n_ctx)
    tb_x = min(256, seq)
    tb_c = min(256, n_ctx)
    segs = [(hg_lo, hg_hi, F32), (0, c_hyp, BF16), (hg_hi, hg_hi + c_fn, BF16)]

    def mixer(z_hy, z_fn, n, layer):
        cmat, smat = dft[("hy", n)]
        h = _hyena_filters(n, c_hy, hy_filt_w1[layer], hy_filt_b1[layer], hy_filt_w2[layer],
                           hy_filt_b2[layer], hy_filt_w3[layer], hy_filt_freq[layer])
        hs, hd = _filter_halves(h, hy_bias[layer])
        kr, ki, kn = _filter_spectrum(hs, hd, cmat, smat)
        y_hy = _hyena(z_hy, hy_conv_w[layer], hy_conv_b[layer], cmat, smat, kr, ki, kn, hy_norm_g[layer])
        fc, fs = dft[("fn", n)]
        y_fn = _fnet(z_fn, fc, fs, fn_w[layer], fn_b[layer], fn_norm_g[layer])
        return y_hy, y_fn

    for layer in range(depth):
        last = layer == depth - 1
        mod = mods[layer].reshape(n_rows, 1, N_MOD * d)
        w_in_b = w_in[layer].astype(BF16)
        w_out_b = w_out[layer].astype(BF16)
        wg = moe_w_gate[layer].astype(BF16)
        wu = moe_w_up[layer].astype(BF16)
        wd = moe_w_down[layer].astype(BF16)
        lb = lower_bounds[layer]
        x_pos = pos if layer == 0 else None

        if last:
            (zc_hg,) = _in_proj(ctx, None, norm1_g[layer], mod, c_row, w_in_b[:, hg_lo:hg_hi],
                                [(0, 5 * w_hg, F32)], tm_c)
        else:
            zc_hg, zc_hy, zc_fn = _in_proj(ctx, None, norm1_g[layer], mod, c_row, w_in_b, segs, tm_c)
        oc_f, oc_b, s_f, s_b = _hgrn2_scan(zc_hg, lb, s_zero, s_zero, tb_c)

        zx_hg, zx_hy, zx_fn = _in_proj(x, x_pos, norm1_g[layer], mod, x_row, w_in_b, segs, tm_x)
        ox_f, ox_b, _, _ = _hgrn2_scan(zx_hg, lb, s_f, s_b, tb_x)
        yx_hy, yx_fn = mixer(zx_hy, zx_fn, seq, layer)
        x_mid, hx2, comb_x = _out_proj(x, x_pos, yx_hy, ox_f, ox_b, zx_hg, yx_fn, w_out_b, hg_norm_g[layer],
                                       mod, x_row, norm2_g[layer], rwt, rb, tm_x)
        tm_moe = min(1024, seq)
        tiles_per_seq = seq // tm_moe
        x = _moe(hx2.reshape(bsz * seq, d), comb_x, wg, wu, wd, x_mid.reshape(bsz * seq, d), mod,
                 lambda i: i // tiles_per_seq, final_norm_g if last else None, tm_moe).reshape(bsz, seq, d)

        if not last:
            yc_hy, yc_fn = mixer(zc_hy, zc_fn, n_ctx, layer)
            c_mid, hc2, comb_c = _out_proj(ctx, None, yc_hy, oc_f, oc_b, zc_hg, yc_fn, w_out_b,
                                           hg_norm_g[layer], mod, c_row, norm2_g[layer], rwt, rb, tm_c)
            tm_mc = min(1024, bsz * n_ctx)
            ctx = _moe(hc2.reshape(bsz * n_ctx, d), comb_c, wg, wu, wd, c_mid.reshape(bsz * n_ctx, d), mod,
                       lambda i: bsz, None, tm_mc).reshape(bsz, n_ctx, d)
    return x
```

```python
import functools
import math

import numpy as np
import jax
import jax.numpy as jnp
from jax import lax
from jax.experimental import pallas as pl
from jax.experimental.pallas import tpu as pltpu

F32 = jnp.float32
BF16 = jnp.bfloat16
HIGHEST = lax.Precision.HIGHEST

GRID_W = 64
HY_ORDER = 2
HY_BANDS = 16
HY_DECAY_TARGET = 1e-2
HY_FAST_DECAY_PCT = 0.3
HY_SLOW_DECAY_PCT = 1.5
HG_HEAD_DIM = 128
FN_GROUPS = 4
N_EXPERTS = 16
N_GROUPS = 4
EXPERTS_PER_GROUP = N_EXPERTS // N_GROUPS
N_MOD = 6
RMS_EPS = 1e-6
POS_BASE = 10000.0

V7X_VMEM_LIMIT_BYTES = 56 * 1024 * 1024
SCAN_CHUNK = 64
SCAN_SUB = 16
SCAN_SAFE_LOG2_RANGE = 80.0
LOG2_E = 1.0 / math.log(2.0)
MOE_CAP = 192
DFT_SPLIT = 64
NEG_BIG = -1e30


def _cparams(sem):
    return pltpu.CompilerParams(dimension_semantics=sem, vmem_limit_bytes=V7X_VMEM_LIMIT_BYTES)


def _const_spec(shape):
    nd = len(shape)
    return pl.BlockSpec(shape, lambda *_: (0,) * nd, pipeline_mode=pl.Buffered(1))


def _silu(x):
    return x * jax.nn.sigmoid(x)


def _dot(a, b):
    return jnp.dot(a, b, preferred_element_type=F32)


def _dot_nt(a, b, precision=None):
    return lax.dot_general(a, b, (((1,), (1,)), ((), ())), precision=precision,
                           preferred_element_type=F32)


def _dot_tn(a, b):
    return lax.dot_general(a, b, (((0,), (0,)), ((), ())), preferred_element_type=F32)


def _ada_kernel(c_ref, w_ref, b_ref, o_ref):
    s = _silu(c_ref[...])
    o_ref[0] = jnp.dot(s, w_ref[0], precision=HIGHEST, preferred_element_type=F32) + b_ref[0]


def _ada_modulation(cond, ada_w, ada_b):
    depth, d, nd = ada_w.shape
    rows = cond.shape[0]
    tn = 1536
    return pl.pallas_call(
        _ada_kernel,
        out_shape=jax.ShapeDtypeStruct((depth, rows, nd), F32),
        grid=(depth, nd // tn),
        in_specs=[pl.BlockSpec((rows, d), lambda l, j: (0, 0)),
                  pl.BlockSpec((1, d, tn), lambda l, j: (l, 0, j)),
                  pl.BlockSpec((1, 1, tn), lambda l, j: (l, 0, j))],
        out_specs=pl.BlockSpec((1, rows, tn), lambda l, j: (l, 0, j)),
        compiler_params=_cparams(("parallel", "parallel")),
        name="ada_modulation",
    )(cond, ada_w, ada_b.reshape(depth, 1, nd))


def _inproj_kernel(*refs, add_pos, segs):
    if add_pos:
        x_ref, pos_ref, g_ref, sh_ref, sc_ref, w_ref = refs[:6]
        outs = refs[6:]
        x = x_ref[0] + pos_ref[...]
    else:
        x_ref, g_ref, sh_ref, sc_ref, w_ref = refs[:5]
        outs = refs[5:]
        x = x_ref[0]
    ms = jnp.mean(x * x, axis=-1, keepdims=True)
    h = x * lax.rsqrt(ms + RMS_EPS) * g_ref[...]
    h = (h * (1.0 + sc_ref[0]) + sh_ref[0]).astype(BF16)
    for o_ref, (a, b, _) in zip(outs, segs):
        o_ref[0] = _dot(h, w_ref[:, a:b]).astype(o_ref.dtype)


def _in_proj(x, pos, norm_g, mod, mod_row, w, segs, tm):
    bsz, seq, d = x.shape
    add_pos = pos is not None
    in_specs = [pl.BlockSpec((1, tm, d), lambda b, i: (b, i, 0))]
    args = [x]
    if add_pos:
        in_specs.append(pl.BlockSpec((tm, d), lambda b, i: (i, 0)))
        args.append(pos)
    in_specs += [pl.BlockSpec((1, d), lambda b, i: (0, 0)),
                 pl.BlockSpec((1, 1, d), lambda b, i: (mod_row(b), 0, 0)),
                 pl.BlockSpec((1, 1, d), lambda b, i: (mod_row(b), 0, 1)),
                 _const_spec(w.shape)]
    args += [norm_g.reshape(1, d), mod, mod, w]
    out_shape = [jax.ShapeDtypeStruct((bsz, seq, b - a), dt) for a, b, dt in segs]
    out_specs = [pl.BlockSpec((1, tm, b - a), lambda bb, i: (bb, i, 0)) for a, b, _ in segs]
    return pl.pallas_call(
        functools.partial(_inproj_kernel, add_pos=add_pos, segs=tuple(segs)),
        out_shape=out_shape, grid=(bsz, seq // tm), in_specs=in_specs, out_specs=out_specs,
        compiler_params=_cparams(("parallel", "parallel")),
        name="in_proj",
    )(*args)


def _cumsum_rows(x, reverse):
    c = x.shape[0]
    row = lax.broadcasted_iota(jnp.int32, x.shape, 0)
    sh = 1
    while sh < c:
        if reverse:
            x = x + jnp.where(row < c - sh, pltpu.roll(x, c - sh, 0), 0.0)
        else:
            x = x + jnp.where(row >= sh, pltpu.roll(x, sh, 0), 0.0)
        sh *= 2
    return x


def _scan_chunk_fast(q, lf2, k, v, st_ref, h, reverse):
    c = SCAN_CHUNK
    s = SCAN_SUB
    nsub = c // s
    b = _cumsum_rows(lf2, reverse)
    qs = _silu(q) * (HG_HEAD_DIM ** -0.5)
    st = st_ref[h]
    o_inter = _dot_nt((qs * jnp.exp2(b)).astype(BF16), st.astype(BF16))
    v_bf = v.astype(BF16)
    zero_row = jnp.zeros((1, HG_HEAD_DIM), F32)
    zero_blk = jnp.zeros((s, HG_HEAD_DIM), BF16)
    kt = {}
    prev_ref = None
    sc = [None] * nsub
    for i in (range(nsub - 1, -1, -1) if reverse else range(nsub)):
        r0 = i * s
        bi = b[r0:r0 + s]
        if reverse:
            ref = b[r0 + s:r0 + s + 1] if i < nsub - 1 else zero_row
        else:
            ref = b[r0 - 1:r0] if i > 0 else zero_row
        if prev_ref is not None:
            step = jnp.exp2(ref - prev_ref)
            kt = {j: blk * step for j, blk in kt.items()}
        kt[i] = k[r0:r0 + s] * jnp.exp2(ref - bi)
        prev_ref = ref
        qi = (qs[r0:r0 + s] * jnp.exp2(bi - ref)).astype(BF16)
        keys = jnp.concatenate([kt[j].astype(BF16) if j in kt else zero_blk for j in range(nsub)], axis=0)
        sc[i] = _dot_nt(qi, keys)
    scores = jnp.concatenate(sc, axis=0)
    rr = lax.broadcasted_iota(jnp.int32, (c, c), 0)
    cc = lax.broadcasted_iota(jnp.int32, (c, c), 1)
    scores = jnp.where((cc >= rr) if reverse else (cc <= rr), scores, 0.0)
    o = o_inter + _dot(scores.astype(BF16), v_bf)

    b_end = b[0:1] if reverse else b[c - 1:c]
    kd = (k * jnp.exp2(b_end - b)).astype(BF16)
    st_ref[h] = st * jnp.exp2(b_end) + _dot_tn(v_bf, kd)
    return o


def _scan_chunk_exact(q, lf2, k, v, st_ref, h, b_scr, q_scr, reverse):
    c = SCAN_CHUNK
    s = SCAN_SUB
    nsub = c // s
    b = _cumsum_rows(lf2, reverse)
    qs = _silu(q) * (HG_HEAD_DIM ** -0.5)
    st = st_ref[h]
    o_inter = _dot_nt((qs * jnp.exp2(b)).astype(BF16), st.astype(BF16))

    b_scr[...] = b
    q_scr[...] = qs
    v_bf = v.astype(BF16)
    ones = jnp.ones((HG_HEAD_DIM, HG_HEAD_DIM), BF16)
    sub_iota = lax.broadcasted_iota(jnp.int32, (s, HG_HEAD_DIM), 0)
    row_iota = lax.broadcasted_iota(jnp.int32, (c, HG_HEAD_DIM), 0)
    grp = (lax.broadcasted_iota(jnp.int32, (s, s * s), 1) // s ==
           lax.broadcasted_iota(jnp.int32, (s, s * s), 0)).astype(BF16)
    o_parts = []
    for i in range(nsub):
        r0 = i * s
        bi = b[r0:r0 + s]
        ki = k[r0:r0 + s]
        vi = v[r0:r0 + s]
        prods = []
        for t in range(s):
            bt = b_scr[pl.ds(r0 + t, 1), :]
            qt = q_scr[pl.ds(r0 + t, 1), :]
            keep = (sub_iota >= t) if reverse else (sub_iota <= t)
            e = jnp.exp2(jnp.where(keep, bt - bi, NEG_BIG))
            prods.append(((qt * ki) * e).astype(BF16))
        p = jnp.concatenate(prods, axis=0)
        rsum = _dot(p, ones)
        zt = (rsum * jnp.concatenate([vi] * s, axis=0)).astype(BF16)
        o_i = _dot(grp, zt)
        if reverse and i < nsub - 1:
            ref_row = b[r0 + s:r0 + s + 1]
            key_rows = row_iota >= r0 + s
        elif (not reverse) and i > 0:
            ref_row = b[r0 - 1:r0]
            key_rows = row_iota < r0
        else:
            ref_row = None
        if ref_row is not None:
            qi = qs[r0:r0 + s] * jnp.exp2(bi - ref_row)
            ks = k * jnp.exp2(jnp.where(key_rows, ref_row - b, NEG_BIG))
            sc = _dot_nt(qi.astype(BF16), ks.astype(BF16))
            o_i = o_i + _dot(sc.astype(BF16), v_bf)
        o_parts.append(o_i)
    o = o_inter + jnp.concatenate(o_parts, axis=0)

    b_end = b[0:1] if reverse else b[c - 1:c]
    kd = (k * jnp.exp2(b_end - b)).astype(BF16)
    st_ref[h] = st * jnp.exp2(b_end) + _dot_tn(v_bf, kd)
    return o


def _scan_kernel(qf_ref, qb_ref, ff_ref, fb_ref, vf_ref, vb_ref, lb_ref, s0f_ref, s0b_ref,
                 of_ref, ob_ref, sf_ref, sb_ref, stf, stb, lf_scr, k_scr, bf_scr, qf_scr, bb_scr, qb_scr,
                 *, n_heads, n_chunks):
    n = pl.program_id(1)

    @pl.when(n == 0)
    def _():
        stf[...] = s0f_ref[0]
        stb[...] = s0b_ref[0]

    tb = n_chunks * SCAN_CHUNK
    nblk = tb // SCAN_SUB
    sel = (lax.broadcasted_iota(jnp.int32, (nblk, tb), 1) // SCAN_SUB ==
           lax.broadcasted_iota(jnp.int32, (nblk, tb), 0)).astype(BF16)
    worst = jnp.zeros((nblk, lb_ref.shape[1]), F32)
    for d, f_ref in enumerate((ff_ref, fb_ref)):
        lb = lb_ref[d:d + 1, :]
        forget = lb + (1.0 - lb) * jax.nn.sigmoid(f_ref[0])
        lf2 = jnp.log(forget) * LOG2_E
        lf_scr[d] = lf2
        k_scr[d] = 1.0 - forget
        worst = jnp.maximum(worst, -_dot(sel, lf2.astype(BF16)))
    safe = jnp.max(worst) <= SCAN_SAFE_LOG2_RANGE

    def rows_of(ci):
        rf = pl.ds(pl.multiple_of(ci * SCAN_CHUNK, SCAN_CHUNK), SCAN_CHUNK)
        rb = pl.ds(pl.multiple_of((n_chunks - 1 - ci) * SCAN_CHUNK, SCAN_CHUNK), SCAN_CHUNK)
        return rf, rb

    @pl.when(safe)
    def _():
        def chunk_body(ci, carry):
            rf, rb = rows_of(ci)
            for h in range(n_heads):
                cols = slice(h * HG_HEAD_DIM, (h + 1) * HG_HEAD_DIM)
                of_ref[0, rf, cols] = _scan_chunk_fast(qf_ref[0, rf, cols], lf_scr[0, rf, cols],
                                                       k_scr[0, rf, cols], vf_ref[0, rf, cols], stf, h, False)
                ob_ref[0, rb, cols] = _scan_chunk_fast(qb_ref[0, rb, cols], lf_scr[1, rb, cols],
                                                       k_scr[1, rb, cols], vb_ref[0, rb, cols], stb, h, True)
            return carry

        lax.fori_loop(0, n_chunks, chunk_body, 0, unroll=True)

    @pl.when(jnp.logical_not(safe))
    def _():
        def head_body(h, carry):
            cols = pl.ds(pl.multiple_of(h * HG_HEAD_DIM, HG_HEAD_DIM), HG_HEAD_DIM)

            def chunk_body(ci, carry2):
                rf, rb = rows_of(ci)
                of_ref[0, rf, cols] = _scan_chunk_exact(qf_ref[0, rf, cols], lf_scr[0, rf, cols],
                                                        k_scr[0, rf, cols], vf_ref[0, rf, cols], stf, h,
                                                        bf_scr, qf_scr, False)
                ob_ref[0, rb, cols] = _scan_chunk_exact(qb_ref[0, rb, cols], lf_scr[1, rb, cols],
                                                        k_scr[1, rb, cols], vb_ref[0, rb, cols], stb, h,
                                                        bb_scr, qb_scr, True)
                return carry2

            return lax.fori_loop(0, n_chunks, chunk_body, carry)

        lax.fori_loop(0, n_heads, head_body, 0)

    @pl.when(n == pl.num_programs(1) - 1)
    def _():
        sf_ref[0] = stf[...]
        sb_ref[0] = stb[...]


def _hgrn2_scan(z_hg, lb, s0f, s0b, tb):
    bsz, seq, w5 = z_hg.shape
    w = w5 // 5
    nh = w // HG_HEAD_DIM
    nb = seq // tb
    blk = (1, tb, w)
    fwd = lambda j: pl.BlockSpec(blk, lambda b, n: (b, n, j))
    bwd = lambda j: pl.BlockSpec(blk, lambda b, n: (b, nb - 1 - n, j))
    st_spec = pl.BlockSpec((1, nh, HG_HEAD_DIM, HG_HEAD_DIM), lambda b, n: (b, 0, 0, 0))
    st_shape = jax.ShapeDtypeStruct((bsz, nh, HG_HEAD_DIM, HG_HEAD_DIM), F32)
    o_shape = jax.ShapeDtypeStruct((bsz, seq, w), F32)
    return pl.pallas_call(
        functools.partial(_scan_kernel, n_heads=nh, n_chunks=tb // SCAN_CHUNK),
        out_shape=[o_shape, o_shape, st_shape, st_shape],
        grid=(bsz, nb),
        in_specs=[fwd(0), bwd(0), fwd(1), bwd(2), fwd(3), bwd(3),
                  pl.BlockSpec((2, w), lambda b, n: (0, 0)), st_spec, st_spec],
        out_specs=[fwd(0), bwd(0), st_spec, st_spec],
        scratch_shapes=[pltpu.VMEM((nh, HG_HEAD_DIM, HG_HEAD_DIM), F32),
                        pltpu.VMEM((nh, HG_HEAD_DIM, HG_HEAD_DIM), F32),
                        pltpu.VMEM((2, tb, w), F32),
                        pltpu.VMEM((2, tb, w), F32),
                        pltpu.VMEM((SCAN_CHUNK, HG_HEAD_DIM), F32),
                        pltpu.VMEM((SCAN_CHUNK, HG_HEAD_DIM), F32),
                        pltpu.VMEM((SCAN_CHUNK, HG_HEAD_DIM), F32),
                        pltpu.VMEM((SCAN_CHUNK, HG_HEAD_DIM), F32)],
        compiler_params=_cparams(("parallel", "arbitrary")),
        name="hgrn2_scan",
    )(z_hg, z_hg, z_hg, z_hg, z_hg, z_hg, lb, s0f, s0b)


def _dft_mats(n_rows, period):
    t = jnp.arange(n_rows, dtype=jnp.int32)

    def table(mult, count):
        m = (jnp.arange(count, dtype=jnp.int32)[:, None] * mult * t[None, :]) % period
        ang = m.astype(F32) * (2.0 * math.pi / period)
        return jnp.cos(ang), jnp.sin(ang)

    n_hi = n_rows // DFT_SPLIT
    ca, sa = table(DFT_SPLIT, n_hi)
    cb, sb = table(1, DFT_SPLIT)
    cmat = ca[:, None, :] * cb[None, :, :] - sa[:, None, :] * sb[None, :, :]
    smat = sa[:, None, :] * cb[None, :, :] + ca[:, None, :] * sb[None, :, :]
    return cmat.reshape(n_rows, n_rows).astype(BF16), smat.reshape(n_rows, n_rows).astype(BF16)


def _split_bf16(x):
    hi = x.astype(BF16)
    lo = (x - hi.astype(F32)).astype(BF16)
    return hi, lo


def _filter_spectrum_kernel(hs_ref, hd_ref, c_ref, s_ref, kr_ref, ki_ref, kn_ref, *, scale):
    hs = hs_ref[...]
    hd = hd_ref[...]
    hs_hi, hs_lo = _split_bf16(hs)
    hd_hi, hd_lo = _split_bf16(hd)
    c = c_ref[...]
    s = s_ref[...]
    kr_ref[...] = ((_dot(c, hs_hi) + _dot(c, hs_lo)) * scale).astype(kr_ref.dtype)
    ki_ref[...] = ((_dot(s, hd_hi) + _dot(s, hd_lo)) * (-scale)).astype(ki_ref.dtype)
    row = lax.broadcasted_iota(jnp.int32, hs.shape, 0)
    sign = jnp.where(row % 2 == 0, 1.0, -1.0)
    kn = jnp.sum(hs * sign, axis=0, keepdims=True) * scale
    kn_ref[...] = jnp.broadcast_to(kn, kn_ref.shape)


def _filter_spectrum(hs, hd, cmat, smat):
    seq, ch = hs.shape
    scale = 2.0 / (2 * seq)
    return pl.pallas_call(
        functools.partial(_filter_spectrum_kernel, scale=scale),
        out_shape=[jax.ShapeDtypeStruct((seq, ch), BF16), jax.ShapeDtypeStruct((seq, ch), BF16),
                   jax.ShapeDtypeStruct((8, ch), F32)],
        compiler_params=pltpu.CompilerParams(vmem_limit_bytes=V7X_VMEM_LIMIT_BYTES),
        name="hyena_filter_spectrum",
    )(hs, hd, cmat, smat)


def _hyena_kernel(z_ref, cw_ref, cb_ref, c_ref, s_ref, kr_ref, ki_ref, kn_ref, g_ref, o_ref, *, seq, ch):
    row = lax.broadcasted_iota(jnp.int32, (seq, 1), 0)
    sign = jnp.where(row % 2 == 0, 1.0, -1.0)

    def short_conv(part):
        cols = slice(part * ch, (part + 1) * ch)
        z = z_ref[0, :, cols].astype(F32)
        z_prev = jnp.where(row >= 1, pltpu.roll(z, 1, 0), 0.0)
        z_next = jnp.where(row <= seq - 2, pltpu.roll(z, seq - 1, 0), 0.0)
        return z_prev * cw_ref[0:1, cols] + z * cw_ref[1:2, cols] + z_next * cw_ref[2:3, cols] + cb_ref[:, cols]

    def long_conv(x, order):
        cols = slice(order * ch, (order + 1) * ch)
        xb = x.astype(BF16)
        a = _dot(c_ref[...], xb)
        bm = _dot(s_ref[...], xb)
        kr = kr_ref[:, cols]
        ki = ki_ref[:, cols]
        yr = a * kr + bm * ki
        yi = (a * ki - bm * kr).astype(BF16)
        dc = 0.5 * yr[0:1, :]
        x_nyq = jnp.sum(x * sign, axis=0, keepdims=True)
        y = _dot(c_ref[...], yr.astype(BF16)) - _dot(s_ref[...], yi)
        return y - dc + (0.5 * x_nyq * kn_ref[0:1, cols]) * sign

    y = short_conv(1) * long_conv(short_conv(0), 0)
    y = short_conv(2) * long_conv(y, 1)
    ms = jnp.mean(y * y, axis=-1, keepdims=True)
    o_ref[0] = y * lax.rsqrt(ms + RMS_EPS) * g_ref[...]


def _hyena(z_hy, conv_w, conv_b, cmat, smat, kr, ki, kn, norm_g):
    bsz, seq, c3 = z_hy.shape
    ch = c3 // 3
    return pl.pallas_call(
        functools.partial(_hyena_kernel, seq=seq, ch=ch),
        out_shape=jax.ShapeDtypeStruct((bsz, seq, ch), F32),
        grid=(bsz,),
        in_specs=[pl.BlockSpec((1, seq, c3), lambda b: (b, 0, 0)),
                  _const_spec(conv_w.shape), _const_spec((1, c3)),
                  _const_spec(cmat.shape), _const_spec(smat.shape),
                  _const_spec(kr.shape), _const_spec(ki.shape), _const_spec(kn.shape),
                  _const_spec((1, ch))],
        out_specs=pl.BlockSpec((1, seq, ch), lambda b: (b, 0, 0)),
        compiler_params=_cparams(("parallel",)),
        name="hyena_mixer",
    )(z_hy, conv_w, conv_b.reshape(1, c3), cmat, smat, kr, ki, kn, norm_g.reshape(1, ch))


def _fnet_kernel(z_ref, c_ref, s_ref, bdc_ref, bds_ref, bdw_ref, b_ref, g_ref, o_ref, *, scale):
    zb = z_ref[0].astype(BF16)
    y1 = _dot(c_ref[...], zb)
    y2 = _dot(s_ref[...], zb)
    r = (_dot(y1.astype(BF16), bdc_ref[...]) - _dot(y2.astype(BF16), bds_ref[...])) * scale
    y = _dot(r.astype(BF16), bdw_ref[...]) + b_ref[...]
    ms = jnp.mean(y * y, axis=-1, keepdims=True)
    o_ref[0] = y * lax.rsqrt(ms + RMS_EPS) * g_ref[...]


def _block_diag(blocks):
    g, a, b = blocks.shape
    out = jnp.zeros((g * a, g * b), blocks.dtype)
    for i in range(g):
        out = out.at[i * a:(i + 1) * a, i * b:(i + 1) * b].set(blocks[i])
    return out


def _fnet(z_fn, cmat, smat, fn_w, fn_b, norm_g):
    bsz, seq, ch = z_fn.shape
    gd = ch // FN_GROUPS
    k = np.arange(gd)
    ang = 2.0 * np.pi * ((k[:, None] * k[None, :]) % gd) / gd
    eye = np.eye(FN_GROUPS)
    bdc = jnp.asarray(np.kron(eye, np.cos(ang)), BF16)
    bds = jnp.asarray(np.kron(eye, np.sin(ang)), BF16)
    bdw = _block_diag(fn_w).astype(BF16)
    scale = 1.0 / math.sqrt(seq * gd)
    return pl.pallas_call(
        functools.partial(_fnet_kernel, scale=scale),
        out_shape=jax.ShapeDtypeStruct((bsz, seq, ch), F32),
        grid=(bsz,),
        in_specs=[pl.BlockSpec((1, seq, ch), lambda b: (b, 0, 0)),
                  _const_spec(cmat.shape), _const_spec(smat.shape),
                  _const_spec((ch, ch)), _const_spec((ch, ch)), _const_spec((ch, ch)),
                  _const_spec((1, ch)), _const_spec((1, ch))],
        out_specs=pl.BlockSpec((1, seq, ch), lambda b: (b, 0, 0)),
        compiler_params=_cparams(("parallel",)),
        name="fnet_mixer",
    )(z_fn, cmat, smat, bdc, bds, bdw, fn_b.reshape(1, ch), norm_g.reshape(1, ch))


def _route(logits_t, rb_ref):
    rows = [logits_t[e:e + 1, :] for e in range(N_EXPERTS)]
    mx = functools.reduce(jnp.maximum, rows)
    ex = [jnp.exp(r - mx) for r in rows]
    inv = 1.0 / functools.reduce(lambda a, b: a + b, ex)
    probs = [e * inv for e in ex]
    sel = [probs[e] + rb_ref[e:e + 1, 0:1] for e in range(N_EXPERTS)]
    epg = EXPERTS_PER_GROUP
    gscore = []
    for g in range(N_GROUPS):
        s = sel[g * epg:(g + 1) * epg]
        pairs = [s[i] + s[j] for i in range(epg) for j in range(i + 1, epg)]
        gscore.append(functools.reduce(jnp.maximum, pairs))
    best = gscore[0]
    best_g = jnp.zeros_like(best, dtype=jnp.int32)
    for g in range(1, N_GROUPS):
        better = gscore[g] > best
        best_g = jnp.where(better, g, best_g)
        best = jnp.where(better, gscore[g], best)
    chosen = []
    for e in range(N_EXPERTS):
        g = e // epg
        beaten = jnp.zeros_like(best_g)
        for e2 in range(g * epg, (g + 1) * epg):
            if e2 == e:
                continue
            wins = (sel[e2] > sel[e]) | ((sel[e2] == sel[e]) & (e2 < e))
            beaten = beaten + wins.astype(jnp.int32)
        chosen.append(jnp.where((best_g == g) & (beaten < 2), probs[e], 0.0))
    tot = functools.reduce(lambda a, b: a + b, chosen)
    inv_tot = 1.0 / tot
    return jnp.concatenate([c * inv_tot for c in chosen], axis=0)


def _outproj_kernel(*refs, add_pos, n_heads):
    if add_pos:
        x_ref, pos_ref = refs[:2]
        rest = refs[2:]
        x = x_ref[0] + pos_ref[...]
    else:
        x_ref = refs[0]
        rest = refs[1:]
        x = x_ref[0]
    (yhy_ref, of_ref, ob_ref, g_ref, yfn_ref, wout_ref, hgg_ref, gate_ref, n2g_ref, sh_ref, sc_ref,
     rwt_ref, rb_ref, xo_ref, h_ref, comb_ref) = rest
    o = of_ref[0] + ob_ref[0]
    parts = []
    for h in range(n_heads):
        oh = o[:, h * HG_HEAD_DIM:(h + 1) * HG_HEAD_DIM]
        parts.append(oh * lax.rsqrt(jnp.mean(oh * oh, axis=-1, keepdims=True) + RMS_EPS))
    y_hg = jnp.concatenate(parts, axis=-1) * hgg_ref[...] * _silu(g_ref[0])
    c_hy = yhy_ref.shape[-1]
    c_hg = y_hg.shape[-1]
    mix = (_dot(yhy_ref[0].astype(BF16), wout_ref[0:c_hy, :]) +
           _dot(y_hg.astype(BF16), wout_ref[c_hy:c_hy + c_hg, :]) +
           _dot(yfn_ref[0].astype(BF16), wout_ref[c_hy + c_hg:, :]))
    xn = x + gate_ref[0] * mix
    xo_ref[0] = xn
    ms = jnp.mean(xn * xn, axis=-1, keepdims=True)
    h2 = xn * lax.rsqrt(ms + RMS_EPS) * n2g_ref[...]
    h2 = h2 * (1.0 + sc_ref[0]) + sh_ref[0]
    h_ref[0] = h2.astype(BF16)
    logits_t = _dot_nt(rwt_ref[...], h2, precision=HIGHEST)
    comb_ref[0] = _route(logits_t, rb_ref)


def _out_proj(x, pos, y_hy, o_f, o_b, z_hg, y_fn, w_out, hg_norm_g, mod, mod_row, norm2_g, rwt, rb, tm):
    bsz, seq, d = x.shape
    add_pos = pos is not None
    c_hy, c_hg, c_fn = y_hy.shape[-1], o_f.shape[-1], y_fn.shape[-1]
    tok = lambda c, j=0: pl.BlockSpec((1, tm, c), lambda b, i: (b, i, j))
    modk = lambda k: pl.BlockSpec((1, 1, d), lambda b, i: (mod_row(b), 0, k))
    in_specs = [tok(d)]
    args = [x]
    if add_pos:
        in_specs.append(pl.BlockSpec((tm, d), lambda b, i: (i, 0)))
        args.append(pos)
    in_specs += [tok(c_hy), tok(c_hg), tok(c_hg), tok(c_hg, 4), tok(c_fn),
                 _const_spec(w_out.shape), _const_spec((1, c_hg)), modk(2), _const_spec((1, d)),
                 modk(3), modk(4), _const_spec(rwt.shape), _const_spec(rb.shape)]
    args += [y_hy, o_f, o_b, z_hg, y_fn, w_out, hg_norm_g.reshape(1, c_hg), mod, norm2_g.reshape(1, d),
             mod, mod, rwt, rb]
    nt = seq // tm
    out_shape = [jax.ShapeDtypeStruct((bsz, seq, d), F32), jax.ShapeDtypeStruct((bsz, seq, d), BF16),
                 jax.ShapeDtypeStruct((bsz * nt, N_EXPERTS, tm), F32)]
    out_specs = [tok(d), tok(d), pl.BlockSpec((1, N_EXPERTS, tm), lambda b, i: (b * nt + i, 0, 0))]
    return pl.pallas_call(
        functools.partial(_outproj_kernel, add_pos=add_pos, n_heads=c_hg // HG_HEAD_DIM),
        out_shape=out_shape, grid=(bsz, nt), in_specs=in_specs, out_specs=out_specs,
        compiler_params=_cparams(("parallel", "parallel")),
        name="out_proj_router",
    )(*args)


def _moe_kernel(*refs, final, cap):
    if final:
        (cnt_ref, h_ref, comb_ref, combt_ref, wg_ref, wu_ref, wd_ref, x_ref, gate_ref, fg_ref, o_ref,
         acc_ref, rcol_ref, rrow_ref) = refs
    else:
        (cnt_ref, h_ref, comb_ref, combt_ref, wg_ref, wu_ref, wd_ref, x_ref, gate_ref, o_ref,
         acc_ref, rcol_ref, rrow_ref) = refs
    e = pl.program_id(1)
    tm = h_ref.shape[0]
    n_rt = combt_ref.shape[0]

    @pl.when(e == 0)
    def _():
        acc_ref[...] = jnp.zeros_like(acc_ref)
        ti = lax.broadcasted_iota(jnp.int32, (tm, tm), 0)
        tj = lax.broadcasted_iota(jnp.int32, (tm, tm), 1)
        before_col = jnp.where(tj < ti, 1.0, 0.0).astype(BF16)
        before_row = jnp.where(ti < tj, 1.0, 0.0).astype(BF16)
        sel_col = jnp.where(comb_ref[...] > 0.0, 1.0, 0.0).astype(BF16)
        combt = jnp.concatenate([combt_ref[j] for j in range(n_rt)], axis=1)
        sel_row = jnp.where(combt > 0.0, 1.0, 0.0).astype(BF16)
        rcol_ref[...] = _dot(before_col, sel_col)
        rrow_ref[...] = _dot(sel_row, before_row)

    comb = comb_ref[...]
    lane = lax.broadcasted_iota(jnp.int32, comb.shape, 1)
    ce = jnp.sum(jnp.where(lane == e, comb, 0.0), axis=-1, keepdims=True)
    fits = cnt_ref[pl.program_id(0), e] <= cap

    @pl.when(fits)
    def _():
        h = h_ref[...]
        ce_row = jnp.concatenate([combt_ref[j, pl.ds(e, 1), :] for j in range(n_rt)], axis=1)
        rank_col = jnp.sum(jnp.where(lane == e, rcol_ref[...], 0.0), axis=-1, keepdims=True)
        rank_row = rrow_ref[pl.ds(e, 1), :]
        slot_r = lax.broadcasted_iota(jnp.int32, (cap, tm), 0).astype(F32)
        pick = jnp.where((slot_r == rank_row) & (ce_row > 0.0), 1.0, 0.0).astype(BF16)
        hc = _dot(pick, h).astype(BF16)
        a = _silu(_dot(hc, wg_ref[0])) * _dot(hc, wu_ref[0])
        y = _dot(a.astype(BF16), wd_ref[0]).astype(BF16)
        slot_c = lax.broadcasted_iota(jnp.int32, (tm, cap), 1).astype(F32)
        spread = jnp.where((slot_c == rank_col) & (ce > 0.0), 1.0, 0.0).astype(BF16)
        acc_ref[...] += ce * _dot(spread, y)

    @pl.when(jnp.logical_not(fits))
    def _():
        h = h_ref[...]
        a = _silu(_dot(h, wg_ref[0])) * _dot(h, wu_ref[0])
        acc_ref[...] += ce * _dot(a.astype(BF16), wd_ref[0])

    @pl.when(e == pl.num_programs(1) - 1)
    def _():
        y = x_ref[...] + gate_ref[0] * acc_ref[...]
        if final:
            ms = jnp.mean(y * y, axis=-1, keepdims=True)
            y = y * lax.rsqrt(ms + RMS_EPS) * fg_ref[...]
        o_ref[...] = y


def _moe(h, comb_t, wg, wu, wd, x, mod, mod_row_of_tile, final_g, tm):
    n, d = h.shape
    ne, _, de = wg.shape
    final = final_g is not None
    tr = comb_t.shape[-1]
    n_rt = tm // tr
    lanes = 128
    comb = jnp.transpose(comb_t, (0, 2, 1)).reshape(n, ne)
    comb = jnp.pad(comb, ((0, 0), (0, lanes - ne)))
    cap = min(MOE_CAP, tm)
    counts = jnp.sum((comb_t > 0.0).reshape(n // tm, n_rt, ne, tr), axis=(1, 3)).astype(jnp.int32)
    in_specs = [pl.BlockSpec((tm, d), lambda i, e, c: (i, 0)),
                pl.BlockSpec((tm, lanes), lambda i, e, c: (i, 0)),
                pl.BlockSpec((n_rt, ne, tr), lambda i, e, c: (i, 0, 0)),
                pl.BlockSpec((1, d, de), lambda i, e, c: (e, 0, 0)),
                pl.BlockSpec((1, d, de), lambda i, e, c: (e, 0, 0)),
                pl.BlockSpec((1, de, d), lambda i, e, c: (e, 0, 0)),
                pl.BlockSpec((tm, d), lambda i, e, c: (i, 0)),
                pl.BlockSpec((1, 1, d), lambda i, e, c: (mod_row_of_tile(i), 0, 5))]
    args = [counts, h, comb, comb_t, wg, wu, wd, x, mod]
    if final:
        in_specs.append(pl.BlockSpec((1, d), lambda i, e, c: (0, 0)))
        args.append(final_g.reshape(1, d))
    return pl.pallas_call(
        functools.partial(_moe_kernel, final=final, cap=cap),
        out_shape=jax.ShapeDtypeStruct((n, d), F32),
        grid_spec=pltpu.PrefetchScalarGridSpec(
            num_scalar_prefetch=1, grid=(n // tm, ne), in_specs=in_specs,
            out_specs=pl.BlockSpec((tm, d), lambda i, e, c: (i, 0)),
            scratch_shapes=[pltpu.VMEM((tm, d), F32), pltpu.VMEM((tm, lanes), F32),
                            pltpu.VMEM((ne, tm), F32)]),
        compiler_params=_cparams(("parallel", "arbitrary")),
        name="moe_experts",
    )(*args)


def _grid_sincos(n_tok, d):
    rows = n_tok // GRID_W
    row = jnp.repeat(jnp.arange(rows, dtype=F32), GRID_W)
    col = jnp.tile(jnp.arange(GRID_W, dtype=F32), rows)
    quarter = d // 4
    omega = 1.0 / (POS_BASE ** (jnp.arange(quarter, dtype=F32) / quarter))
    ar = row[:, None] * omega
    ac = col[:, None] * omega
    return jnp.concatenate([jnp.sin(ar), jnp.cos(ar), jnp.sin(ac), jnp.cos(ac)], axis=-1)


def _hyena_filters(seq, ch, w1, b1, w2, b2, w3, freq):
    t = jnp.linspace(0.0, 1.0, seq, dtype=F32)[:, None]
    w = 2.0 * math.pi * jnp.arange(seq, dtype=F32)[:, None] / seq
    bands = jnp.linspace(1e-4, HY_BANDS - 1, HY_BANDS, dtype=F32)[None, :]
    z = jnp.concatenate([t, jnp.cos(bands * w), -jnp.sin(bands * w)], axis=-1)
    hp = functools.partial(jnp.dot, precision=HIGHEST)
    h = jnp.sin(freq * (hp(z, w1) + b1))
    h = jnp.sin(freq * (hp(h, w2) + b2))
    h = hp(h, w3).reshape(seq, 2 * HY_ORDER, ch)
    max_decay = math.log(HY_DECAY_TARGET) / HY_FAST_DECAY_PCT
    min_decay = math.log(HY_DECAY_TARGET) / HY_SLOW_DECAY_PCT
    deltas = jnp.linspace(min_decay, max_decay, ch, dtype=F32)
    return h * jnp.exp(-t[:, :, None] * jnp.abs(deltas))


def _filter_halves(h, bias):
    seq = h.shape[0]
    not0 = (jnp.arange(seq) > 0)[:, None].astype(F32)
    hs, hd = [], []
    for o in range(HY_ORDER):
        hf = h[:, 2 * o].at[0].add(bias[o])
        hb = h[:, 2 * o + 1] * not0
        hs.append(hf + hb)
        hd.append(hf - hb)
    return jnp.concatenate(hs, axis=-1), jnp.concatenate(hd, axis=-1)


def kernel(x, c, ctx, c_ctx, ada_w, ada_b, norm1_g, norm2_g, w_in, w_out, hy_conv_w, hy_conv_b, hy_filt_w1, hy_filt_b1, hy_filt_w2, hy_filt_b2, hy_filt_w3, hy_filt_freq, hy_bias, hy_norm_g, hg_lower_bounds, hg_norm_g, fn_w, fn_b, fn_norm_g, router_w, router_b, moe_w_gate, moe_w_up, moe_w_down, final_norm_g):
    bsz, seq, d = x.shape
    n_ctx = ctx.shape[1]
    depth = ada_w.shape[0]
    c_hy = hy_norm_g.shape[-1]
    c_hyp = hy_conv_w.shape[-1]
    w_hg = hg_norm_g.shape[-1]
    c_fn = fn_norm_g.shape[-1]
    hg_lo, hg_hi = c_hyp, c_hyp + 5 * w_hg
    nh = w_hg // HG_HEAD_DIM

    cs = jnp.cumsum(jax.nn.softmax(hg_lower_bounds.astype(F32), axis=0), axis=0)
    lower_bounds = cs - cs[0:1]
    pos = _grid_sincos(seq, d)

    n_rows = -(-(bsz + 1) // 8) * 8
    cond = jnp.zeros((n_rows, d), F32).at[:bsz].set(c).at[bsz].set(c_ctx)
    mods = _ada_modulation(cond, ada_w, ada_b)
    x_row = lambda b: b
    c_row = lambda b: bsz

    dft = {}
    for n in (seq, n_ctx):
        dft[("hy", n)] = _dft_mats(n, 2 * n)
        dft[("fn", n)] = _dft_mats(n, n)

    rwt = jnp.transpose(router_w).astype(F32)
    rb = jnp.broadcast_to(router_b.astype(F32)[:, None], (N_EXPERTS, 128))
    s_zero = jnp.zeros((bsz, nh, HG_HEAD_DIM, HG_HEAD_DIM), F32)

    tm_x = min(512, seq)
    tm_c = min(256, n_ctx)
    tb_x = min(256, seq)
    tb_c = min(256, n_ctx)
    segs = [(hg_lo, hg_hi, F32), (0, c_hyp, BF16), (hg_hi, hg_hi + c_fn, BF16)]

    def mixer(z_hy, z_fn, n, layer):
        cmat, smat = dft[("hy", n)]
        h = _hyena_filters(n, c_hy, hy_filt_w1[layer], hy_filt_b1[layer], hy_filt_w2[layer],
                           hy_filt_b2[layer], hy_filt_w3[layer], hy_filt_freq[layer])
        hs, hd = _filter_halves(h, hy_bias[layer])
        kr, ki, kn = _filter_spectrum(hs, hd, cmat, smat)
        y_hy = _hyena(z_hy, hy_conv_w[layer], hy_conv_b[layer], cmat, smat, kr, ki, kn, hy_norm_g[layer])
        fc, fs = dft[("fn", n)]
        y_fn = _fnet(z_fn, fc, fs, fn_w[layer], fn_b[layer], fn_norm_g[layer])
        return y_hy, y_fn

    for layer in range(depth):
        last = layer == depth - 1
        mod = mods[layer].reshape(n_rows, 1, N_MOD * d)
        w_in_b = w_in[layer].astype(BF16)
        w_out_b = w_out[layer].astype(BF16)
        wg = moe_w_gate[layer].astype(BF16)
        wu = moe_w_up[layer].astype(BF16)
        wd = moe_w_down[layer].astype(BF16)
        lb = lower_bounds[layer]
        x_pos = pos if layer == 0 else None

        if last:
            (zc_hg,) = _in_proj(ctx, None, norm1_g[layer], mod, c_row, w_in_b[:, hg_lo:hg_hi],
                                [(0, 5 * w_hg, F32)], tm_c)
        else:
            zc_hg, zc_hy, zc_fn = _in_proj(ctx, None, norm1_g[layer], mod, c_row, w_in_b, segs, tm_c)
        oc_f, oc_b, s_f, s_b = _hgrn2_scan(zc_hg, lb, s_zero, s_zero, tb_c)

        zx_hg, zx_hy, zx_fn = _in_proj(x, x_pos, norm1_g[layer], mod, x_row, w_in_b, segs, tm_x)
        ox_f, ox_b, _, _ = _hgrn2_scan(zx_hg, lb, s_f, s_b, tb_x)
        yx_hy, yx_fn = mixer(zx_hy, zx_fn, seq, layer)
        x_mid, hx2, comb_x = _out_proj(x, x_pos, yx_hy, ox_f, ox_b, zx_hg, yx_fn, w_out_b, hg_norm_g[layer],
                                       mod, x_row, norm2_g[layer], rwt, rb, tm_x)
        tm_moe = min(1024, seq)
        tiles_per_seq = seq // tm_moe
        x = _moe(hx2.reshape(bsz * seq, d), comb_x, wg, wu, wd, x_mid.reshape(bsz * seq, d), mod,
                 lambda i: i // tiles_per_seq, final_norm_g if last else None, tm_moe).reshape(bsz, seq, d)

        if not last:
            yc_hy, yc_fn = mixer(zc_hy, zc_fn, n_ctx, layer)
            c_mid, hc2, comb_c = _out_proj(ctx, None, yc_hy, oc_f, oc_b, zc_hg, yc_fn, w_out_b,
                                           hg_norm_g[layer], mod, c_row, norm2_g[layer], rwt, rb, tm_c)
            tm_mc = min(1024, bsz * n_ctx)
            ctx = _moe(hc2.reshape(bsz * n_ctx, d), comb_c, wg, wu, wd, c_mid.reshape(bsz * n_ctx, d), mod,
                       lambda i: bsz, None, tm_mc).reshape(bsz, n_ctx, d)
    return x
```
